```python
import jax, jax.numpy as jnp
from jax import lax
import numpy as np

D_MODEL = 1024
BATCH = 16
SEQ = 4096
DEPTH = 1
DEC_BATCH = 128
DEC_SEQ = 4
PAST_LEN = 8192
PAGE_SIZE = 128

HEAD_DIM = 64
FOX_HEADS = 8
FOX_KV_HEADS = 4
DSA_HEADS = 8
DSA_KV_HEADS = 2
IDX_HEADS = 8
IDX_DIM = 64
TOPK_MAX = 256
ROPE_THETA = 500000.0
ROPE_FRACTION_DIV = 4
D_FF = 4 * D_MODEL
Q_BLOCK = 128
NORM_EPS = 1e-6
FORGET_BIAS_INIT = 3.0

FOX_Q = FOX_HEADS * HEAD_DIM
FOX_KV = FOX_KV_HEADS * HEAD_DIM
DSA_Q = DSA_HEADS * HEAD_DIM
DSA_KV = DSA_KV_HEADS * HEAD_DIM
IDX_Q = IDX_HEADS * IDX_DIM
IN_SIZES = (FOX_Q, FOX_KV, FOX_KV, FOX_HEADS, DSA_Q, DSA_KV, DSA_KV, IDX_Q, IDX_DIM, IDX_HEADS, D_MODEL, D_MODEL)
N_IN = FOX_Q + 2 * FOX_KV + FOX_HEADS + DSA_Q + 2 * DSA_KV + IDX_Q + IDX_DIM + IDX_HEADS + 2 * D_MODEL

kernel_name = 'fox_dsa_gated_hybrid_step'


def rmsnorm(x, g):
    xf = x.astype(jnp.float32)
    y = xf * lax.rsqrt(jnp.mean(xf * xf, axis=-1, keepdims=True) + NORM_EPS)
    return (y * g.astype(jnp.float32)).astype(x.dtype)


def partial_rope(x, pos):
    r = x.shape[-1] // ROPE_FRACTION_DIV
    half = r // 2
    inv_freq = jnp.float32(ROPE_THETA) ** (-jnp.arange(half, dtype=jnp.float32) * (2.0 / r))
    ang = pos.astype(jnp.float32)[:, None] * inv_freq[None, :]
    cos = jnp.cos(ang)[None, :, None, :]
    sin = jnp.sin(ang)[None, :, None, :]
    xr = x[..., :r].astype(jnp.float32)
    x1, x2 = xr[..., :half], xr[..., half:]
    rot = jnp.concatenate([x1 * cos - x2 * sin, x2 * cos + x1 * sin], axis=-1).astype(x.dtype)
    return jnp.concatenate([rot, x[..., r:]], axis=-1)


def project_branches(n, w_in, b_forget, pos):
    B, T, _ = n.shape
    z = jnp.einsum('btd,de->bte', n, w_in)
    cuts = [int(c) for c in np.cumsum(IN_SIZES)[:-1]]
    fq, fk, fv, ff, dq, dk, dv, iq, ik, iw, ga, gb = jnp.split(z, cuts, axis=-1)
    fq = fq.reshape(B, T, FOX_HEADS, HEAD_DIM)
    fk = fk.reshape(B, T, FOX_KV_HEADS, HEAD_DIM)
    fv = fv.reshape(B, T, FOX_KV_HEADS, HEAD_DIM)
    logf = jax.nn.log_sigmoid((ff + b_forget).astype(jnp.float32))
    dq = partial_rope(dq.reshape(B, T, DSA_HEADS, HEAD_DIM), pos)
    dk = partial_rope(dk.reshape(B, T, DSA_KV_HEADS, HEAD_DIM), pos)
    dv = dv.reshape(B, T, DSA_KV_HEADS, HEAD_DIM)
    iq = partial_rope(iq.reshape(B, T, IDX_HEADS, IDX_DIM), pos)
    ik = partial_rope(ik[:, :, None, :], pos)[:, :, 0, :]
    return fq, fk, fv, logf, dq, dk, dv, iq, ik, iw, jax.nn.sigmoid(ga), jax.nn.sigmoid(gb)


def fox_attend(q, k, v, c_q, c_k, q_pos, k_pos):
    B, Q, H, Dh = q.shape
    S, KVH = k.shape[1], k.shape[2]
    G = H // KVH
    qg = q.reshape(B, Q, KVH, G, Dh)
    s = jnp.einsum('bqngd,bsnd->bngqs', qg, k).astype(jnp.float32) * (Dh ** -0.5)
    decay = (c_q.reshape(B, Q, KVH, G).transpose(0, 2, 3, 1)[..., :, None]
             - c_k.reshape(B, S, KVH, G).transpose(0, 2, 3, 1)[..., None, :])
    causal = k_pos[None, :] <= q_pos[:, None]
    s = jnp.where(causal, s + decay, -jnp.inf)
    p = jax.nn.softmax(s, axis=-1).astype(v.dtype)
    o = jnp.einsum('bngqs,bsnd->bqngd', p, v)
    return o.reshape(B, Q, H * Dh)


def fox_prompt(fq, fk, fv, logf):
    B, T = fq.shape[:2]
    nb = T // Q_BLOCK
    c = jnp.cumsum(logf, axis=1)
    k_pos = jnp.arange(T)

    def block(args):
        q, cq, start = args
        return fox_attend(q, fk, fv, cq, c, start + jnp.arange(Q_BLOCK), k_pos)

    qb = fq.reshape(B, nb, Q_BLOCK, FOX_HEADS, HEAD_DIM).swapaxes(0, 1)
    cb = c.reshape(B, nb, Q_BLOCK, FOX_HEADS).swapaxes(0, 1)
    o = lax.map(block, (qb, cb, jnp.arange(nb) * Q_BLOCK))
    return o.swapaxes(0, 1).reshape(B, T, FOX_Q)


def gather_past(pool, page_table):
    g = pool[page_table]
    return g.reshape((page_table.shape[0], page_table.shape[1] * PAGE_SIZE) + pool.shape[2:])


def fox_sample(fq, fk, fv, logf, k_pool, v_pool, lf_pool, page_table):
    Q = fq.shape[1]
    past = page_table.shape[1] * PAGE_SIZE
    k_all = jnp.concatenate([gather_past(k_pool, page_table), fk], axis=1)
    v_all = jnp.concatenate([gather_past(v_pool, page_table), fv], axis=1)
    lf_all = jnp.concatenate([gather_past(lf_pool, page_table).astype(jnp.float32), logf], axis=1)
    c = jnp.cumsum(lf_all, axis=1)
    q_pos = past + jnp.arange(Q)
    k_pos = jnp.arange(past + Q)
    return fox_attend(fq, k_all, v_all, c[:, past:], c, q_pos, k_pos)


def index_scores(iq, iw, ik, q_pos, k_pos):
    dots = jnp.einsum('bqhd,bsd->bqhs', iq, ik).astype(jnp.float32) * (IDX_DIM ** -0.5)
    sc = jnp.einsum('bqh,bqhs->bqs', iw.astype(jnp.float32) * (IDX_HEADS ** -0.5), jax.nn.relu(dots))
    return jnp.where(k_pos[None, None, :] <= q_pos[None, :, None], sc, -jnp.inf)


def dsa_attend(q, k_sel, v_sel, valid):
    B, Q, H, Dh = q.shape
    KVH = k_sel.shape[3]
    G = H // KVH
    qg = q.reshape(B, Q, KVH, G, Dh)
    s = jnp.einsum('bqngd,bqknd->bqngk', qg, k_sel).astype(jnp.float32) * (Dh ** -0.5)
    s = jnp.where(valid[:, :, None, None, :], s, -jnp.inf)
    p = jax.nn.softmax(s, axis=-1).astype(v_sel.dtype)
    return jnp.einsum('bqngk,bqknd->bqngd', p, v_sel).reshape(B, Q, H * Dh)


def take_rows(rows, idx):
    return jax.vmap(lambda r, i: r[i])(rows, idx)


def dsa_prompt(dq, dk, dv, iq, ik, iw):
    B, T = dq.shape[:2]
    nb = T // Q_BLOCK
    n_sel = min(TOPK_MAX, T // 4)
    k_pos = jnp.arange(T)

    def to_blocks(a):
        return a.reshape((B, nb, Q_BLOCK) + a.shape[2:]).swapaxes(0, 1)

    def block(args):
        q, qi, wi, start = args
        q_pos = start + jnp.arange(Q_BLOCK)
        sc = index_scores(qi, wi, ik, q_pos, k_pos)
        _, sel = lax.top_k(sc, n_sel)
        valid = sel <= q_pos[None, :, None]
        return dsa_attend(q, take_rows(dk, sel), take_rows(dv, sel), valid)

    o = lax.map(block, (to_blocks(dq), to_blocks(iq), to_blocks(iw), jnp.arange(nb) * Q_BLOCK))
    return o.swapaxes(0, 1).reshape(B, T, DSA_Q)


def gather_selected(pool, new_rows, page_table, sel, past):
    db = sel.shape[0]
    is_past = sel < past
    p = jnp.minimum(sel, past - 1)
    phys = jnp.take_along_axis(page_table, (p // PAGE_SIZE).reshape(db, -1), axis=1).reshape(sel.shape)
    past_rows = pool[phys, p % PAGE_SIZE]
    new_sel = take_rows(new_rows, jnp.clip(sel - past, 0, new_rows.shape[1] - 1))
    mask = is_past.reshape(sel.shape + (1,) * (past_rows.ndim - sel.ndim))
    return jnp.where(mask, past_rows, new_sel)


def dsa_sample(dq, dk, dv, iq, ik, iw, k_pool, v_pool, ik_pool, page_table):
    Q = dq.shape[1]
    past = page_table.shape[1] * PAGE_SIZE
    L = past + Q
    ik_all = jnp.concatenate([gather_past(ik_pool, page_table), ik], axis=1)
    q_pos = past + jnp.arange(Q)
    sc = index_scores(iq, iw, ik_all, q_pos, jnp.arange(L))
    _, sel = lax.top_k(sc, min(TOPK_MAX, L // 4))
    valid = sel <= q_pos[None, :, None]
    k_sel = gather_selected(k_pool, dk, page_table, sel, past)
    v_sel = gather_selected(v_pool, dv, page_table, sel, past)
    return dsa_attend(dq, k_sel, v_sel, valid)


def prompt_mixer(fq, fk, fv, logf, dq, dk, dv, iq, ik, iw):
    return fox_prompt(fq, fk, fv, logf), dsa_prompt(dq, dk, dv, iq, ik, iw)


def layer_forward(x, pos, mixer, norm1_g, w_in, b_forget, w_o_fox, w_o_dsa, w_out, norm2_g, w_up, w_down):
    n = rmsnorm(x, norm1_g)
    fq, fk, fv, logf, dq, dk, dv, iq, ik, iw, gate_a, gate_b = project_branches(n, w_in, b_forget, pos)
    o_fox, o_dsa = mixer(fq, fk, fv, logf, dq, dk, dv, iq, ik, iw)
    branch_a = jnp.einsum('btk,kd->btd', o_fox, w_o_fox)
    branch_b = jnp.einsum('btk,kd->btd', o_dsa, w_o_dsa)
    h = x + jnp.einsum('btd,de->bte', gate_a * branch_a + gate_b * branch_b, w_out)
    u = jax.nn.relu(jnp.einsum('btd,df->btf', rmsnorm(h, norm2_g), w_up))
    y = h + jnp.einsum('btf,fd->btd', u * u, w_down)
    return y, (fk, fv, logf, dk, dv, ik)


def stack_layers(rows, i, dtype):
    return jnp.stack([r[i] for r in rows], axis=0).astype(dtype)


def setup_inputs(seed: int = 0) -> dict:
    key = jax.random.key(seed)
    ks = jax.random.split(key, 20)
    n_pages = PAST_LEN // PAGE_SIZE
    n_used = DEC_BATCH * n_pages
    n_phys = n_used + n_used // 4

    def nrm(k, shape, scale=1.0):
        return jax.random.normal(k, shape, jnp.float32) * scale

    page_table = jax.random.permutation(ks[0], n_phys)[:n_used].reshape(DEC_BATCH, n_pages).astype(jnp.int32)
    pool = (DEPTH, n_phys, PAGE_SIZE)
    return {
        'x_prompt': nrm(ks[1], (BATCH, SEQ, D_MODEL)),
        'x_sample': nrm(ks[2], (DEC_BATCH, DEC_SEQ, D_MODEL)),
        'cache_fox_k': nrm(ks[3], pool + (FOX_KV_HEADS, HEAD_DIM)),
        'cache_fox_v': nrm(ks[4], pool + (FOX_KV_HEADS, HEAD_DIM)),
        'cache_fox_logf': jax.nn.log_sigmoid(FORGET_BIAS_INIT + nrm(ks[5], pool + (FOX_HEADS,))),
        'cache_dsa_k': nrm(ks[6], pool + (DSA_KV_HEADS, HEAD_DIM)),
        'cache_dsa_v': nrm(ks[7], pool + (DSA_KV_HEADS, HEAD_DIM)),
        'cache_idx_k': nrm(ks[8], pool + (IDX_DIM,)),
        'page_table': page_table,
        'norm1_g': 1.0 + nrm(ks[9], (DEPTH, D_MODEL), 0.02),
        'w_in': nrm(ks[10], (DEPTH, D_MODEL, N_IN), D_MODEL ** -0.5),
        'b_forget': FORGET_BIAS_INIT + nrm(ks[11], (DEPTH, FOX_HEADS), 0.1),
        'w_o_fox': nrm(ks[12], (DEPTH, FOX_Q, D_MODEL), FOX_Q ** -0.5),
        'w_o_dsa': nrm(ks[13], (DEPTH, DSA_Q, D_MODEL), DSA_Q ** -0.5),
        'w_out': nrm(ks[14], (DEPTH, D_MODEL, D_MODEL), D_MODEL ** -0.5),
        'norm2_g': 1.0 + nrm(ks[15], (DEPTH, D_MODEL), 0.02),
        'w_up': nrm(ks[16], (DEPTH, D_MODEL, D_FF), D_MODEL ** -0.5),
        'w_down': nrm(ks[17], (DEPTH, D_FF, D_MODEL), D_FF ** -0.5),
        'final_norm_g': 1.0 + nrm(ks[18], (D_MODEL,), 0.02),
    }


def reference(x_prompt, x_sample, cache_fox_k, cache_fox_v, cache_fox_logf, cache_dsa_k, cache_dsa_v, cache_idx_k,
              page_table, norm1_g, w_in, b_forget, w_o_fox, w_o_dsa, w_out, norm2_g, w_up, w_down, final_norm_g):
    past = page_table.shape[1] * PAGE_SIZE
    pos_p = jnp.arange(x_prompt.shape[1])
    pos_s = past + jnp.arange(x_sample.shape[1])
    xp, xs = x_prompt, x_sample
    rows_p, rows_s = [], []
    for l in range(DEPTH):
        w = (norm1_g[l], w_in[l], b_forget[l], w_o_fox[l], w_o_dsa[l], w_out[l], norm2_g[l], w_up[l], w_down[l])
        xp, new_p = layer_forward(xp, pos_p, prompt_mixer, *w)

        def sample_mixer(fq, fk, fv, logf, dq, dk, dv, iq, ik, iw, l=l):
            o_fox = fox_sample(fq, fk, fv, logf, cache_fox_k[l], cache_fox_v[l], cache_fox_logf[l], page_table)
            o_dsa = dsa_sample(dq, dk, dv, iq, ik, iw, cache_dsa_k[l], cache_dsa_v[l], cache_idx_k[l], page_table)
            return o_fox, o_dsa

        xs, new_s = layer_forward(xs, pos_s, sample_mixer, *w)
        rows_p.append(new_p)
        rows_s.append(new_s)
    y_prompt = rmsnorm(xp, final_norm_g)
    y_sample = rmsnorm(xs, final_norm_g)
    fox_k_p = stack_layers(rows_p, 0, cache_fox_k.dtype)
    fox_v_p = stack_layers(rows_p, 1, cache_fox_v.dtype)
    fox_logf_p = stack_layers(rows_p, 2, cache_fox_logf.dtype)
    dsa_k_p = stack_layers(rows_p, 3, cache_dsa_k.dtype)
    dsa_v_p = stack_layers(rows_p, 4, cache_dsa_v.dtype)
    idx_k_p = stack_layers(rows_p, 5, cache_idx_k.dtype)
    fox_k_s = stack_layers(rows_s, 0, cache_fox_k.dtype)
    fox_v_s = stack_layers(rows_s, 1, cache_fox_v.dtype)
    fox_logf_s = stack_layers(rows_s, 2, cache_fox_logf.dtype)
    dsa_k_s = stack_layers(rows_s, 3, cache_dsa_k.dtype)
    dsa_v_s = stack_layers(rows_s, 4, cache_dsa_v.dtype)
    idx_k_s = stack_layers(rows_s, 5, cache_idx_k.dtype)
    return (y_prompt, y_sample, fox_k_p, fox_v_p, fox_logf_p, dsa_k_p, dsa_v_p, idx_k_p,
            fox_k_s, fox_v_s, fox_logf_s, dsa_k_s, dsa_v_s, idx_k_s)
```

```python
import functools

import jax
import jax.numpy as jnp
import numpy as np
from jax import lax
from jax.experimental import pallas as pl
from jax.experimental.pallas import tpu as pltpu

HEAD_DIM = 64
FOX_HEADS = 8
FOX_KV_HEADS = 4
DSA_HEADS = 8
DSA_KV_HEADS = 2
IDX_HEADS = 8
IDX_DIM = 64
TOPK_MAX = 256
ROPE_THETA = 500000.0
ROPE_FRACTION_DIV = 4
NORM_EPS = 1e-6
PAGE_SIZE = 128

FOX_Q = FOX_HEADS * HEAD_DIM
FOX_KV = FOX_KV_HEADS * HEAD_DIM
DSA_Q = DSA_HEADS * HEAD_DIM
DSA_KV = DSA_KV_HEADS * HEAD_DIM
IDX_Q = IDX_HEADS * IDX_DIM

LANES = 128
NEG = -1e30
QK_SCALE = HEAD_DIM ** -0.5
VMEM_LIMIT = 56 * 1024 * 1024

F32 = jnp.float32
BF16 = jnp.bfloat16

_C_FOX = 0
_C_DSA = _C_FOX + FOX_Q + 2 * FOX_KV
_C_IQ = _C_DSA + DSA_Q + 2 * DSA_KV
_C_IK = _C_IQ + IDX_Q
_C_FF = _C_IK + LANES
_C_IW = _C_FF + LANES
_C_END = _C_IW + LANES


def _dot(a, b):
    return jnp.dot(a, b, preferred_element_type=F32)


def _dot_nt(a, b):
    return lax.dot_general(a, b, (((1,), (1,)), ((), ())), preferred_element_type=F32)


def _dot_tn(a, b):
    return lax.dot_general(a, b, (((0,), (0,)), ((), ())), preferred_element_type=F32)


def _split3(x):
    hi = x.astype(BF16)
    r1 = x - hi.astype(F32)
    mid = r1.astype(BF16)
    lo = (r1 - mid.astype(F32)).astype(BF16)
    return hi, mid, lo


def _lane_iota(shape):
    return lax.broadcasted_iota(jnp.int32, shape, len(shape) - 1)


def _row_iota(shape):
    return lax.broadcasted_iota(jnp.int32, shape, len(shape) - 2)


def _rope(x, cos, sa, sb):
    w = x.shape[1]
    reps = w // LANES
    if reps > 1:
        cos = jnp.concatenate([cos] * reps, axis=1)
        sa = jnp.concatenate([sa] * reps, axis=1)
        sb = jnp.concatenate([sb] * reps, axis=1)
    up = pltpu.roll(x, w - 8, 1)
    dn = pltpu.roll(x, 8, 1)
    return x * cos + up * sa + dn * sb


def _rmsnorm(x, g):
    ms = jnp.mean(x * x, axis=-1, keepdims=True)
    return x * lax.rsqrt(ms + NORM_EPS) * g


def _proj_body(x_ref, g_ref, w_ref, b_ref, rope_ref, *refs, prompt, blocks_per_seq):
    (fq_ref, fk_ref, fv_ref, lf_ref, dq_ref, dk_ref, dv_ref, iq_ref, ik_ref, iw_ref) = refs[:10]
    tm = x_ref.shape[0]
    n = _rmsnorm(x_ref[...], g_ref[...]).astype(BF16)
    cos, sa, sb = rope_ref[0], rope_ref[1], rope_ref[2]
    low = _lane_iota((tm, LANES)) < HEAD_DIM

    def mm(lo, hi):
        return _dot(n, w_ref[:, lo:hi])

    fq = mm(_C_FOX, _C_FOX + FOX_Q)
    fq_ref[...] = (fq * QK_SCALE).astype(BF16)
    fk = mm(_C_FOX + FOX_Q, _C_FOX + FOX_Q + FOX_KV)
    fk_ref[...] = fk
    fv = mm(_C_FOX + FOX_Q + FOX_KV, _C_DSA)
    fv_ref[...] = fv
    dq = _rope(mm(_C_DSA, _C_DSA + DSA_Q), cos, sa, sb)
    dq_ref[...] = (dq * QK_SCALE).astype(BF16)
    dk = _rope(mm(_C_DSA + DSA_Q, _C_DSA + DSA_Q + DSA_KV), cos, sa, sb)
    dk_ref[...] = dk
    dv = mm(_C_DSA + DSA_Q + DSA_KV, _C_IQ)
    dv_ref[...] = dv
    iq = _rope(mm(_C_IQ, _C_IK), cos, sa, sb)
    iq_ref[...] = (iq * (IDX_DIM ** -0.5)).astype(BF16)
    ikg = _rope(mm(_C_IK, _C_FF), cos, sa, sb)
    ik_ref[...] = ikg[:, :IDX_DIM]
    ff = mm(_C_FF, _C_IW) + b_ref[...]
    lf = jnp.minimum(ff, 0.0) - jnp.log(1.0 + jnp.exp(-jnp.abs(ff)))
    lf_ref[...] = lf[:, :FOX_HEADS]
    iw = mm(_C_IW, _C_END)
    iw_ref[...] = iw[:, :IDX_HEADS] * (IDX_HEADS ** -0.5)

    if not prompt:
        return
    negc_ref, fkt_ref, fvd_ref, dkt_ref, dvq_ref, ikt_ref, carry_ref = refs[10:]

    @pl.when(pl.program_id(0) % blocks_per_seq == 0)
    def _():
        carry_ref[...] = jnp.zeros_like(carry_ref)

    tri = (_row_iota((tm, tm)) >= _lane_iota((tm, tm))).astype(BF16)
    hi, mid, lo = _split3(lf)
    csum = _dot(tri, hi) + _dot(tri, mid) + _dot(tri, lo) + carry_ref[0:1, :]
    carry_ref[0:1, :] = csum[tm - 1:tm, :]
    negc_ref[0] = -(csum.T[:FOX_HEADS, :])

    fkt = fk.T.astype(BF16)
    for h in range(FOX_KV_HEADS):
        kt = fkt[h * HEAD_DIM:(h + 1) * HEAD_DIM, :]
        fkt_ref[h, 0] = jnp.concatenate([kt, kt], axis=0)
    for p in range(FOX_KV_HEADS // 2):
        pair = fv[:, p * LANES:(p + 1) * LANES]
        swap = pltpu.roll(pair, HEAD_DIM, 1)
        fvd_ref[2 * p] = jnp.where(low, pair, swap).astype(BF16)
        fvd_ref[2 * p + 1] = jnp.where(low, swap, pair).astype(BF16)

    dkt = dk.T.astype(BF16)
    swap = pltpu.roll(dv, HEAD_DIM, 1)
    dv_dup = (jnp.where(low, dv, swap).astype(BF16), jnp.where(low, swap, dv).astype(BF16))
    for h in range(DSA_KV_HEADS):
        kt = dkt[h * HEAD_DIM:(h + 1) * HEAD_DIM, :]
        dkt_ref[h, 0] = jnp.concatenate([kt, kt, kt, kt], axis=0)
        dvq_ref[h] = jnp.concatenate([dv_dup[h], dv_dup[h]], axis=1)

    ikt = ikg.T.astype(BF16)[:IDX_DIM, :]
    ikt_ref[0] = jnp.concatenate([ikt, ikt], axis=0)


def _proj(x2d, g1, w1, bias, rope_tab, *, prompt, seq_len, tm):
    m, d = x2d.shape
    nblk = m // tm
    npos = rope_tab.shape[1] // tm
    row = lambda i: (i, 0)
    outs = [
        (jax.ShapeDtypeStruct((m, FOX_Q), BF16), pl.BlockSpec((tm, FOX_Q), row)),
        (jax.ShapeDtypeStruct((m, FOX_KV), F32), pl.BlockSpec((tm, FOX_KV), row)),
        (jax.ShapeDtypeStruct((m, FOX_KV), F32), pl.BlockSpec((tm, FOX_KV), row)),
        (jax.ShapeDtypeStruct((m, FOX_HEADS), F32), pl.BlockSpec((tm, FOX_HEADS), row)),
        (jax.ShapeDtypeStruct((m, DSA_Q), BF16), pl.BlockSpec((tm, DSA_Q), row)),
        (jax.ShapeDtypeStruct((m, DSA_KV), F32), pl.BlockSpec((tm, DSA_KV), row)),
        (jax.ShapeDtypeStruct((m, DSA_KV), F32), pl.BlockSpec((tm, DSA_KV), row)),
        (jax.ShapeDtypeStruct((m, IDX_Q), BF16), pl.BlockSpec((tm, IDX_Q), row)),
        (jax.ShapeDtypeStruct((m, IDX_DIM), F32), pl.BlockSpec((tm, IDX_DIM), row)),
        (jax.ShapeDtypeStruct((m, IDX_HEADS), F32), pl.BlockSpec((tm, IDX_HEADS), row)),
    ]
    scratch = []
    if prompt:
        outs += [
            (jax.ShapeDtypeStruct((nblk, FOX_HEADS, tm), F32), pl.BlockSpec((1, FOX_HEADS, tm), lambda i: (i, 0, 0))),
            (jax.ShapeDtypeStruct((FOX_KV_HEADS, nblk, 2 * HEAD_DIM, tm), BF16),
             pl.BlockSpec((FOX_KV_HEADS, 1, 2 * HEAD_DIM, tm), lambda i: (0, i, 0, 0))),
            (jax.ShapeDtypeStruct((FOX_KV_HEADS, m, 2 * HEAD_DIM), BF16),
             pl.BlockSpec((FOX_KV_HEADS, tm, 2 * HEAD_DIM), lambda i: (0, i, 0))),
            (jax.ShapeDtypeStruct((DSA_KV_HEADS, nblk, 4 * HEAD_DIM, tm), BF16),
             pl.BlockSpec((DSA_KV_HEADS, 1, 4 * HEAD_DIM, tm), lambda i: (0, i, 0, 0))),
            (jax.ShapeDtypeStruct((DSA_KV_HEADS, m, 4 * HEAD_DIM), BF16),
             pl.BlockSpec((DSA_KV_HEADS, tm, 4 * HEAD_DIM), lambda i: (0, i, 0))),
            (jax.ShapeDtypeStruct((nblk, 2 * IDX_DIM, tm), BF16), pl.BlockSpec((1, 2 * IDX_DIM, tm), lambda i: (i, 0, 0))),
        ]
        scratch = [pltpu.VMEM((8, LANES), F32)]
    const = lambda i: (0, 0)
    return pl.pallas_call(
        functools.partial(_proj_body, prompt=prompt, blocks_per_seq=seq_len // tm),
        grid=(nblk,),
        in_specs=[
            pl.BlockSpec((tm, d), row),
            pl.BlockSpec((1, d), const),
            pl.BlockSpec((d, _C_END), const, pipeline_mode=pl.Buffered(1)),
            pl.BlockSpec((1, LANES), const),
            pl.BlockSpec((3, tm, LANES), lambda i: (0, i % npos, 0)),
        ],
        out_specs=[o[1] for o in outs],
        out_shape=[o[0] for o in outs],
        scratch_shapes=scratch,
        compiler_params=pltpu.CompilerParams(dimension_semantics=("arbitrary",), vmem_limit_bytes=VMEM_LIMIT),
        name="proj_prompt" if prompt else "proj_sample",
    )(x2d, g1, w1, bias, rope_tab)


def _softmax_step(s, m, l, acc, v):
    m_new = jnp.maximum(m, jnp.max(s, axis=-1, keepdims=True))
    alpha = jnp.exp(m - m_new)
    p = jnp.exp(s - m_new)
    l = alpha * l + jnp.sum(p, axis=-1, keepdims=True)
    acc = alpha * acc + _dot(p.astype(BF16), v)
    return m_new, l, acc


def _pick_head_lanes(acc, tq, groups):
    lane = _lane_iota((tq, groups * HEAD_DIM))
    out = acc[0:tq]
    for g in range(1, groups):
        out = jnp.where(lane >= g * HEAD_DIM, acc[g * tq:(g + 1) * tq], out)
    return out


def _stack_heads(q, tq, groups):
    lane = _lane_iota((tq, groups * HEAD_DIM))
    zero = jnp.zeros_like(q)
    return jnp.concatenate(
        [jnp.where((lane >= g * HEAD_DIM) & (lane < (g + 1) * HEAD_DIM), q, zero) for g in range(groups)], axis=0)


def _fox_prompt_body(q_ref, kt_ref, v_ref, negc_ref, o_ref, *, tq, tk):
    i = pl.program_id(1)
    nfull = (i * tq) // tk
    groups = FOX_HEADS // FOX_KV_HEADS
    qpos = i * tq + (_row_iota((groups * tq, tk)) & (tq - 1))
    kloc = _lane_iota((groups * tq, tk))

    for h in range(FOX_KV_HEADS):
        qs = _stack_heads(q_ref[:, h * groups * HEAD_DIM:(h + 1) * groups * HEAD_DIM], tq, groups)

        def logits(j):
            s = _dot(qs, kt_ref[h, j])
            nb = negc_ref[j]
            bias = jnp.concatenate(
                [jnp.broadcast_to(nb[h * groups + g:h * groups + g + 1, :], (tq, tk)) for g in range(groups)], axis=0)
            return s + bias

        def step(j, carry, masked):
            s = logits(j)
            if masked:
                s = jnp.where(j * tk + kloc <= qpos, s, NEG)
            v = v_ref[h, pl.ds(pl.multiple_of(j * tk, tk), tk), :]
            return _softmax_step(s, *carry, v)

        init = (jnp.full((groups * tq, 1), NEG, F32), jnp.zeros((groups * tq, 1), F32),
                jnp.zeros((groups * tq, groups * HEAD_DIM), F32))
        carry = lax.fori_loop(0, nfull, lambda j, c: step(j, c, False), init)
        m, l, acc = step(nfull, carry, True)
        out = _pick_head_lanes(acc / l, tq, groups)
        o_ref[:, h * groups * HEAD_DIM:(h + 1) * groups * HEAD_DIM] = out.astype(BF16)


def _fox_prompt(fq, fkt, fvd, negc, *, batch, seq_len, tq, tk):
    m = fq.shape[0]
    nq = seq_len // tq
    nkb = seq_len // tk
    return pl.pallas_call(
        functools.partial(_fox_prompt_body, tq=tq, tk=tk),
        grid=(batch, nq),
        in_specs=[
            pl.BlockSpec((tq, FOX_Q), lambda b, i: (b * nq + i, 0)),
            pl.BlockSpec((FOX_KV_HEADS, nkb, 2 * HEAD_DIM, tk), lambda b, i: (0, b, 0, 0), pipeline_mode=pl.Buffered(1)),
            pl.BlockSpec((FOX_KV_HEADS, seq_len, 2 * HEAD_DIM), lambda b, i: (0, b, 0), pipeline_mode=pl.Buffered(1)),
            pl.BlockSpec((nkb, FOX_HEADS, tk), lambda b, i: (b, 0, 0), pipeline_mode=pl.Buffered(1)),
        ],
        out_specs=pl.BlockSpec((tq, FOX_Q), lambda b, i: (b * nq + i, 0)),
        out_shape=jax.ShapeDtypeStruct((m, FOX_Q), BF16),
        compiler_params=pltpu.CompilerParams(dimension_semantics=("arbitrary", "arbitrary"), vmem_limit_bytes=VMEM_LIMIT),
        name="fox_prompt",
    )(fq, fkt, fvd, negc)


_BISECT_PASSES = 14


def _select_topk(s_ref, bias_ref, nch, ksel, nvalid, ch):
    rows = s_ref.shape[1]
    nsub = ch // LANES
    kf = jnp.float32(ksel)
    inf = jnp.float32(jnp.inf)

    def fold(x, op):
        out = x[:, 0:LANES]
        for k in range(1, nsub):
            out = op(out, x[:, k * LANES:(k + 1) * LANES])
        return out

    def reduce_chunks(fn, op, init, lane_reduce):
        def body(c, acc):
            return op(acc, fold(fn(c, s_ref[c]), op))
        acc = lax.fori_loop(0, nch, body, jnp.full((rows, LANES), init, F32))
        return lane_reduce(acc, axis=1, keepdims=True)

    def count_ge(t):
        return reduce_chunks(lambda c, blk: jnp.where(blk >= t, 1.0, 0.0), jnp.add, 0.0, jnp.sum)

    row_max = reduce_chunks(lambda c, blk: blk, jnp.maximum, -inf, jnp.max)
    row_min = reduce_chunks(lambda c, blk: jnp.where(blk > -inf, blk, inf), jnp.minimum, inf, jnp.min)

    one, zero = jnp.float32(1.0), jnp.float32(0.0)
    all_sel = jnp.where(nvalid <= kf, one, zero)
    lo, clo = row_min, nvalid
    hi, chi = jnp.full((rows, 1), inf, F32), jnp.zeros((rows, 1), F32)

    def bisect(_, st):
        lo, clo, hi, chi, done = st
        t = jnp.where(hi == inf, row_max, 0.5 * (lo + hi))
        cnt = count_ge(t)
        upd = (done < 0.5) & (t > lo) & (t < hi)
        ge = cnt >= kf
        lo = jnp.where(upd & ge, t, lo)
        clo = jnp.where(upd & ge, cnt, clo)
        hi = jnp.where(upd & (cnt < kf), t, hi)
        chi = jnp.where(upd & (cnt < kf), cnt, chi)
        return lo, clo, hi, chi, jnp.where(clo == kf, one, done)

    lo, clo, hi, chi, done = lax.fori_loop(0, _BISECT_PASSES, bisect, (lo, clo, hi, chi, all_sel))

    def snap_cond(st):
        return jnp.min(st[4]) < 0.5

    def snap(st):
        lo, clo, hi, chi, done = st
        v = reduce_chunks(lambda c, blk: jnp.where(blk < hi, blk, -inf), jnp.maximum, -inf, jnp.max)
        cv = count_ge(v)
        fin = (done < 0.5) & (cv >= kf)
        go = (done < 0.5) & (cv < kf)
        lo = jnp.where(fin, v, lo)
        clo = jnp.where(fin, cv, clo)
        hi = jnp.where(go, v, hi)
        chi = jnp.where(go, cv, chi)
        return lo, clo, hi, chi, jnp.where(fin, one, done)

    lo, clo, hi, chi, done = lax.while_loop(snap_cond, snap, (lo, clo, hi, chi, done))

    thr = jnp.where(all_sel > 0.5, -inf, lo)
    need_cut = jnp.where((all_sel < 0.5) & (clo > kf), one, zero)
    need = kf - chi
    big = jnp.asarray(nch * ch).astype(F32)

    def idx_of(c):
        return (c * ch + _lane_iota((rows, ch))).astype(F32)

    def cut_search():
        def body(_, st):
            jlo, jhi = st
            mid = jnp.floor(0.5 * (jlo + jhi))
            cnt = reduce_chunks(lambda c, blk: jnp.where((blk == thr) & (idx_of(c) <= mid), 1.0, 0.0), jnp.add, 0.0, jnp.sum)
            ok = cnt >= need
            return jnp.where(ok, jlo, mid), jnp.where(ok, mid, jhi)

        npass = int(np.ceil(np.log2(s_ref.shape[0] * ch))) + 1
        _, jhi = lax.fori_loop(0, npass, body, (jnp.full((rows, 1), -1.0, F32), jnp.zeros((rows, 1), F32) + (big - 1.0)))
        return jhi

    jcut = lax.cond(jnp.max(need_cut) > 0.5, cut_search, lambda: jnp.full((rows, 1), -1.0, F32))
    jcut = jnp.where(need_cut > 0.5, jcut, jnp.where(all_sel > 0.5, -1.0, big))

    def write(c, _):
        blk = s_ref[c]
        sel = (blk > thr) | ((blk == thr) & (idx_of(c) <= jcut))
        bias_ref[c] = jnp.where(sel, 0.0, NEG)
        return 0

    lax.fori_loop(0, nch, write, 0)


def _dsa_prompt_body(iq_ref, iw_ref, dq_ref, ikt_ref, dkt_ref, dv_ref, o_ref, s_ref, bias_ref, *, tq, ch, ksel):
    i = pl.program_id(1)
    nch = (i * tq + tq - 1) // ch + 1
    groups = DSA_HEADS // DSA_KV_HEADS

    iq = iq_ref[...]
    lane = _lane_iota((tq, 2 * IDX_DIM))
    zero = jnp.zeros((tq, 2 * IDX_DIM), BF16)
    parts = []
    for h in range(IDX_HEADS):
        pair = iq[:, (h // 2) * 2 * IDX_DIM:(h // 2 + 1) * 2 * IDX_DIM]
        keep = (lane < IDX_DIM) if h % 2 == 0 else (lane >= IDX_DIM)
        parts.append(jnp.where(keep, pair, zero))
    iqs = jnp.concatenate(parts, axis=0)
    iw = iw_ref[...]
    wcol = [iw[:, h:h + 1] for h in range(IDX_HEADS)]
    qpos = i * tq + _row_iota((tq, ch))
    kloc = _lane_iota((tq, ch))

    def score_chunk(c, _):
        d = _dot(iqs, ikt_ref[c])
        acc = jnp.maximum(d[0:tq], 0.0) * wcol[0]
        for h in range(1, IDX_HEADS):
            acc = acc + jnp.maximum(d[h * tq:(h + 1) * tq], 0.0) * wcol[h]
        s_ref[c] = jnp.where(c * ch + kloc <= qpos, acc, -jnp.inf)
        return 0

    lax.fori_loop(0, nch, score_chunk, 0)

    nvalid = (i * tq + _row_iota((tq, 1)) + 1).astype(F32)
    _select_topk(s_ref, bias_ref, nch, ksel, nvalid, ch)

    for h in range(DSA_KV_HEADS):
        qs = _stack_heads(dq_ref[:, h * groups * HEAD_DIM:(h + 1) * groups * HEAD_DIM], tq, groups)

        def step(c, carry):
            b = bias_ref[c]
            s = _dot(qs, dkt_ref[h, c]) + jnp.concatenate([b] * groups, axis=0)
            v = dv_ref[h, pl.ds(pl.multiple_of(c * ch, ch), ch), :]
            return _softmax_step(s, *carry, v)

        init = (jnp.full((groups * tq, 1), NEG, F32), jnp.zeros((groups * tq, 1), F32),
                jnp.zeros((groups * tq, groups * HEAD_DIM), F32))
        m, l, acc = lax.fori_loop(0, nch, step, init)
        out = _pick_head_lanes(acc / l, tq, groups)
        o_ref[:, h * groups * HEAD_DIM:(h + 1) * groups * HEAD_DIM] = out.astype(BF16)


def _dsa_prompt(iq, iw, dq, ikt, dkt, dvq, *, batch, seq_len, tq, ch):
    m = iq.shape[0]
    nq = seq_len // tq
    nkb = seq_len // ch
    ksel = min(TOPK_MAX, seq_len // 4)
    row = lambda b, i: (b * nq + i, 0)
    return pl.pallas_call(
        functools.partial(_dsa_prompt_body, tq=tq, ch=ch, ksel=ksel),
        grid=(batch, nq),
        in_specs=[
            pl.BlockSpec((tq, IDX_Q), row),
            pl.BlockSpec((tq, IDX_HEADS), row),
            pl.BlockSpec((tq, DSA_Q), row),
            pl.BlockSpec((nkb, 2 * IDX_DIM, ch), lambda b, i: (b, 0, 0), pipeline_mode=pl.Buffered(1)),
            pl.BlockSpec((DSA_KV_HEADS, nkb, 4 * HEAD_DIM, ch), lambda b, i: (0, b, 0, 0), pipeline_mode=pl.Buffered(1)),
            pl.BlockSpec((DSA_KV_HEADS, seq_len, 4 * HEAD_DIM), lambda b, i: (0, b, 0), pipeline_mode=pl.Buffered(1)),
        ],
        out_specs=pl.BlockSpec((tq, DSA_Q), row),
        out_shape=jax.ShapeDtypeStruct((m, DSA_Q), BF16),
        scratch_shapes=[pltpu.VMEM((nkb, tq, ch), F32), pltpu.VMEM((nkb, tq, ch), F32)],
        compiler_params=pltpu.CompilerParams(dimension_semantics=("arbitrary", "arbitrary"), vmem_limit_bytes=VMEM_LIMIT),
        name="dsa_prompt",
    )(iq, iw, dq, ikt, dkt, dvq)


_PG = 8
_ROWS = 32


def _page_specs(block, n):
    def spec(k):
        return pl.BlockSpec(block, lambda b, j, pt: (pt[b, j * _PG + k],) + (0,) * (len(block) - 1))
    return [spec(k) for k in range(n)]


def _pad_rows(x, rows):
    return jnp.concatenate([x, jnp.zeros((rows - x.shape[0], x.shape[1]), x.dtype)], axis=0)


def _fox_sample_body(pt_ref, q_ref, kn_ref, vn_ref, lfn_ref, *refs, dec_seq):
    kp = refs[0:_PG]
    vp = refs[_PG:2 * _PG]
    lp = refs[2 * _PG:3 * _PG]
    o_ref, m_ref, l_ref, acc_ref, pre_ref = refs[3 * _PG:]
    j = pl.program_id(1)

    @pl.when(j == 0)
    def _():
        m_ref[...] = jnp.full_like(m_ref, NEG)
        l_ref[...] = jnp.zeros_like(l_ref)
        acc_ref[...] = jnp.zeros_like(acc_ref)
        pre_ref[...] = jnp.zeros_like(pre_ref)

    q = q_ref[0]
    triu = (_row_iota((PAGE_SIZE, PAGE_SIZE)) <= _lane_iota((PAGE_SIZE, PAGE_SIZE))).astype(BF16)

    def cum_rows(lf):
        hi, mid, lo = _split3(lf)
        return _dot(hi, triu) + _dot(mid, triu) + _dot(lo, triu)

    def update(s, v):
        m, l, acc = _softmax_step(s, m_ref[...], l_ref[...], acc_ref[...], v)
        m_ref[...] = m
        l_ref[...] = l
        acc_ref[...] = acc

    for k in range(_PG):
        c = cum_rows(lp[k][0]) + pre_ref[...]
        pre_ref[...] = jnp.broadcast_to(c[:, PAGE_SIZE - 1:PAGE_SIZE], pre_ref.shape)
        s = _dot_nt(q, kp[k][0].astype(BF16)) - jnp.concatenate([c] * dec_seq, axis=0)
        update(s, vp[k][0].astype(BF16))

    @pl.when(j == pl.num_programs(1) - 1)
    def _():
        kn = _pad_rows(kn_ref[0], PAGE_SIZE)
        vn = _pad_rows(vn_ref[0], PAGE_SIZE)
        c = cum_rows(lfn_ref[0]) + pre_ref[...]
        s = _dot_nt(q, kn.astype(BF16)) - jnp.concatenate([c] * dec_seq, axis=0)
        kidx = _lane_iota((_ROWS, PAGE_SIZE))
        tok = _row_iota((_ROWS, PAGE_SIZE)) >> 3
        s = jnp.where(kidx <= tok, s, NEG)
        update(s, vn.astype(BF16))
        o_ref[0] = acc_ref[...] / l_ref[...]


def _fox_sample(page_table, q_rows, k_new, v_new, lf_new, k_pool, v_pool, lf_pool, *, dec_seq):
    db, n_pages = page_table.shape
    ng = n_pages // _PG
    per_b = lambda b, j, pt: (b, 0, 0)
    grid_spec = pltpu.PrefetchScalarGridSpec(
        num_scalar_prefetch=1,
        grid=(db, ng),
        in_specs=[
            pl.BlockSpec((1, _ROWS, FOX_KV), per_b),
            pl.BlockSpec((1, 8, FOX_KV), per_b),
            pl.BlockSpec((1, 8, FOX_KV), per_b),
            pl.BlockSpec((1, FOX_HEADS, PAGE_SIZE), per_b),
        ] + _page_specs((1, PAGE_SIZE, FOX_KV), _PG) + _page_specs((1, PAGE_SIZE, FOX_KV), _PG)
        + _page_specs((1, FOX_HEADS, PAGE_SIZE), _PG),
        out_specs=pl.BlockSpec((1, _ROWS, FOX_KV), per_b),
        scratch_shapes=[pltpu.VMEM((_ROWS, 1), F32), pltpu.VMEM((_ROWS, 1), F32), pltpu.VMEM((_ROWS, FOX_KV), F32),
                        pltpu.VMEM((FOX_HEADS, PAGE_SIZE), F32)],
    )
    return pl.pallas_call(
        functools.partial(_fox_sample_body, dec_seq=dec_seq),
        grid_spec=grid_spec,
        out_shape=jax.ShapeDtypeStruct((db, _ROWS, FOX_KV), F32),
        compiler_params=pltpu.CompilerParams(dimension_semantics=("arbitrary", "arbitrary"), vmem_limit_bytes=VMEM_LIMIT),
        name="fox_sample",
    )(page_table, q_rows, k_new, v_new, lf_new, *([k_pool] * _PG), *([v_pool] * _PG), *([lf_pool] * _PG))


def _idx_sample_body(pt_ref, q_ref, w_ref, kn_ref, *refs, dec_seq, past, ksel):
    kp = refs[0:_PG]
    bias_ref, s_ref = refs[_PG:]
    j = pl.program_id(1)
    ng = pl.num_programs(1)
    ch = _PG * PAGE_SIZE
    q = q_ref[0]
    w = w_ref[0]

    def scores(keys):
        d = jnp.maximum(_dot_nt(q, keys.astype(BF16)), 0.0) * w
        per_tok = [jnp.sum(d[t * IDX_HEADS:(t + 1) * IDX_HEADS], axis=0, keepdims=True) for t in range(dec_seq)]
        pad = jnp.full((8 - dec_seq, d.shape[1]), -jnp.inf, F32)
        return jnp.concatenate(per_tok + [pad], axis=0)

    s_ref[j] = jnp.concatenate([scores(kp[k][0]) for k in range(_PG)], axis=1)

    @pl.when(j == ng - 1)
    def _():
        sn = scores(_pad_rows(kn_ref[0], PAGE_SIZE))
        kidx = _lane_iota((8, PAGE_SIZE))
        tok = _row_iota((8, PAGE_SIZE))
        sn = jnp.where(kidx <= tok, sn, -jnp.inf)
        s_ref[ng] = jnp.concatenate([sn, jnp.full((8, ch - PAGE_SIZE), -jnp.inf, F32)], axis=1)
        tokc = _row_iota((8, 1))
        nvalid = jnp.where(tokc < dec_seq, (past + 1 + tokc).astype(F32), 0.0)
        _select_topk(s_ref, bias_ref.at[0], ng + 1, ksel, nvalid, ch)


def _idx_sample(page_table, iq_rows, iw_rows, ik_new, ik_pool, *, dec_seq, ksel):
    db, n_pages = page_table.shape
    ng = n_pages // _PG
    ch = _PG * PAGE_SIZE
    per_b = lambda b, j, pt: (b, 0, 0)
    grid_spec = pltpu.PrefetchScalarGridSpec(
        num_scalar_prefetch=1,
        grid=(db, ng),
        in_specs=[
            pl.BlockSpec((1, _ROWS, IDX_DIM), per_b),
            pl.BlockSpec((1, _ROWS, 1), per_b),
            pl.BlockSpec((1, 8, IDX_DIM), per_b),
        ] + _page_specs((1, PAGE_SIZE, IDX_DIM), _PG),
        out_specs=pl.BlockSpec((1, ng + 1, 8, ch), lambda b, j, pt: (b, 0, 0, 0)),
        scratch_shapes=[pltpu.VMEM((ng + 1, 8, ch), F32)],
    )
    return pl.pallas_call(
        functools.partial(_idx_sample_body, dec_seq=dec_seq, past=n_pages * PAGE_SIZE, ksel=ksel),
        grid_spec=grid_spec,
        out_shape=jax.ShapeDtypeStruct((db, ng + 1, 8, ch), F32),
        compiler_params=pltpu.CompilerParams(dimension_semantics=("arbitrary", "arbitrary"), vmem_limit_bytes=VMEM_LIMIT),
        name="idx_sample",
    )(page_table, iq_rows, iw_rows, ik_new, *([ik_pool] * _PG))


def _dsa_sample_body(pt_ref, q_ref, bias_ref, kn_ref, vn_ref, *refs, dec_seq):
    kp = refs[0:_PG]
    vp = refs[_PG:2 * _PG]
    o_ref, m_ref, l_ref, acc_ref = refs[2 * _PG:]
    j = pl.program_id(1)
    ng = pl.num_programs(1)

    @pl.when(j == 0)
    def _():
        m_ref[...] = jnp.full_like(m_ref, NEG)
        l_ref[...] = jnp.zeros_like(l_ref)
        acc_ref[...] = jnp.zeros_like(acc_ref)

    q = q_ref[0]

    def rows_of(b):
        return jnp.concatenate([jnp.broadcast_to(b[t:t + 1, :], (DSA_HEADS, b.shape[1])) for t in range(dec_seq)], axis=0)

    def update(s, v):
        m, l, acc = _softmax_step(s, m_ref[...], l_ref[...], acc_ref[...], v)
        m_ref[...] = m
        l_ref[...] = l
        acc_ref[...] = acc

    for k in range(_PG):
        b = bias_ref[0, j, :, k * PAGE_SIZE:(k + 1) * PAGE_SIZE]
        update(_dot_nt(q, kp[k][0].astype(BF16)) + rows_of(b), vp[k][0].astype(BF16))

    @pl.when(j == ng - 1)
    def _():
        b = bias_ref[0, ng, :, 0:PAGE_SIZE]
        kn = _pad_rows(kn_ref[0], PAGE_SIZE)
        vn = _pad_rows(vn_ref[0], PAGE_SIZE)
        update(_dot_nt(q, kn.astype(BF16)) + rows_of(b), vn.astype(BF16))
        o_ref[0] = acc_ref[...] / l_ref[...]


def _dsa_sample(page_table, q_rows, bias, k_new, v_new, k_pool, v_pool, *, dec_seq):
    db, n_pages = page_table.shape
    ng = n_pages // _PG
    ch = _PG * PAGE_SIZE
    per_b = lambda b, j, pt: (b, 0, 0)
    grid_spec = pltpu.PrefetchScalarGridSpec(
        num_scalar_prefetch=1,
        grid=(db, ng),
        in_specs=[
            pl.BlockSpec((1, _ROWS, DSA_KV), per_b),
            pl.BlockSpec((1, ng + 1, 8, ch), lambda b, j, pt: (b, 0, 0, 0)),
            pl.BlockSpec((1, 8, DSA_KV), per_b),
            pl.BlockSpec((1, 8, DSA_KV), per_b),
        ] + _page_specs((1, PAGE_SIZE, DSA_KV), _PG) + _page_specs((1, PAGE_SIZE, DSA_KV), _PG),
        out_specs=pl.BlockSpec((1, _ROWS, DSA_KV), per_b),
        scratch_shapes=[pltpu.VMEM((_ROWS, 1), F32), pltpu.VMEM((_ROWS, 1), F32), pltpu.VMEM((_ROWS, DSA_KV), F32)],
    )
    return pl.pallas_call(
        functools.partial(_dsa_sample_body, dec_seq=dec_seq),
        grid_spec=grid_spec,
        out_shape=jax.ShapeDtypeStruct((db, _ROWS, DSA_KV), F32),
        compiler_params=pltpu.CompilerParams(dimension_semantics=("arbitrary", "arbitrary"), vmem_limit_bytes=VMEM_LIMIT),
        name="dsa_sample",
    )(page_table, q_rows, bias, k_new, v_new, *([k_pool] * _PG), *([v_pool] * _PG))


_FF_CHUNK = 1024


def _post_body(x_ref, of_ref, od_ref, g1_ref, wg_ref, wof_ref, wod_ref, wout_ref, g2_ref, wup_ref, wdn_ref, gf_ref, y_ref):
    x = x_ref[...]
    d = x.shape[1]
    n = _rmsnorm(x, g1_ref[...]).astype(BF16)
    gate_a = jax.nn.sigmoid(_dot(n, wg_ref[:, 0:d]))
    gate_b = jax.nn.sigmoid(_dot(n, wg_ref[:, d:2 * d]))
    mix = gate_a * _dot(of_ref[...], wof_ref[...]) + gate_b * _dot(od_ref[...], wod_ref[...])
    h = x + _dot(mix.astype(BF16), wout_ref[...])
    n2 = _rmsnorm(h, g2_ref[...]).astype(BF16)
    y = h
    for c in range(wup_ref.shape[1] // _FF_CHUNK):
        u = jnp.maximum(_dot(n2, wup_ref[:, c * _FF_CHUNK:(c + 1) * _FF_CHUNK]), 0.0)
        y = y + _dot((u * u).astype(BF16), wdn_ref[c * _FF_CHUNK:(c + 1) * _FF_CHUNK, :])
    y_ref[...] = _rmsnorm(y, gf_ref[...])


def _post(x2d, o_fox, o_dsa, g1, wg, wof, wod, wout, g2, wup, wdn, gf, *, tm):
    m, d = x2d.shape
    row = lambda i: (i, 0)
    const = lambda i: (0, 0)

    def whole(a):
        return pl.BlockSpec(a.shape, const, pipeline_mode=pl.Buffered(1))

    return pl.pallas_call(
        _post_body,
        grid=(m // tm,),
        in_specs=[pl.BlockSpec((tm, d), row), pl.BlockSpec((tm, FOX_Q), row), pl.BlockSpec((tm, DSA_Q), row),
                  whole(g1), whole(wg), whole(wof), whole(wod), whole(wout), whole(g2), whole(wup), whole(wdn), whole(gf)],
        out_specs=pl.BlockSpec((tm, d), row),
        out_shape=jax.ShapeDtypeStruct((m, d), F32),
        compiler_params=pltpu.CompilerParams(dimension_semantics=("arbitrary",), vmem_limit_bytes=VMEM_LIMIT),
        name="post",
    )(x2d, o_fox, o_dsa, g1, wg, wof, wod, wout, g2, wup, wdn, gf)


def _rope_tables(pos):
    r = HEAD_DIM // ROPE_FRACTION_DIV
    half = r // 2
    inv_freq = jnp.float32(ROPE_THETA) ** (-jnp.arange(half, dtype=F32) * (2.0 / r))
    ang = pos.astype(F32)[:, None] * inv_freq[None, :]
    cos, sin = jnp.cos(ang), jnp.sin(ang)
    p = pos.shape[0]
    one = jnp.ones((p, HEAD_DIM - r), F32)
    zero_r = jnp.zeros((p, HEAD_DIM - r), F32)
    zero_h = jnp.zeros((p, half), F32)
    c = jnp.concatenate([cos, cos, one], axis=1)
    sa = jnp.concatenate([-sin, zero_h, zero_r], axis=1)
    sb = jnp.concatenate([zero_h, sin, zero_r], axis=1)
    return jnp.stack([jnp.tile(t, (1, LANES // HEAD_DIM)) for t in (c, sa, sb)], axis=0)


def _pad_cols(w, width):
    return jnp.pad(w, ((0, 0), (0, width - w.shape[1])))


def _repack_w_in(w_in, d_model):
    sizes = (FOX_Q, FOX_KV, FOX_KV, FOX_HEADS, DSA_Q, DSA_KV, DSA_KV, IDX_Q, IDX_DIM, IDX_HEADS, d_model, d_model)
    cuts = [int(c) for c in np.cumsum(sizes)[:-1]]
    fq, fk, fv, ff, dq, dk, dv, iq, ik, iw, ga, gb = jnp.split(w_in, cuts, axis=-1)
    w1 = jnp.concatenate([fq, fk, fv, dq, dk, dv, iq, _pad_cols(ik, LANES), _pad_cols(ff, LANES), _pad_cols(iw, LANES)], axis=1)
    return w1.astype(BF16), jnp.concatenate([ga, gb], axis=1).astype(BF16)


def _head_rows(x, heads, kv_heads):
    db, t, _ = x.shape
    g = heads // kv_heads
    xh = x.reshape(db, t, kv_heads, g, 1, HEAD_DIM)
    eye = jnp.eye(kv_heads, dtype=x.dtype).reshape(1, 1, kv_heads, 1, kv_heads, 1)
    return (xh * eye).reshape(db, t * heads, kv_heads * HEAD_DIM)


def _unstack_rows(o, heads, kv_heads):
    db = o.shape[0]
    g = heads // kv_heads
    t = o.shape[1] // heads
    oh = o.reshape(db, t, kv_heads, g, kv_heads, HEAD_DIM)
    idx = jnp.arange(kv_heads)
    return oh[:, :, idx, :, idx, :].transpose(1, 2, 0, 3, 4).reshape(db * t, heads * HEAD_DIM)


def _pad_tokens(x, rows=8):
    return jnp.pad(x, ((0, 0), (0, rows - x.shape[1]), (0, 0)))


def kernel(x_prompt, x_sample, cache_fox_k, cache_fox_v, cache_fox_logf, cache_dsa_k, cache_dsa_v, cache_idx_k,
           page_table, norm1_g, w_in, b_forget, w_o_fox, w_o_dsa, w_out, norm2_g, w_up, w_down, final_norm_g):
    batch, seq_len, d = x_prompt.shape
    db, dec_seq, _ = x_sample.shape
    depth = norm1_g.shape[0]
    n_pages = page_table.shape[1]
    past = n_pages * PAGE_SIZE
    n_phys = cache_fox_k.shape[1]
    assert dec_seq * FOX_HEADS == _ROWS and n_pages % _PG == 0 and depth >= 1

    tm_p = min(512, seq_len)
    tm_s = db * dec_seq
    rope_p = _rope_tables(jnp.arange(seq_len))
    rope_s = _rope_tables(jnp.tile(past + jnp.arange(dec_seq), db))

    xp = x_prompt.reshape(batch * seq_len, d)
    xs = x_sample.reshape(db * dec_seq, d)
    rows_p, rows_s = [], []
    for l in range(depth):
        w1, wg = _repack_w_in(w_in[l], d)
        bias = _pad_cols(b_forget[l][None, :], LANES)
        g1 = norm1_g[l][None, :]
        g2 = norm2_g[l][None, :]
        gf = final_norm_g[None, :] if l == depth - 1 else jnp.ones((1, d), F32)
        post_w = (g1, wg, w_o_fox[l].astype(BF16), w_o_dsa[l].astype(BF16), w_out[l].astype(BF16), g2,
                  w_up[l].astype(BF16), w_down[l].astype(BF16), gf)

        (fq, fk, fv, lf, dq, dk, dv, iq, ik, iw, negc, fkt, fvd, dkt, dvq, ikt) = _proj(
            xp, g1, w1, bias, rope_p, prompt=True, seq_len=seq_len, tm=tm_p)
        o_fox = _fox_prompt(fq, fkt, fvd, negc, batch=batch, seq_len=seq_len, tq=min(256, seq_len), tk=tm_p)
        o_dsa = _dsa_prompt(iq, iw, dq, ikt, dkt, dvq, batch=batch, seq_len=seq_len, tq=min(128, seq_len), ch=tm_p)
        xp = _post(xp, o_fox, o_dsa, *post_w, tm=tm_p)
        rows_p.append((fk, fv, lf, dk, dv, ik))

        (fq, fk, fv, lf, dq, dk, dv, iq, ik, iw) = _proj(xs, g1, w1, bias, rope_s, prompt=False, seq_len=tm_s, tm=tm_s)
        tok = lambda a: a.reshape(db, dec_seq, a.shape[-1])
        o_fox = _fox_sample(
            page_table, _head_rows(tok(fq), FOX_HEADS, FOX_KV_HEADS), _pad_tokens(tok(fk)), _pad_tokens(tok(fv)),
            jnp.pad(jnp.swapaxes(tok(lf), 1, 2), ((0, 0), (0, 0), (0, PAGE_SIZE - dec_seq))), cache_fox_k[l].reshape(n_phys, PAGE_SIZE, FOX_KV),
            cache_fox_v[l].reshape(n_phys, PAGE_SIZE, FOX_KV), jnp.swapaxes(cache_fox_logf[l], 1, 2), dec_seq=dec_seq)
        ksel = min(TOPK_MAX, (past + dec_seq) // 4)
        sel_bias = _idx_sample(
            page_table, iq.reshape(db, _ROWS, IDX_DIM), iw.reshape(db, _ROWS, 1), _pad_tokens(tok(ik)),
            cache_idx_k[l], dec_seq=dec_seq, ksel=ksel)
        o_dsa = _dsa_sample(
            page_table, _head_rows(tok(dq), DSA_HEADS, DSA_KV_HEADS), sel_bias, _pad_tokens(tok(dk)), _pad_tokens(tok(dv)),
            cache_dsa_k[l].reshape(n_phys, PAGE_SIZE, DSA_KV), cache_dsa_v[l].reshape(n_phys, PAGE_SIZE, DSA_KV),
            dec_seq=dec_seq)
        xs = _post(xs, _unstack_rows(o_fox, FOX_HEADS, FOX_KV_HEADS).astype(BF16),
                   _unstack_rows(o_dsa, DSA_HEADS, DSA_KV_HEADS).astype(BF16), *post_w, tm=tm_s)
        rows_s.append((fk, fv, lf, dk, dv, ik))

    def stack(rows, idx, lead, tail, dtype):
        return jnp.stack([r[idx].reshape(lead + tail) for r in rows], axis=0).astype(dtype)

    lp, ls = (batch, seq_len), (db, dec_seq)
    fox_t, dsa_t = (FOX_KV_HEADS, HEAD_DIM), (DSA_KV_HEADS, HEAD_DIM)
    return (
        xp.reshape(batch, seq_len, d), xs.reshape(db, dec_seq, d),
        stack(rows_p, 0, lp, fox_t, cache_fox_k.dtype), stack(rows_p, 1, lp, fox_t, cache_fox_v.dtype),
        stack(rows_p, 2, lp, (FOX_HEADS,), cache_fox_logf.dtype),
        stack(rows_p, 3, lp, dsa_t, cache_dsa_k.dtype), stack(rows_p, 4, lp, dsa_t, cache_dsa_v.dtype),
        stack(rows_p, 5, lp, (IDX_DIM,), cache_idx_k.dtype),
        stack(rows_s, 0, ls, fox_t, cache_fox_k.dtype), stack(rows_s, 1, ls, fox_t, cache_fox_v.dtype),
        stack(rows_s, 2, ls, (FOX_HEADS,), cache_fox_logf.dtype),
        stack(rows_s, 3, ls, dsa_t, cache_dsa_k.dtype), stack(rows_s, 4, ls, dsa_t, cache_dsa_v.dtype),
        stack(rows_s, 5, ls, (IDX_DIM,), cache_idx_k.dtype),
    )
```

```python
import functools

import jax
import jax.numpy as jnp
import numpy as np
from jax import lax
from jax.experimental import pallas as pl
from jax.experimental.pallas import tpu as pltpu

HEAD_DIM = 64
FOX_HEADS = 8
FOX_KV_HEADS = 4
DSA_HEADS = 8
DSA_KV_HEADS = 2
IDX_HEADS = 8
IDX_DIM = 64
TOPK_MAX = 256
ROPE_THETA = 500000.0
ROPE_FRACTION_DIV = 4
NORM_EPS = 1e-6
PAGE_SIZE = 128

FOX_Q = FOX_HEADS * HEAD_DIM
FOX_KV = FOX_KV_HEADS * HEAD_DIM
DSA_Q = DSA_HEADS * HEAD_DIM
DSA_KV = DSA_KV_HEADS * HEAD_DIM
IDX_Q = IDX_HEADS * IDX_DIM
FOX_GROUPS = FOX_HEADS // FOX_KV_HEADS
DSA_GROUPS = DSA_HEADS // DSA_KV_HEADS

LANES = 128
SUBLANES = 8
NEG = -(2.0 ** 100)
QK_SCALE = HEAD_DIM ** -0.5
VMEM_LIMIT = 56 * 1024 * 1024

TOKEN_TILE = 512
FOX_Q_TILE = 256
DSA_Q_TILE = 128
FOX_PAGES = 16
DSA_PAGES = 16
BISECT_PASSES = 16

F32 = jnp.float32
BF16 = jnp.bfloat16

_C_FOX = 0
_C_DSA = _C_FOX + FOX_Q + 2 * FOX_KV
_C_IQ = _C_DSA + DSA_Q + 2 * DSA_KV
_C_IK = _C_IQ + IDX_Q
_C_FF = _C_IK + LANES
_C_IW = _C_FF + LANES
_C_END = _C_IW + LANES


def _dot(a, b):
    return jnp.dot(a, b, preferred_element_type=F32)


def _dot_nt(a, b):
    return lax.dot_general(a, b, (((1,), (1,)), ((), ())), preferred_element_type=F32)


def _split3(x):
    hi = x.astype(BF16)
    r1 = x - hi.astype(F32)
    mid = r1.astype(BF16)
    lo = (r1 - mid.astype(F32)).astype(BF16)
    return hi, mid, lo


def _lane_iota(shape):
    return lax.broadcasted_iota(jnp.int32, shape, len(shape) - 1)


def _row_iota(shape):
    return lax.broadcasted_iota(jnp.int32, shape, len(shape) - 2)


def _rope(x, cos, sa, sb):
    w = x.shape[1]
    reps = w // LANES
    if reps > 1:
        cos = jnp.concatenate([cos] * reps, axis=1)
        sa = jnp.concatenate([sa] * reps, axis=1)
        sb = jnp.concatenate([sb] * reps, axis=1)
    up = pltpu.roll(x, w - 8, 1)
    dn = pltpu.roll(x, 8, 1)
    return x * cos + up * sa + dn * sb


def _rmsnorm(x, g):
    ms = jnp.mean(x * x, axis=-1, keepdims=True)
    return x * lax.rsqrt(ms + NORM_EPS) * g


def _proj_body(x_ref, g_ref, w_ref, b_ref, rope_ref, *refs, prompt, blocks_per_seq):
    fq_ref, fk_ref, fv_ref, lf_ref, dq_ref, dk_ref, dv_ref, ik_ref = refs[:8]
    tm = x_ref.shape[0]
    n = _rmsnorm(x_ref[...], g_ref[...]).astype(BF16)
    cos, sa, sb = rope_ref[0], rope_ref[1], rope_ref[2]

    def mm(lo, hi):
        return _dot(n, w_ref[:, lo:hi])

    fq = mm(_C_FOX, _C_FOX + FOX_Q)
    fq_ref[...] = (fq * QK_SCALE).astype(BF16)
    fk = mm(_C_FOX + FOX_Q, _C_FOX + FOX_Q + FOX_KV)
    fk_ref[...] = fk
    fv = mm(_C_FOX + FOX_Q + FOX_KV, _C_DSA)
    fv_ref[...] = fv
    dq = _rope(mm(_C_DSA, _C_DSA + DSA_Q), cos, sa, sb)
    dq_ref[...] = (dq * QK_SCALE).astype(BF16)
    dk = _rope(mm(_C_DSA + DSA_Q, _C_DSA + DSA_Q + DSA_KV), cos, sa, sb)
    dk_ref[...] = dk
    dv = mm(_C_DSA + DSA_Q + DSA_KV, _C_IQ)
    dv_ref[...] = dv
    iq = _rope(mm(_C_IQ, _C_IK), cos, sa, sb) * (IDX_DIM ** -0.5)
    ikg = _rope(mm(_C_IK, _C_FF), cos, sa, sb)
    ik_ref[...] = ikg[:, :IDX_DIM]
    ff = mm(_C_FF, _C_IW) + b_ref[...]
    lf = jnp.minimum(ff, 0.0) - jnp.log(1.0 + jnp.exp(-jnp.abs(ff)))
    lf_ref[...] = lf[:, :FOX_HEADS]
    iw = mm(_C_IW, _C_END) * (IDX_HEADS ** -0.5)

    if not prompt:
        iq_ref, iw_ref = refs[8:]
        iq_ref[...] = iq.astype(BF16)
        iw_ref[...] = iw[:, :IDX_HEADS]
        return
    negc_ref, fkt_ref, fvd_ref, dkt_ref, dvq_ref, iqt_ref, iwt_ref, ikb_ref, carry_ref = refs[8:]
    low = _lane_iota((tm, LANES)) < HEAD_DIM

    @pl.when(pl.program_id(0) % blocks_per_seq == 0)
    def _():
        carry_ref[...] = jnp.zeros_like(carry_ref)

    tri = (_row_iota((tm, tm)) >= _lane_iota((tm, tm))).astype(BF16)
    hi, mid, lo = _split3(lf)
    csum = _dot(tri, hi) + _dot(tri, mid) + _dot(tri, lo) + carry_ref[0:1, :]
    carry_ref[0:1, :] = csum[tm - 1:tm, :]
    negc_ref[0] = -(csum.T[:FOX_HEADS, :])

    fkt = fk.T.astype(BF16)
    for h in range(FOX_KV_HEADS):
        kt = fkt[h * HEAD_DIM:(h + 1) * HEAD_DIM, :]
        fkt_ref[h, 0] = jnp.concatenate([kt, kt], axis=0)
    for p in range(FOX_KV_HEADS // 2):
        pair = fv[:, p * LANES:(p + 1) * LANES]
        swap = pltpu.roll(pair, HEAD_DIM, 1)
        fvd_ref[2 * p] = jnp.where(low, pair, swap).astype(BF16)
        fvd_ref[2 * p + 1] = jnp.where(low, swap, pair).astype(BF16)

    dkt = dk.T.astype(BF16)
    swap = pltpu.roll(dv, HEAD_DIM, 1)
    dv_dup = (jnp.where(low, dv, swap).astype(BF16), jnp.where(low, swap, dv).astype(BF16))
    for h in range(DSA_KV_HEADS):
        kt = dkt[h * HEAD_DIM:(h + 1) * HEAD_DIM, :]
        dkt_ref[h, 0] = jnp.concatenate([kt, kt, kt, kt], axis=0)
        dvq_ref[h] = jnp.concatenate([dv_dup[h], dv_dup[h]], axis=1)

    iqt_ref[...] = iq.T.astype(BF16)
    iwt_ref[...] = iw.T[:IDX_HEADS, :]
    ikb_ref[0] = ikg[:, :IDX_DIM].astype(BF16)


def _proj(x2d, g1, w1, bias, rope_tab, *, prompt, seq_len, tm):
    m, d = x2d.shape
    nblk = m // tm
    npos = rope_tab.shape[1] // tm
    row = lambda i: (i, 0)
    col = lambda i: (0, i)
    outs = [
        (jax.ShapeDtypeStruct((m, FOX_Q), BF16), pl.BlockSpec((tm, FOX_Q), row)),
        (jax.ShapeDtypeStruct((m, FOX_KV), F32), pl.BlockSpec((tm, FOX_KV), row)),
        (jax.ShapeDtypeStruct((m, FOX_KV), F32), pl.BlockSpec((tm, FOX_KV), row)),
        (jax.ShapeDtypeStruct((m, FOX_HEADS), F32), pl.BlockSpec((tm, FOX_HEADS), row)),
        (jax.ShapeDtypeStruct((m, DSA_Q), BF16), pl.BlockSpec((tm, DSA_Q), row)),
        (jax.ShapeDtypeStruct((m, DSA_KV), F32), pl.BlockSpec((tm, DSA_KV), row)),
        (jax.ShapeDtypeStruct((m, DSA_KV), F32), pl.BlockSpec((tm, DSA_KV), row)),
        (jax.ShapeDtypeStruct((m, IDX_DIM), F32), pl.BlockSpec((tm, IDX_DIM), row)),
    ]
    scratch = []
    if prompt:
        outs += [
            (jax.ShapeDtypeStruct((nblk, FOX_HEADS, tm), F32), pl.BlockSpec((1, FOX_HEADS, tm), lambda i: (i, 0, 0))),
            (jax.ShapeDtypeStruct((FOX_KV_HEADS, nblk, 2 * HEAD_DIM, tm), BF16),
             pl.BlockSpec((FOX_KV_HEADS, 1, 2 * HEAD_DIM, tm), lambda i: (0, i, 0, 0))),
            (jax.ShapeDtypeStruct((FOX_KV_HEADS, m, 2 * HEAD_DIM), BF16),
             pl.BlockSpec((FOX_KV_HEADS, tm, 2 * HEAD_DIM), lambda i: (0, i, 0))),
            (jax.ShapeDtypeStruct((DSA_KV_HEADS, nblk, 4 * HEAD_DIM, tm), BF16),
             pl.BlockSpec((DSA_KV_HEADS, 1, 4 * HEAD_DIM, tm), lambda i: (0, i, 0, 0))),
            (jax.ShapeDtypeStruct((DSA_KV_HEADS, m, 4 * HEAD_DIM), BF16),
             pl.BlockSpec((DSA_KV_HEADS, tm, 4 * HEAD_DIM), lambda i: (0, i, 0))),
            (jax.ShapeDtypeStruct((IDX_Q, m), BF16), pl.BlockSpec((IDX_Q, tm), col)),
            (jax.ShapeDtypeStruct((IDX_HEADS, m), F32), pl.BlockSpec((IDX_HEADS, tm), col)),
            (jax.ShapeDtypeStruct((nblk, tm, IDX_DIM), BF16), pl.BlockSpec((1, tm, IDX_DIM), lambda i: (i, 0, 0))),
        ]
        scratch = [pltpu.VMEM((SUBLANES, LANES), F32)]
    else:
        outs += [
            (jax.ShapeDtypeStruct((m, IDX_Q), BF16), pl.BlockSpec((tm, IDX_Q), row)),
            (jax.ShapeDtypeStruct((m, IDX_HEADS), F32), pl.BlockSpec((tm, IDX_HEADS), row)),
        ]
    const = lambda i: (0, 0)
    return pl.pallas_call(
        functools.partial(_proj_body, prompt=prompt, blocks_per_seq=seq_len // tm),
        grid=(nblk,),
        in_specs=[
            pl.BlockSpec((tm, d), row),
            pl.BlockSpec((1, d), const),
            pl.BlockSpec((d, _C_END), const, pipeline_mode=pl.Buffered(1)),
            pl.BlockSpec((1, LANES), const),
            pl.BlockSpec((3, tm, LANES), lambda i: (0, i % npos, 0)),
        ],
        out_specs=[o[1] for o in outs],
        out_shape=[o[0] for o in outs],
        scratch_shapes=scratch,
        compiler_params=pltpu.CompilerParams(dimension_semantics=("arbitrary",), vmem_limit_bytes=VMEM_LIMIT),
        name="proj_prompt" if prompt else "proj_sample",
    )(x2d, g1, w1, bias, rope_tab)


def _softmax_step(s, m, l, acc, v):
    m_new = jnp.maximum(m, jnp.max(s, axis=-1, keepdims=True))
    alpha = jnp.exp(m - m_new)
    p = jnp.exp(s - m_new)
    l = alpha * l + jnp.sum(p, axis=-1, keepdims=True)
    acc = alpha * acc + _dot(p.astype(BF16), v)
    return m_new, l, acc


def _softmax_init(rows, width):
    return jnp.full((rows, 1), NEG, F32), jnp.zeros((rows, 1), F32), jnp.zeros((rows, width), F32)


def _pick_head_lanes(acc, tq, groups):
    lane = _lane_iota((tq, groups * HEAD_DIM))
    out = acc[0:tq]
    for g in range(1, groups):
        out = jnp.where(lane >= g * HEAD_DIM, acc[g * tq:(g + 1) * tq], out)
    return out


def _stack_heads(q, tq, groups):
    lane = _lane_iota((tq, groups * HEAD_DIM))
    zero = jnp.zeros_like(q)
    return jnp.concatenate(
        [jnp.where((lane >= g * HEAD_DIM) & (lane < (g + 1) * HEAD_DIM), q, zero) for g in range(groups)], axis=0)


def _fox_prompt_body(q_ref, kt_ref, v_ref, negc_ref, o_ref, *, tq, tk):
    i = pl.program_id(1)
    nfull = (i * tq) // tk
    g = FOX_GROUPS
    w = g * HEAD_DIM
    qpos = i * tq + (_row_iota((g * tq, tk)) & (tq - 1))
    kloc = _lane_iota((g * tq, tk))
    qs = [_stack_heads(q_ref[:, h * w:(h + 1) * w], tq, g) for h in range(FOX_KV_HEADS)]

    def step(j, carry, masked):
        nb = negc_ref[j]
        off = pl.multiple_of(j * tk, tk)
        out = []
        for h in range(FOX_KV_HEADS):
            bias = jnp.concatenate(
                [jnp.broadcast_to(nb[h * g + k:h * g + k + 1, :], (tq, tk)) for k in range(g)], axis=0)
            s = _dot(qs[h], kt_ref[h, j]) + bias
            if masked:
                s = jnp.where(j * tk + kloc <= qpos, s, NEG)
            out.append(_softmax_step(s, *carry[h], v_ref[h, pl.ds(off, tk), :]))
        return tuple(out)

    init = tuple(_softmax_init(g * tq, w) for _ in range(FOX_KV_HEADS))
    carry = lax.fori_loop(0, nfull, lambda j, c: step(j, c, False), init)
    carry = step(nfull, carry, True)
    for h in range(FOX_KV_HEADS):
        m, l, acc = carry[h]
        o_ref[:, h * w:(h + 1) * w] = _pick_head_lanes(acc / l, tq, g).astype(BF16)


def _fox_prompt(fq, fkt, fvd, negc, *, batch, seq_len, tq, tk):
    m = fq.shape[0]
    nq = seq_len // tq
    nkb = seq_len // tk
    return pl.pallas_call(
        functools.partial(_fox_prompt_body, tq=tq, tk=tk),
        grid=(batch, nq),
        in_specs=[
            pl.BlockSpec((tq, FOX_Q), lambda b, i: (b * nq + i, 0)),
            pl.BlockSpec((FOX_KV_HEADS, nkb, 2 * HEAD_DIM, tk), lambda b, i: (0, b, 0, 0), pipeline_mode=pl.Buffered(1)),
            pl.BlockSpec((FOX_KV_HEADS, seq_len, 2 * HEAD_DIM), lambda b, i: (0, b, 0), pipeline_mode=pl.Buffered(1)),
            pl.BlockSpec((nkb, FOX_HEADS, tk), lambda b, i: (b, 0, 0), pipeline_mode=pl.Buffered(1)),
        ],
        out_specs=pl.BlockSpec((tq, FOX_Q), lambda b, i: (b * nq + i, 0)),
        out_shape=jax.ShapeDtypeStruct((m, FOX_Q), BF16),
        compiler_params=pltpu.CompilerParams(dimension_semantics=("arbitrary", "arbitrary"), vmem_limit_bytes=VMEM_LIMIT),
        name="fox_prompt",
    )(fq, fkt, fvd, negc)


def _topk_threshold(reduce_fn, idx_of, nvalid, ksel, n_idx):
    kf = jnp.float32(ksel)
    inf = jnp.float32(jnp.inf)
    one, zero = jnp.float32(1.0), jnp.float32(0.0)
    shape = nvalid.shape

    def count_ge(t):
        return reduce_fn(lambda c, blk: jnp.where(blk >= t, one, zero), "sum")

    row_max = reduce_fn(lambda c, blk: blk, "max")
    row_min = reduce_fn(lambda c, blk: jnp.where(blk > -inf, blk, inf), "min")
    all_sel = jnp.where(nvalid <= kf, one, zero)

    def bisect(_, st):
        lo, clo, hi, chi, done = st
        t = jnp.where(hi == inf, row_max, 0.5 * (lo + hi))
        cnt = count_ge(t)
        upd = (done < 0.5) & (t > lo) & (t < hi)
        ge = cnt >= kf
        lo = jnp.where(upd & ge, t, lo)
        clo = jnp.where(upd & ge, cnt, clo)
        hi = jnp.where(upd & (cnt < kf), t, hi)
        chi = jnp.where(upd & (cnt < kf), cnt, chi)
        return lo, clo, hi, chi, jnp.where(clo == kf, one, done)

    st = (row_min, nvalid, jnp.full(shape, inf, F32), jnp.zeros(shape, F32), all_sel)
    st = lax.fori_loop(0, BISECT_PASSES, bisect, st)

    def snap(st):
        lo, clo, hi, chi, done = st
        v = reduce_fn(lambda c, blk: jnp.where(blk < hi, blk, -inf), "max")
        cv = count_ge(v)
        fin = (done < 0.5) & (cv >= kf)
        go = (done < 0.5) & (cv < kf)
        lo = jnp.where(fin, v, lo)
        clo = jnp.where(fin, cv, clo)
        hi = jnp.where(go, v, hi)
        chi = jnp.where(go, cv, chi)
        return lo, clo, hi, chi, jnp.where(fin, one, done)

    lo, clo, hi, chi, done = lax.while_loop(lambda st: jnp.min(st[4]) < 0.5, snap, st)

    thr = jnp.where(all_sel > 0.5, -inf, lo)
    need_cut = jnp.where((all_sel < 0.5) & (clo > kf), one, zero)
    need = kf - chi
    big = jnp.float32(n_idx)

    def cut_search():
        def body(_, st):
            jlo, jhi = st
            mid = jnp.floor(0.5 * (jlo + jhi))
            cnt = reduce_fn(lambda c, blk: jnp.where((blk == thr) & (idx_of(c) <= mid), one, zero), "sum")
            ok = cnt >= need
            return jnp.where(ok, jlo, mid), jnp.where(ok, mid, jhi)

        npass = int(np.ceil(np.log2(n_idx))) + 1
        _, jhi = lax.fori_loop(0, npass, body, (jnp.full(shape, -1.0, F32), jnp.full(shape, n_idx - 1.0, F32)))
        return jhi

    jcut = lax.cond(jnp.max(need_cut) > 0.5, cut_search, lambda: jnp.full(shape, -1.0, F32))
    jcut = jnp.where(need_cut > 0.5, jcut, jnp.where(all_sel > 0.5, -1.0, big))
    return thr, jcut


_REDUCE_OPS = {"sum": (jnp.add, 0.0, jnp.sum), "max": (jnp.maximum, -np.inf, jnp.max), "min": (jnp.minimum, np.inf, jnp.min)}


def _tree(parts, op):
    while len(parts) > 1:
        parts = [op(parts[k], parts[k + 1]) if k + 1 < len(parts) else parts[k] for k in range(0, len(parts), 2)]
    return parts[0]


def _row_major_reducer(s_ref, nch):
    _, rows, ch = s_ref.shape

    def reduce_fn(elem_fn, kind):
        op, init, lane_reduce = _REDUCE_OPS[kind]

        def body(c, acc):
            x = elem_fn(c, s_ref[c])
            return op(acc, _tree([x[:, k * LANES:(k + 1) * LANES] for k in range(ch // LANES)], op))

        acc = jnp.full((rows, LANES), init, F32)
        if isinstance(nch, int):
            for c in range(nch):
                acc = body(c, acc)
        else:
            acc = lax.fori_loop(0, nch, body, acc)
        return lane_reduce(acc, axis=1, keepdims=True)

    return reduce_fn


def _lane_major_reducer(s_ref, nch):
    _, ch, _ = s_ref.shape

    def reduce_fn(elem_fn, kind):
        op, init, sub_reduce = _REDUCE_OPS[kind]

        def body(c, acc):
            x = elem_fn(c, s_ref[c])
            return op(acc, _tree([x[k * SUBLANES:(k + 1) * SUBLANES, :] for k in range(ch // SUBLANES)], op))

        acc = lax.fori_loop(0, nch, body, jnp.full((SUBLANES, LANES), init, F32))
        return sub_reduce(acc, axis=0, keepdims=True)

    return reduce_fn


def _dsa_prompt_body(iqt_ref, iwt_ref, dq_ref, ikb_ref, dkt_ref, dv_ref, o_ref, s_ref, bias_ref, *, tq, ch, ksel):
    i = pl.program_id(1)
    nch = (i * tq + tq - 1) // ch + 1
    g = DSA_GROUPS
    w = g * HEAD_DIM

    iqt = iqt_ref[...]
    rhs = [jnp.concatenate([iqt[(2 * p) * IDX_DIM:(2 * p + 1) * IDX_DIM, :], iqt[(2 * p + 1) * IDX_DIM:(2 * p + 2) * IDX_DIM, :]],
                           axis=1) for p in range(IDX_HEADS // 2)]
    iwt = iwt_ref[...]
    qpos = i * tq + _lane_iota((ch, tq))
    kloc = _row_iota((ch, tq))

    def score_chunk(c, _):
        keys = ikb_ref[c]
        acc = None
        for p in range(IDX_HEADS // 2):
            d = jnp.maximum(_dot(keys, rhs[p]), 0.0)
            term = d[:, :tq] * iwt[2 * p:2 * p + 1, :] + d[:, tq:] * iwt[2 * p + 1:2 * p + 2, :]
            acc = term if acc is None else acc + term
        s_ref[c] = jnp.where(c * ch + kloc <= qpos, acc, -jnp.inf)
        return 0

    lax.fori_loop(0, nch, score_chunk, 0)

    nvalid = (i * tq + _lane_iota((1, tq)) + 1).astype(F32)
    idx_of = lambda c: (c * ch + kloc).astype(F32)
    thr, jcut = _topk_threshold(_lane_major_reducer(s_ref, nch), idx_of, nvalid, ksel, s_ref.shape[0] * ch)

    def write_bias(c, _):
        blk = s_ref[c]
        sel = (blk > thr) | ((blk == thr) & (idx_of(c) <= jcut))
        bias_ref[c] = jnp.where(sel, 0.0, NEG).T
        return 0

    lax.fori_loop(0, nch, write_bias, 0)

    qs = [_stack_heads(dq_ref[:, h * w:(h + 1) * w], tq, g) for h in range(DSA_KV_HEADS)]

    def step(c, carry):
        b = bias_ref[c]
        bias = jnp.concatenate([b] * g, axis=0)
        off = pl.multiple_of(c * ch, ch)
        return tuple(_softmax_step(_dot(qs[h], dkt_ref[h, c]) + bias, *carry[h], dv_ref[h, pl.ds(off, ch), :])
                     for h in range(DSA_KV_HEADS))

    carry = lax.fori_loop(0, nch, step, tuple(_softmax_init(g * tq, w) for _ in range(DSA_KV_HEADS)))
    for h in range(DSA_KV_HEADS):
        m, l, acc = carry[h]
        o_ref[:, h * w:(h + 1) * w] = _pick_head_lanes(acc / l, tq, g).astype(BF16)


def _dsa_prompt(iqt, iwt, dq, ikb, dkt, dvq, *, batch, seq_len, tq, ch):
    m = dq.shape[0]
    nq = seq_len // tq
    nkb = seq_len // ch
    ksel = min(TOPK_MAX, seq_len // 4)
    row = lambda b, i: (b * nq + i, 0)
    col = lambda b, i: (0, b * nq + i)
    return pl.pallas_call(
        functools.partial(_dsa_prompt_body, tq=tq, ch=ch, ksel=ksel),
        grid=(batch, nq),
        in_specs=[
            pl.BlockSpec((IDX_Q, tq), col),
            pl.BlockSpec((IDX_HEADS, tq), col),
            pl.BlockSpec((tq, DSA_Q), row),
            pl.BlockSpec((nkb, ch, IDX_DIM), lambda b, i: (b, 0, 0), pipeline_mode=pl.Buffered(1)),
            pl.BlockSpec((DSA_KV_HEADS, nkb, 4 * HEAD_DIM, ch), lambda b, i: (0, b, 0, 0), pipeline_mode=pl.Buffered(1)),
            pl.BlockSpec((DSA_KV_HEADS, seq_len, 4 * HEAD_DIM), lambda b, i: (0, b, 0), pipeline_mode=pl.Buffered(1)),
        ],
        out_specs=pl.BlockSpec((tq, DSA_Q), row),
        out_shape=jax.ShapeDtypeStruct((m, DSA_Q), BF16),
        scratch_shapes=[pltpu.VMEM((nkb, ch, tq), F32), pltpu.VMEM((nkb, tq, ch), F32)],
        compiler_params=pltpu.CompilerParams(dimension_semantics=("arbitrary", "arbitrary"), vmem_limit_bytes=VMEM_LIMIT),
        name="dsa_prompt",
    )(iqt, iwt, dq, ikb, dkt, dvq)


def _page_specs(block, layer, pages):
    def spec(k):
        return pl.BlockSpec(block, lambda b, j, pt: (layer, pt[b, j * pages + k]) + (0,) * (len(block) - 2))
    return [spec(k) for k in range(pages)]


def _pad_rows(x, rows):
    return jnp.concatenate([x, jnp.zeros((rows - x.shape[0], x.shape[1]), x.dtype)], axis=0)


def _page_rows(page_refs):
    return jnp.concatenate([r[0, 0].reshape(-1, HEAD_DIM) for r in page_refs], axis=0).astype(BF16)


def _own_head(rows, cols, kv_heads, groups):
    shift = groups.bit_length() - 1
    head = _row_iota((rows, cols)) & (kv_heads * groups - 1)
    return (_lane_iota((rows, cols)) & (kv_heads - 1)) == (head >> shift)


def _sample_softmax_step(s, v, m_ref, l_ref, acc_ref):
    m, l, acc = _softmax_step(s, m_ref[...], l_ref[...], acc_ref[...], v)
    m_ref[...] = m
    l_ref[...] = l
    acc_ref[...] = acc


def _fox_sample_body(pt_ref, q_ref, kn_ref, vn_ref, lfn_ref, *refs, dec_seq):
    np_ = FOX_PAGES
    kp, vp, lp = refs[0:np_], refs[np_:2 * np_], refs[2 * np_:3 * np_]
    o_ref, m_ref, l_ref, acc_ref, pre_ref = refs[3 * np_:]
    j = pl.program_id(1)
    rows = dec_seq * FOX_HEADS
    cols = PAGE_SIZE * FOX_KV_HEADS

    @pl.when(j == 0)
    def _():
        m_ref[...] = jnp.full_like(m_ref, NEG)
        l_ref[...] = jnp.zeros_like(l_ref)
        acc_ref[...] = jnp.zeros_like(acc_ref)
        pre_ref[...] = jnp.zeros_like(pre_ref)

    cum = (_row_iota((PAGE_SIZE, cols)) <= (_lane_iota((PAGE_SIZE, cols)) >> (FOX_KV_HEADS.bit_length() - 1))).astype(BF16)

    def cum_cols(lf, pre):
        hi, mid, lo = _split3(lf)
        return _dot(hi, cum) + _dot(mid, cum) + _dot(lo, cum) + pre

    q = q_ref[0]
    cs = []
    pre = pre_ref[...]
    for k in range(np_):
        c = cum_cols(lp[k][0, 0].T, pre)
        pre = jnp.broadcast_to(c[:, cols - 1:cols], pre.shape)
        cs.append(c)
    pre_ref[...] = pre
    c_all = jnp.concatenate(cs, axis=1)
    s = _dot_nt(q, _page_rows(kp)) - jnp.concatenate([c_all] * dec_seq, axis=0)
    s = jnp.where(_own_head(rows, np_ * cols, FOX_KV_HEADS, FOX_GROUPS), s, NEG)
    _sample_softmax_step(s, _page_rows(vp), m_ref, l_ref, acc_ref)

    @pl.when(j == pl.num_programs(1) - 1)
    def _():
        c = cum_cols(lfn_ref[0], pre_ref[...])
        s = _dot_nt(q, kn_ref[0].astype(BF16)) - jnp.concatenate([c] * dec_seq, axis=0)
        key = _lane_iota((rows, cols)) >> (FOX_KV_HEADS.bit_length() - 1)
        tok = _row_iota((rows, cols)) >> (FOX_HEADS.bit_length() - 1)
        s = jnp.where(_own_head(rows, cols, FOX_KV_HEADS, FOX_GROUPS) & (key <= tok), s, NEG)
        _sample_softmax_step(s, vn_ref[0].astype(BF16), m_ref, l_ref, acc_ref)
        o_ref[0] = acc_ref[...] / l_ref[...]


def _fox_sample(page_table, q_rows, k_new, v_new, lf_new, k_pool, v_pool, lf_pool, *, layer, dec_seq):
    db, n_pages = page_table.shape
    ng = n_pages // FOX_PAGES
    rows = dec_seq * FOX_HEADS
    cols = PAGE_SIZE * FOX_KV_HEADS
    per_b = lambda b, j, pt: (b, 0, 0)
    grid_spec = pltpu.PrefetchScalarGridSpec(
        num_scalar_prefetch=1,
        grid=(db, ng),
        in_specs=[
            pl.BlockSpec((1, rows, HEAD_DIM), per_b),
            pl.BlockSpec((1, cols, HEAD_DIM), per_b),
            pl.BlockSpec((1, cols, HEAD_DIM), per_b),
            pl.BlockSpec((1, FOX_HEADS, PAGE_SIZE), per_b),
        ] + _page_specs((1, 1, PAGE_SIZE, FOX_KV_HEADS, HEAD_DIM), layer, FOX_PAGES)
        + _page_specs((1, 1, PAGE_SIZE, FOX_KV_HEADS, HEAD_DIM), layer, FOX_PAGES)
        + _page_specs((1, 1, PAGE_SIZE, FOX_HEADS), layer, FOX_PAGES),
        out_specs=pl.BlockSpec((1, rows, HEAD_DIM), per_b),
        scratch_shapes=[pltpu.VMEM((rows, 1), F32), pltpu.VMEM((rows, 1), F32), pltpu.VMEM((rows, HEAD_DIM), F32),
                        pltpu.VMEM((FOX_HEADS, cols), F32)],
    )
    return pl.pallas_call(
        functools.partial(_fox_sample_body, dec_seq=dec_seq),
        grid_spec=grid_spec,
        out_shape=jax.ShapeDtypeStruct((db, rows, HEAD_DIM), F32),
        compiler_params=pltpu.CompilerParams(dimension_semantics=("arbitrary", "arbitrary"), vmem_limit_bytes=VMEM_LIMIT),
        name="fox_sample",
    )(page_table, q_rows, k_new, v_new, lf_new, *([k_pool] * FOX_PAGES), *([v_pool] * FOX_PAGES),
      *([lf_pool] * FOX_PAGES))


def _idx_sample_body(pt_ref, q_ref, w_ref, kn_ref, *refs, dec_seq, past, ksel):
    np_ = DSA_PAGES
    kp = refs[0:np_]
    bias_ref, s_ref = refs[np_:]
    j = pl.program_id(1)
    ng = pl.num_programs(1)
    ch = np_ * PAGE_SIZE
    q = q_ref[0]
    w = w_ref[0]

    def scores(keys):
        d = jnp.maximum(_dot_nt(q, keys.astype(BF16)), 0.0) * w
        shape = (SUBLANES, d.shape[1])
        tok = _row_iota(shape)
        out = jnp.full(shape, -jnp.inf, F32)
        for t in range(dec_seq):
            per_tok = jnp.sum(d[t * IDX_HEADS:(t + 1) * IDX_HEADS], axis=0, keepdims=True)
            out = jnp.where(tok == t, jnp.broadcast_to(per_tok, shape), out)
        return out

    s_ref[j] = scores(jnp.concatenate([r[0, 0] for r in kp], axis=0))

    @pl.when(j == ng - 1)
    def _():
        sn = scores(_pad_rows(kn_ref[0], PAGE_SIZE))
        sn = jnp.where(_lane_iota((SUBLANES, PAGE_SIZE)) <= _row_iota((SUBLANES, PAGE_SIZE)), sn, -jnp.inf)
        s_ref[ng] = jnp.concatenate([sn, jnp.full((SUBLANES, ch - PAGE_SIZE), -jnp.inf, F32)], axis=1)
        tok = _row_iota((SUBLANES, 1))
        nvalid = jnp.where(tok < dec_seq, (past + 1 + tok).astype(F32), 0.0)
        idx_of = lambda c: (c * ch + _lane_iota((SUBLANES, ch))).astype(F32)
        thr, jcut = _topk_threshold(_row_major_reducer(s_ref, ng + 1), idx_of, nvalid, ksel, s_ref.shape[0] * ch)
        for c in range(s_ref.shape[0]):
            blk = s_ref[c]
            sel = (blk > thr) | ((blk == thr) & (idx_of(c) <= jcut))
            bias_ref[0, c] = jnp.where(sel, 0.0, NEG)


def _idx_sample(page_table, iq_rows, iw_rows, ik_new, ik_pool, *, layer, dec_seq, ksel):
    db, n_pages = page_table.shape
    ng = n_pages // DSA_PAGES
    ch = DSA_PAGES * PAGE_SIZE
    rows = dec_seq * IDX_HEADS
    per_b = lambda b, j, pt: (b, 0, 0)
    grid_spec = pltpu.PrefetchScalarGridSpec(
        num_scalar_prefetch=1,
        grid=(db, ng),
        in_specs=[
            pl.BlockSpec((1, rows, IDX_DIM), per_b),
            pl.BlockSpec((1, rows, 1), per_b),
            pl.BlockSpec((1, SUBLANES, IDX_DIM), per_b),
        ] + _page_specs((1, 1, PAGE_SIZE, IDX_DIM), layer, DSA_PAGES),
        out_specs=pl.BlockSpec((1, ng + 1, SUBLANES, ch), lambda b, j, pt: (b, 0, 0, 0)),
        scratch_shapes=[pltpu.VMEM((ng + 1, SUBLANES, ch), F32)],
    )
    return pl.pallas_call(
        functools.partial(_idx_sample_body, dec_seq=dec_seq, past=n_pages * PAGE_SIZE, ksel=ksel),
        grid_spec=grid_spec,
        out_shape=jax.ShapeDtypeStruct((db, ng + 1, SUBLANES, ch), F32),
        compiler_params=pltpu.CompilerParams(dimension_semantics=("arbitrary", "arbitrary"), vmem_limit_bytes=VMEM_LIMIT),
        name="idx_sample",
    )(page_table, iq_rows, iw_rows, ik_new, *([ik_pool] * DSA_PAGES))


def _dsa_sample_body(pt_ref, q_ref, bias_ref, kn_ref, vn_ref, *refs, dec_seq):
    np_ = DSA_PAGES
    kp, vp = refs[0:np_], refs[np_:2 * np_]
    o_ref, m_ref, l_ref, acc_ref = refs[2 * np_:]
    j = pl.program_id(1)
    ng = pl.num_programs(1)
    rows = dec_seq * DSA_HEADS
    cols = PAGE_SIZE * DSA_KV_HEADS

    @pl.when(j == 0)
    def _():
        m_ref[...] = jnp.full_like(m_ref, NEG)
        l_ref[...] = jnp.zeros_like(l_ref)
        acc_ref[...] = jnp.zeros_like(acc_ref)

    spread = (_row_iota((PAGE_SIZE, cols)) == (_lane_iota((PAGE_SIZE, cols)) >> (DSA_KV_HEADS.bit_length() - 1))).astype(BF16)

    def mask_rows(b):
        wide = jnp.concatenate([_dot(b[:, p * PAGE_SIZE:(p + 1) * PAGE_SIZE].astype(BF16), spread)
                                for p in range(b.shape[1] // PAGE_SIZE)], axis=1)
        shape = (rows, wide.shape[1])
        tok = _row_iota(shape) >> (DSA_HEADS.bit_length() - 1)
        out = jnp.broadcast_to(wide[0:1, :], shape)
        for t in range(1, dec_seq):
            out = jnp.where(tok == t, jnp.broadcast_to(wide[t:t + 1, :], shape), out)
        return jnp.where(_own_head(rows, wide.shape[1], DSA_KV_HEADS, DSA_GROUPS), out, NEG)

    q = q_ref[0]
    _sample_softmax_step(_dot_nt(q, _page_rows(kp)) + mask_rows(bias_ref[0, j]), _page_rows(vp), m_ref, l_ref, acc_ref)

    @pl.when(j == ng - 1)
    def _():
        s = _dot_nt(q, kn_ref[0].astype(BF16)) + mask_rows(bias_ref[0, ng, :, 0:PAGE_SIZE])
        _sample_softmax_step(s, vn_ref[0].astype(BF16), m_ref, l_ref, acc_ref)
        o_ref[0] = acc_ref[...] / l_ref[...]


def _dsa_sample(page_table, q_rows, bias, k_new, v_new, k_pool, v_pool, *, layer, dec_seq):
    db, n_pages = page_table.shape
    ng = n_pages // DSA_PAGES
    ch = DSA_PAGES * PAGE_SIZE
    rows = dec_seq * DSA_HEADS
    cols = PAGE_SIZE * DSA_KV_HEADS
    per_b = lambda b, j, pt: (b, 0, 0)
    grid_spec = pltpu.PrefetchScalarGridSpec(
        num_scalar_prefetch=1,
        grid=(db, ng),
        in_specs=[
            pl.BlockSpec((1, rows, HEAD_DIM), per_b),
            pl.BlockSpec((1, ng + 1, SUBLANES, ch), lambda b, j, pt: (b, 0, 0, 0)),
            pl.BlockSpec((1, cols, HEAD_DIM), per_b),
            pl.BlockSpec((1, cols, HEAD_DIM), per_b),
        ] + _page_specs((1, 1, PAGE_SIZE, DSA_KV_HEADS, HEAD_DIM), layer, DSA_PAGES)
        + _page_specs((1, 1, PAGE_SIZE, DSA_KV_HEADS, HEAD_DIM), layer, DSA_PAGES),
        out_specs=pl.BlockSpec((1, rows, HEAD_DIM), per_b),
        scratch_shapes=[pltpu.VMEM((rows, 1), F32), pltpu.VMEM((rows, 1), F32), pltpu.VMEM((rows, HEAD_DIM), F32)],
    )
    return pl.pallas_call(
        functools.partial(_dsa_sample_body, dec_seq=dec_seq),
        grid_spec=grid_spec,
        out_shape=jax.ShapeDtypeStruct((db, rows, HEAD_DIM), F32),
        compiler_params=pltpu.CompilerParams(dimension_semantics=("arbitrary", "arbitrary"), vmem_limit_bytes=VMEM_LIMIT),
        name="dsa_sample",
    )(page_table, q_rows, bias, k_new, v_new, *([k_pool] * DSA_PAGES), *([v_pool] * DSA_PAGES))


_FF_CHUNK = 1024


def _post_body(x_ref, of_ref, od_ref, g1_ref, wg_ref, wof_ref, wod_ref, wout_ref, g2_ref, wup_ref, wdn_ref, gf_ref, y_ref):
    x = x_ref[...]
    d = x.shape[1]
    n = _rmsnorm(x, g1_ref[...]).astype(BF16)
    gate_a = jax.nn.sigmoid(_dot(n, wg_ref[:, 0:d]))
    gate_b = jax.nn.sigmoid(_dot(n, wg_ref[:, d:2 * d]))
    mix = gate_a * _dot(of_ref[...], wof_ref[...]) + gate_b * _dot(od_ref[...], wod_ref[...])
    h = x + _dot(mix.astype(BF16), wout_ref[...])
    n2 = _rmsnorm(h, g2_ref[...]).astype(BF16)
    y = h
    for c in range(wup_ref.shape[1] // _FF_CHUNK):
        u = jnp.maximum(_dot(n2, wup_ref[:, c * _FF_CHUNK:(c + 1) * _FF_CHUNK]), 0.0)
        y = y + _dot((u * u).astype(BF16), wdn_ref[c * _FF_CHUNK:(c + 1) * _FF_CHUNK, :])
    y_ref[...] = _rmsnorm(y, gf_ref[...])


def _post(x2d, o_fox, o_dsa, g1, wg, wof, wod, wout, g2, wup, wdn, gf, *, tm):
    m, d = x2d.shape
    row = lambda i: (i, 0)
    const = lambda i: (0, 0)

    def whole(a):
        return pl.BlockSpec(a.shape, const, pipeline_mode=pl.Buffered(1))

    return pl.pallas_call(
        _post_body,
        grid=(m // tm,),
        in_specs=[pl.BlockSpec((tm, d), row), pl.BlockSpec((tm, FOX_Q), row), pl.BlockSpec((tm, DSA_Q), row),
                  whole(g1), whole(wg), whole(wof), whole(wod), whole(wout), whole(g2), whole(wup), whole(wdn), whole(gf)],
        out_specs=pl.BlockSpec((tm, d), row),
        out_shape=jax.ShapeDtypeStruct((m, d), F32),
        compiler_params=pltpu.CompilerParams(dimension_semantics=("arbitrary",), vmem_limit_bytes=VMEM_LIMIT),
        name="post",
    )(x2d, o_fox, o_dsa, g1, wg, wof, wod, wout, g2, wup, wdn, gf)


def _rope_tables(pos):
    r = HEAD_DIM // ROPE_FRACTION_DIV
    half = r // 2
    inv_freq = jnp.float32(ROPE_THETA) ** (-jnp.arange(half, dtype=F32) * (2.0 / r))
    ang = pos.astype(F32)[:, None] * inv_freq[None, :]
    cos, sin = jnp.cos(ang), jnp.sin(ang)
    p = pos.shape[0]
    one = jnp.ones((p, HEAD_DIM - r), F32)
    zero_r = jnp.zeros((p, HEAD_DIM - r), F32)
    zero_h = jnp.zeros((p, half), F32)
    c = jnp.concatenate([cos, cos, one], axis=1)
    sa = jnp.concatenate([-sin, zero_h, zero_r], axis=1)
    sb = jnp.concatenate([zero_h, sin, zero_r], axis=1)
    return jnp.stack([jnp.tile(t, (1, LANES // HEAD_DIM)) for t in (c, sa, sb)], axis=0)


def _pad_cols(w, width):
    return jnp.pad(w, ((0, 0), (0, width - w.shape[1])))


def _repack_w_in(w_in, d_model):
    sizes = (FOX_Q, FOX_KV, FOX_KV, FOX_HEADS, DSA_Q, DSA_KV, DSA_KV, IDX_Q, IDX_DIM, IDX_HEADS, d_model, d_model)
    cuts = [int(c) for c in np.cumsum(sizes)[:-1]]
    fq, fk, fv, ff, dq, dk, dv, iq, ik, iw, ga, gb = jnp.split(w_in, cuts, axis=-1)
    w1 = jnp.concatenate([fq, fk, fv, dq, dk, dv, iq, _pad_cols(ik, LANES), _pad_cols(ff, LANES), _pad_cols(iw, LANES)], axis=1)
    return w1.astype(BF16), jnp.concatenate([ga, gb], axis=1).astype(BF16)


def _new_rows(x, db, kv_heads):
    xh = x.reshape(db, -1, HEAD_DIM)
    return jnp.pad(xh, ((0, 0), (0, PAGE_SIZE * kv_heads - xh.shape[1]), (0, 0)))


def kernel(x_prompt, x_sample, cache_fox_k, cache_fox_v, cache_fox_logf, cache_dsa_k, cache_dsa_v, cache_idx_k,
           page_table, norm1_g, w_in, b_forget, w_o_fox, w_o_dsa, w_out, norm2_g, w_up, w_down, final_norm_g):
    batch, seq_len, d = x_prompt.shape
    db, dec_seq, _ = x_sample.shape
    depth = norm1_g.shape[0]
    n_pages = page_table.shape[1]
    past = n_pages * PAGE_SIZE
    assert dec_seq <= SUBLANES and dec_seq & (dec_seq - 1) == 0
    assert n_pages % FOX_PAGES == 0 and n_pages % DSA_PAGES == 0

    tm_p = min(TOKEN_TILE, seq_len)
    tm_s = db * dec_seq
    rope_p = _rope_tables(jnp.arange(seq_len))
    rope_s = _rope_tables(jnp.tile(past + jnp.arange(dec_seq), db))

    xp = x_prompt.reshape(batch * seq_len, d)
    xs = x_sample.reshape(db * dec_seq, d)
    rows_p, rows_s = [], []
    for l in range(depth):
        w1, wg = _repack_w_in(w_in[l], d)
        bias = _pad_cols(b_forget[l][None, :], LANES)
        g1 = norm1_g[l][None, :]
        g2 = norm2_g[l][None, :]
        gf = final_norm_g[None, :] if l == depth - 1 else jnp.ones((1, d), F32)
        post_w = (g1, wg, w_o_fox[l].astype(BF16), w_o_dsa[l].astype(BF16), w_out[l].astype(BF16), g2,
                  w_up[l].astype(BF16), w_down[l].astype(BF16), gf)

        (fq, fk, fv, lf, dq, dk, dv, ik, negc, fkt, fvd, dkt, dvq, iqt, iwt, ikb) = _proj(
            xp, g1, w1, bias, rope_p, prompt=True, seq_len=seq_len, tm=tm_p)
        o_fox = _fox_prompt(fq, fkt, fvd, negc, batch=batch, seq_len=seq_len, tq=min(FOX_Q_TILE, seq_len), tk=tm_p)
        o_dsa = _dsa_prompt(iqt, iwt, dq, ikb, dkt, dvq, batch=batch, seq_len=seq_len, tq=min(DSA_Q_TILE, seq_len), ch=tm_p)
        xp = _post(xp, o_fox, o_dsa, *post_w, tm=tm_p)
        rows_p.append((fk, fv, lf, dk, dv, ik))

        (fq, fk, fv, lf, dq, dk, dv, ik, iq, iw) = _proj(xs, g1, w1, bias, rope_s, prompt=False, seq_len=tm_s, tm=tm_s)
        lf_new = jnp.pad(jnp.swapaxes(lf.reshape(db, dec_seq, FOX_HEADS), 1, 2), ((0, 0), (0, 0), (0, PAGE_SIZE - dec_seq)))
        o_fox = _fox_sample(
            page_table, fq.reshape(db, dec_seq * FOX_HEADS, HEAD_DIM), _new_rows(fk, db, FOX_KV_HEADS),
            _new_rows(fv, db, FOX_KV_HEADS), lf_new, cache_fox_k, cache_fox_v, cache_fox_logf, layer=l, dec_seq=dec_seq)
        ksel = min(TOPK_MAX, (past + dec_seq) // 4)
        ik_new = jnp.pad(ik.reshape(db, dec_seq, IDX_DIM), ((0, 0), (0, SUBLANES - dec_seq), (0, 0)))
        sel_bias = _idx_sample(
            page_table, iq.reshape(db, dec_seq * IDX_HEADS, IDX_DIM), iw.reshape(db, dec_seq * IDX_HEADS, 1), ik_new,
            cache_idx_k, layer=l, dec_seq=dec_seq, ksel=ksel)
        o_dsa = _dsa_sample(
            page_table, dq.reshape(db, dec_seq * DSA_HEADS, HEAD_DIM), sel_bias, _new_rows(dk, db, DSA_KV_HEADS),
            _new_rows(dv, db, DSA_KV_HEADS), cache_dsa_k, cache_dsa_v, layer=l, dec_seq=dec_seq)
        xs = _post(xs, o_fox.reshape(tm_s, FOX_Q).astype(BF16), o_dsa.reshape(tm_s, DSA_Q).astype(BF16), *post_w, tm=tm_s)
        rows_s.append((fk, fv, lf, dk, dv, ik))

    def stack(rows, idx, lead, tail, dtype):
        return jnp.stack([r[idx].reshape(lead + tail) for r in rows], axis=0).astype(dtype)

    lp, ls = (batch, seq_len), (db, dec_seq)
    fox_t, dsa_t = (FOX_KV_HEADS, HEAD_DIM), (DSA_KV_HEADS, HEAD_DIM)
    return (
        xp.reshape(batch, seq_len, d), xs.reshape(db, dec_seq, d),
        stack(rows_p, 0, lp, fox_t, cache_fox_k.dtype), stack(rows_p, 1, lp, fox_t, cache_fox_v.dtype),
        stack(rows_p, 2, lp, (FOX_HEADS,), cache_fox_logf.dtype),
        stack(rows_p, 3, lp, dsa_t, cache_dsa_k.dtype), stack(rows_p, 4, lp, dsa_t, cache_dsa_v.dtype),
        stack(rows_p, 5, lp, (IDX_DIM,), cache_idx_k.dtype),
        stack(rows_s, 0, ls, fox_t, cache_fox_k.dtype), stack(rows_s, 1, ls, fox_t, cache_fox_v.dtype),
        stack(rows_s, 2, ls, (FOX_HEADS,), cache_fox_logf.dtype),
        stack(rows_s, 3, ls, dsa_t, cache_dsa_k.dtype), stack(rows_s, 4, ls, dsa_t, cache_dsa_v.dtype),
        stack(rows_s, 5, ls, (IDX_DIM,), cache_idx_k.dtype),
    )
```

```python
import functools

import jax
import jax.numpy as jnp
import numpy as np
from jax import lax
from jax.experimental import pallas as pl
from jax.experimental.pallas import tpu as pltpu

HEAD_DIM = 64
FOX_HEADS = 8
FOX_KV_HEADS = 4
DSA_HEADS = 8
DSA_KV_HEADS = 2
IDX_HEADS = 8
IDX_DIM = 64
TOPK_MAX = 256
ROPE_THETA = 500000.0
ROPE_FRACTION_DIV = 4
NORM_EPS = 1e-6
PAGE_SIZE = 128

FOX_Q = FOX_HEADS * HEAD_DIM
FOX_KV = FOX_KV_HEADS * HEAD_DIM
DSA_Q = DSA_HEADS * HEAD_DIM
DSA_KV = DSA_KV_HEADS * HEAD_DIM
IDX_Q = IDX_HEADS * IDX_DIM
FOX_GROUPS = FOX_HEADS // FOX_KV_HEADS
DSA_GROUPS = DSA_HEADS // DSA_KV_HEADS

LANES = 128
SUBLANES = 8
NEG = -(2.0 ** 100)
QK_SCALE = HEAD_DIM ** -0.5
VMEM_LIMIT = 56 * 1024 * 1024

TOKEN_TILE = 512
FOX_Q_TILE = 256
DSA_Q_TILE = 128
FOX_PAGES = 16
DSA_PAGES = 16
BISECT_PASSES = 16

F32 = jnp.float32
BF16 = jnp.bfloat16

_C_FOX = 0
_C_DSA = _C_FOX + FOX_Q + 2 * FOX_KV
_C_IQ = _C_DSA + DSA_Q + 2 * DSA_KV
_C_IK = _C_IQ + IDX_Q
_C_FF = _C_IK + LANES
_C_IW = _C_FF + LANES
_C_END = _C_IW + LANES


def _dot(a, b):
    return jnp.dot(a, b, preferred_element_type=F32)


def _dot_nt(a, b):
    return lax.dot_general(a, b, (((1,), (1,)), ((), ())), preferred_element_type=F32)


def _split3(x):
    hi = x.astype(BF16)
    r1 = x - hi.astype(F32)
    mid = r1.astype(BF16)
    lo = (r1 - mid.astype(F32)).astype(BF16)
    return hi, mid, lo


def _lane_iota(shape):
    return lax.broadcasted_iota(jnp.int32, shape, len(shape) - 1)


def _row_iota(shape):
    return lax.broadcasted_iota(jnp.int32, shape, len(shape) - 2)


def _rope(x, cos, sa, sb):
    w = x.shape[1]
    reps = w // LANES
    if reps > 1:
        cos = jnp.concatenate([cos] * reps, axis=1)
        sa = jnp.concatenate([sa] * reps, axis=1)
        sb = jnp.concatenate([sb] * reps, axis=1)
    up = pltpu.roll(x, w - 8, 1)
    dn = pltpu.roll(x, 8, 1)
    return x * cos + up * sa + dn * sb


def _rmsnorm(x, g):
    ms = jnp.mean(x * x, axis=-1, keepdims=True)
    return x * lax.rsqrt(ms + NORM_EPS) * g


def _proj_body(x_ref, g_ref, w_ref, b_ref, rope_ref, *refs, prompt, blocks_per_seq):
    fq_ref, dq_ref = refs[:2]
    tm = x_ref.shape[0]
    n = _rmsnorm(x_ref[...], g_ref[...]).astype(BF16)
    cos, sa, sb = rope_ref[0], rope_ref[1], rope_ref[2]

    def mm(lo, hi):
        return _dot(n, w_ref[:, lo:hi])

    fq = mm(_C_FOX, _C_FOX + FOX_Q)
    fq_ref[...] = (fq * QK_SCALE).astype(BF16)
    fk = mm(_C_FOX + FOX_Q, _C_FOX + FOX_Q + FOX_KV)
    fv = mm(_C_FOX + FOX_Q + FOX_KV, _C_DSA)
    dq = _rope(mm(_C_DSA, _C_DSA + DSA_Q), cos, sa, sb)
    dq_ref[...] = (dq * QK_SCALE).astype(BF16)
    dk = _rope(mm(_C_DSA + DSA_Q, _C_DSA + DSA_Q + DSA_KV), cos, sa, sb)
    dv = mm(_C_DSA + DSA_Q + DSA_KV, _C_IQ)
    iq = _rope(mm(_C_IQ, _C_IK), cos, sa, sb) * (IDX_DIM ** -0.5)
    ikg = _rope(mm(_C_IK, _C_FF), cos, sa, sb)
    ff = mm(_C_FF, _C_IW) + b_ref[...]
    lf = jnp.minimum(ff, 0.0) - jnp.log(1.0 + jnp.exp(-jnp.abs(ff)))
    iw = mm(_C_IW, _C_END) * (IDX_HEADS ** -0.5)

    if not prompt:
        fk_ref, fv_ref, lf_ref, dk_ref, dv_ref, ik_ref, iq_ref, iw_ref = refs[2:]
        fk_ref[...] = fk
        fv_ref[...] = fv
        lf_ref[...] = lf[:, :FOX_HEADS]
        dk_ref[...] = dk
        dv_ref[...] = dv
        ik_ref[...] = ikg[:, :IDX_DIM]
        iq_ref[...] = iq.astype(BF16)
        iw_ref[...] = iw[:, :IDX_HEADS]
        return
    (fkx_ref, fvx_ref, lfx_ref, dkx_ref, dvx_ref, ikx_ref,
     negc_ref, fkt_ref, fvd_ref, dkt_ref, dvq_ref, iqt_ref, iwt_ref, ikb_ref, carry_ref) = refs[2:]
    low = _lane_iota((tm, LANES)) < HEAD_DIM

    fk_t, dk_t, ik_t = fk.T, dk.T, ikg.T
    fkx_ref[0] = fk_t
    fvx_ref[0] = fv.T
    lfx_ref[0] = lf.T[:FOX_HEADS, :]
    dkx_ref[0] = dk_t
    dvx_ref[0] = dv.T
    ikx_ref[0] = ik_t[:IDX_DIM, :]

    @pl.when(pl.program_id(0) % blocks_per_seq == 0)
    def _():
        carry_ref[...] = jnp.zeros_like(carry_ref)

    tri = (_row_iota((tm, tm)) >= _lane_iota((tm, tm))).astype(BF16)
    hi, mid, lo = _split3(lf)
    csum = _dot(tri, hi) + _dot(tri, mid) + _dot(tri, lo) + carry_ref[0:1, :]
    carry_ref[0:1, :] = csum[tm - 1:tm, :]
    negc_ref[0] = -(csum.T[:FOX_HEADS, :])

    fkt = fk_t.astype(BF16)
    for h in range(FOX_KV_HEADS):
        kt = fkt[h * HEAD_DIM:(h + 1) * HEAD_DIM, :]
        fkt_ref[h, 0] = jnp.concatenate([kt, kt], axis=0)
    for p in range(FOX_KV_HEADS // 2):
        pair = fv[:, p * LANES:(p + 1) * LANES]
        swap = pltpu.roll(pair, HEAD_DIM, 1)
        fvd_ref[2 * p] = jnp.where(low, pair, swap).astype(BF16)
        fvd_ref[2 * p + 1] = jnp.where(low, swap, pair).astype(BF16)

    dkt = dk_t.astype(BF16)
    swap = pltpu.roll(dv, HEAD_DIM, 1)
    dv_dup = (jnp.where(low, dv, swap).astype(BF16), jnp.where(low, swap, dv).astype(BF16))
    for h in range(DSA_KV_HEADS):
        kt = dkt[h * HEAD_DIM:(h + 1) * HEAD_DIM, :]
        dkt_ref[h, 0] = jnp.concatenate([kt, kt, kt, kt], axis=0)
        dvq_ref[h] = jnp.concatenate([dv_dup[h], dv_dup[h]], axis=1)

    iqt_ref[...] = iq.T.astype(BF16)
    iwt_ref[...] = iw.T[:IDX_HEADS, :]
    ikb_ref[0] = ikg[:, :IDX_DIM].astype(BF16)


def _proj(x2d, g1, w1, bias, rope_tab, *, prompt, seq_len, tm):
    m, d = x2d.shape
    nblk = m // tm
    npos = rope_tab.shape[1] // tm
    row = lambda i: (i, 0)
    col = lambda i: (0, i)
    outs = [
        (jax.ShapeDtypeStruct((m, FOX_Q), BF16), pl.BlockSpec((tm, FOX_Q), row)),
        (jax.ShapeDtypeStruct((m, DSA_Q), BF16), pl.BlockSpec((tm, DSA_Q), row)),
    ]
    cache_widths = (FOX_KV, FOX_KV, FOX_HEADS, DSA_KV, DSA_KV, IDX_DIM)
    scratch = []
    if prompt:
        bps = seq_len // tm
        feat_major = lambda i: (i // bps, 0, i % bps)
        outs += [(jax.ShapeDtypeStruct((m // seq_len, w, seq_len), F32), pl.BlockSpec((1, w, tm), feat_major))
                 for w in cache_widths]
        outs += [
            (jax.ShapeDtypeStruct((nblk, FOX_HEADS, tm), F32), pl.BlockSpec((1, FOX_HEADS, tm), lambda i: (i, 0, 0))),
            (jax.ShapeDtypeStruct((FOX_KV_HEADS, nblk, 2 * HEAD_DIM, tm), BF16),
             pl.BlockSpec((FOX_KV_HEADS, 1, 2 * HEAD_DIM, tm), lambda i: (0, i, 0, 0))),
            (jax.ShapeDtypeStruct((FOX_KV_HEADS, m, 2 * HEAD_DIM), BF16),
             pl.BlockSpec((FOX_KV_HEADS, tm, 2 * HEAD_DIM), lambda i: (0, i, 0))),
            (jax.ShapeDtypeStruct((DSA_KV_HEADS, nblk, 4 * HEAD_DIM, tm), BF16),
             pl.BlockSpec((DSA_KV_HEADS, 1, 4 * HEAD_DIM, tm), lambda i: (0, i, 0, 0))),
            (jax.ShapeDtypeStruct((DSA_KV_HEADS, m, 4 * HEAD_DIM), BF16),
             pl.BlockSpec((DSA_KV_HEADS, tm, 4 * HEAD_DIM), lambda i: (0, i, 0))),
            (jax.ShapeDtypeStruct((IDX_Q, m), BF16), pl.BlockSpec((IDX_Q, tm), col)),
            (jax.ShapeDtypeStruct((IDX_HEADS, m), F32), pl.BlockSpec((IDX_HEADS, tm), col)),
            (jax.ShapeDtypeStruct((nblk, tm, IDX_DIM), BF16), pl.BlockSpec((1, tm, IDX_DIM), lambda i: (i, 0, 0))),
        ]
        scratch = [pltpu.VMEM((SUBLANES, LANES), F32)]
    else:
        outs += [(jax.ShapeDtypeStruct((m, w), F32), pl.BlockSpec((tm, w), row)) for w in cache_widths]
        outs += [
            (jax.ShapeDtypeStruct((m, IDX_Q), BF16), pl.BlockSpec((tm, IDX_Q), row)),
            (jax.ShapeDtypeStruct((m, IDX_HEADS), F32), pl.BlockSpec((tm, IDX_HEADS), row)),
        ]
    const = lambda i: (0, 0)
    return pl.pallas_call(
        functools.partial(_proj_body, prompt=prompt, blocks_per_seq=seq_len // tm),
        grid=(nblk,),
        in_specs=[
            pl.BlockSpec((tm, d), row),
            pl.BlockSpec((1, d), const),
            pl.BlockSpec((d, _C_END), const, pipeline_mode=pl.Buffered(1)),
            pl.BlockSpec((1, LANES), const),
            pl.BlockSpec((3, tm, LANES), lambda i: (0, i % npos, 0)),
        ],
        out_specs=[o[1] for o in outs],
        out_shape=[o[0] for o in outs],
        scratch_shapes=scratch,
        compiler_params=pltpu.CompilerParams(dimension_semantics=("arbitrary",), vmem_limit_bytes=VMEM_LIMIT),
        name="proj_prompt" if prompt else "proj_sample",
    )(x2d, g1, w1, bias, rope_tab)


def _softmax_step(s, m, l, acc, v):
    m_new = jnp.maximum(m, jnp.max(s, axis=-1, keepdims=True))
    alpha = jnp.exp(m - m_new)
    p = jnp.exp(s - m_new)
    l = alpha * l + jnp.sum(p, axis=-1, keepdims=True)
    acc = alpha * acc + _dot(p.astype(BF16), v)
    return m_new, l, acc


def _softmax_init(rows, width):
    return jnp.full((rows, 1), NEG, F32), jnp.zeros((rows, 1), F32), jnp.zeros((rows, width), F32)


def _pick_head_lanes(acc, tq, groups):
    lane = _lane_iota((tq, groups * HEAD_DIM))
    out = acc[0:tq]
    for g in range(1, groups):
        out = jnp.where(lane >= g * HEAD_DIM, acc[g * tq:(g + 1) * tq], out)
    return out


def _stack_heads(q, tq, groups):
    lane = _lane_iota((tq, groups * HEAD_DIM))
    zero = jnp.zeros_like(q)
    return jnp.concatenate(
        [jnp.where((lane >= g * HEAD_DIM) & (lane < (g + 1) * HEAD_DIM), q, zero) for g in range(groups)], axis=0)


def _fox_prompt_body(q_ref, kt_ref, v_ref, negc_ref, o_ref, *, tq, tk):
    i = pl.program_id(1)
    nfull = (i * tq) // tk
    g = FOX_GROUPS
    w = g * HEAD_DIM
    qpos = i * tq + (_row_iota((g * tq, tk)) & (tq - 1))
    kloc = _lane_iota((g * tq, tk))
    qs = [_stack_heads(q_ref[:, h * w:(h + 1) * w], tq, g) for h in range(FOX_KV_HEADS)]

    def step(j, carry, masked):
        nb = negc_ref[j]
        off = pl.multiple_of(j * tk, tk)
        out = []
        for h in range(FOX_KV_HEADS):
            bias = jnp.concatenate(
                [jnp.broadcast_to(nb[h * g + k:h * g + k + 1, :], (tq, tk)) for k in range(g)], axis=0)
            s = _dot(qs[h], kt_ref[h, j]) + bias
            if masked:
                s = jnp.where(j * tk + kloc <= qpos, s, NEG)
            out.append(_softmax_step(s, *carry[h], v_ref[h, pl.ds(off, tk), :]))
        return tuple(out)

    init = tuple(_softmax_init(g * tq, w) for _ in range(FOX_KV_HEADS))
    carry = lax.fori_loop(0, nfull, lambda j, c: step(j, c, False), init)
    carry = step(nfull, carry, True)
    for h in range(FOX_KV_HEADS):
        m, l, acc = carry[h]
        o_ref[:, h * w:(h + 1) * w] = _pick_head_lanes(acc / l, tq, g).astype(BF16)


def _fox_prompt(fq, fkt, fvd, negc, *, batch, seq_len, tq, tk):
    m = fq.shape[0]
    nq = seq_len // tq
    nkb = seq_len // tk
    return pl.pallas_call(
        functools.partial(_fox_prompt_body, tq=tq, tk=tk),
        grid=(batch, nq),
        in_specs=[
            pl.BlockSpec((tq, FOX_Q), lambda b, i: (b * nq + i, 0)),
            pl.BlockSpec((FOX_KV_HEADS, nkb, 2 * HEAD_DIM, tk), lambda b, i: (0, b, 0, 0), pipeline_mode=pl.Buffered(1)),
            pl.BlockSpec((FOX_KV_HEADS, seq_len, 2 * HEAD_DIM), lambda b, i: (0, b, 0), pipeline_mode=pl.Buffered(1)),
            pl.BlockSpec((nkb, FOX_HEADS, tk), lambda b, i: (b, 0, 0), pipeline_mode=pl.Buffered(1)),
        ],
        out_specs=pl.BlockSpec((tq, FOX_Q), lambda b, i: (b * nq + i, 0)),
        out_shape=jax.ShapeDtypeStruct((m, FOX_Q), BF16),
        compiler_params=pltpu.CompilerParams(dimension_semantics=("arbitrary", "arbitrary"), vmem_limit_bytes=VMEM_LIMIT),
        name="fox_prompt",
    )(fq, fkt, fvd, negc)


def _topk_threshold(reduce_fn, idx_of, nvalid, ksel, n_idx):
    kf = jnp.float32(ksel)
    inf = jnp.float32(jnp.inf)
    one, zero = jnp.float32(1.0), jnp.float32(0.0)
    shape = nvalid.shape

    def count_ge(t):
        return reduce_fn(lambda c, blk: jnp.where(blk >= t, one, zero), "sum")

    row_max = reduce_fn(lambda c, blk: blk, "max")
    row_min = reduce_fn(lambda c, blk: jnp.where(blk > -inf, blk, inf), "min")
    all_sel = jnp.where(nvalid <= kf, one, zero)

    def bisect(_, st):
        lo, clo, hi, chi, done = st
        t = jnp.where(hi == inf, row_max, 0.5 * (lo + hi))
        cnt = count_ge(t)
        upd = (done < 0.5) & (t > lo) & (t < hi)
        ge = cnt >= kf
        lo = jnp.where(upd & ge, t, lo)
        clo = jnp.where(upd & ge, cnt, clo)
        hi = jnp.where(upd & (cnt < kf), t, hi)
        chi = jnp.where(upd & (cnt < kf), cnt, chi)
        return lo, clo, hi, chi, jnp.where(clo == kf, one, done)

    st = (row_min, nvalid, jnp.full(shape, inf, F32), jnp.zeros(shape, F32), all_sel)
    st = lax.fori_loop(0, BISECT_PASSES, bisect, st)

    def snap(st):
        lo, clo, hi, chi, done = st
        v = reduce_fn(lambda c, blk: jnp.where(blk < hi, blk, -inf), "max")
        cv = count_ge(v)
        fin = (done < 0.5) & (cv >= kf)
        go = (done < 0.5) & (cv < kf)
        lo = jnp.where(fin, v, lo)
        clo = jnp.where(fin, cv, clo)
        hi = jnp.where(go, v, hi)
        chi = jnp.where(go, cv, chi)
        return lo, clo, hi, chi, jnp.where(fin, one, done)

    lo, clo, hi, chi, done = lax.while_loop(lambda st: jnp.min(st[4]) < 0.5, snap, st)

    thr = jnp.where(all_sel > 0.5, -inf, lo)
    need_cut = jnp.where((all_sel < 0.5) & (clo > kf), one, zero)
    need = kf - chi
    big = jnp.float32(n_idx)

    def cut_search():
        def body(_, st):
            jlo, jhi = st
            mid = jnp.floor(0.5 * (jlo + jhi))
            cnt = reduce_fn(lambda c, blk: jnp.where((blk == thr) & (idx_of(c) <= mid), one, zero), "sum")
            ok = cnt >= need
            return jnp.where(ok, jlo, mid), jnp.where(ok, mid, jhi)

        npass = int(np.ceil(np.log2(n_idx))) + 1
        _, jhi = lax.fori_loop(0, npass, body, (jnp.full(shape, -1.0, F32), jnp.full(shape, n_idx - 1.0, F32)))
        return jhi

    jcut = lax.cond(jnp.max(need_cut) > 0.5, cut_search, lambda: jnp.full(shape, -1.0, F32))
    jcut = jnp.where(need_cut > 0.5, jcut, jnp.where(all_sel > 0.5, -1.0, big))
    return thr, jcut


_REDUCE_OPS = {"sum": (jnp.add, 0.0, jnp.sum), "max": (jnp.maximum, -np.inf, jnp.max), "min": (jnp.minimum, np.inf, jnp.min)}


def _tree(parts, op):
    while len(parts) > 1:
        parts = [op(parts[k], parts[k + 1]) if k + 1 < len(parts) else parts[k] for k in range(0, len(parts), 2)]
    return parts[0]


def _row_major_reducer(s_ref, nch):
    _, rows, ch = s_ref.shape

    def reduce_fn(elem_fn, kind):
        op, init, lane_reduce = _REDUCE_OPS[kind]

        def body(c, acc):
            x = elem_fn(c, s_ref[c])
            return op(acc, _tree([x[:, k * LANES:(k + 1) * LANES] for k in range(ch // LANES)], op))

        acc = jnp.full((rows, LANES), init, F32)
        if isinstance(nch, int):
            for c in range(nch):
                acc = body(c, acc)
        else:
            acc = lax.fori_loop(0, nch, body, acc)
        return lane_reduce(acc, axis=1, keepdims=True)

    return reduce_fn


def _lane_major_reducer(s_ref, nch):
    _, ch, _ = s_ref.shape

    def reduce_fn(elem_fn, kind):
        op, init, sub_reduce = _REDUCE_OPS[kind]

        def body(c, acc):
            x = elem_fn(c, s_ref[c])
            return op(acc, _tree([x[k * SUBLANES:(k + 1) * SUBLANES, :] for k in range(ch // SUBLANES)], op))

        acc = lax.fori_loop(0, nch, body, jnp.full((SUBLANES, LANES), init, F32))
        return sub_reduce(acc, axis=0, keepdims=True)

    return reduce_fn


def _dsa_prompt_body(iqt_ref, iwt_ref, dq_ref, ikb_ref, dkt_ref, dv_ref, o_ref, s_ref, bias_ref, *, tq, ch, ksel):
    i = pl.program_id(1)
    nch = (i * tq + tq - 1) // ch + 1
    g = DSA_GROUPS
    w = g * HEAD_DIM

    iqt = iqt_ref[...]
    rhs = [jnp.concatenate([iqt[(2 * p) * IDX_DIM:(2 * p + 1) * IDX_DIM, :], iqt[(2 * p + 1) * IDX_DIM:(2 * p + 2) * IDX_DIM, :]],
                           axis=1) for p in range(IDX_HEADS // 2)]
    iwt = iwt_ref[...]
    qpos = i * tq + _lane_iota((ch, tq))
    kloc = _row_iota((ch, tq))

    def score_chunk(c, _):
        keys = ikb_ref[c]
        acc = None
        for p in range(IDX_HEADS // 2):
            d = jnp.maximum(_dot(keys, rhs[p]), 0.0)
            term = d[:, :tq] * iwt[2 * p:2 * p + 1, :] + d[:, tq:] * iwt[2 * p + 1:2 * p + 2, :]
            acc = term if acc is None else acc + term
        s_ref[c] = jnp.where(c * ch + kloc <= qpos, acc, -jnp.inf)
        return 0

    lax.fori_loop(0, nch, score_chunk, 0)

    nvalid = (i * tq + _lane_iota((1, tq)) + 1).astype(F32)
    idx_of = lambda c: (c * ch + kloc).astype(F32)
    thr, jcut = _topk_threshold(_lane_major_reducer(s_ref, nch), idx_of, nvalid, ksel, s_ref.shape[0] * ch)

    def write_bias(c, _):
        blk = s_ref[c]
        sel = (blk > thr) | ((blk == thr) & (idx_of(c) <= jcut))
        bias_ref[c] = jnp.where(sel, 0.0, NEG).T
        return 0

    lax.fori_loop(0, nch, write_bias, 0)

    qs = [_stack_heads(dq_ref[:, h * w:(h + 1) * w], tq, g) for h in range(DSA_KV_HEADS)]

    def step(c, carry):
        b = bias_ref[c]
        bias = jnp.concatenate([b] * g, axis=0)
        off = pl.multiple_of(c * ch, ch)
        return tuple(_softmax_step(_dot(qs[h], dkt_ref[h, c]) + bias, *carry[h], dv_ref[h, pl.ds(off, ch), :])
                     for h in range(DSA_KV_HEADS))

    carry = lax.fori_loop(0, nch, step, tuple(_softmax_init(g * tq, w) for _ in range(DSA_KV_HEADS)))
    for h in range(DSA_KV_HEADS):
        m, l, acc = carry[h]
        o_ref[:, h * w:(h + 1) * w] = _pick_head_lanes(acc / l, tq, g).astype(BF16)


def _dsa_prompt(iqt, iwt, dq, ikb, dkt, dvq, *, batch, seq_len, tq, ch):
    m = dq.shape[0]
    nq = seq_len // tq
    nkb = seq_len // ch
    ksel = min(TOPK_MAX, seq_len // 4)
    row = lambda b, i: (b * nq + i, 0)
    col = lambda b, i: (0, b * nq + i)
    return pl.pallas_call(
        functools.partial(_dsa_prompt_body, tq=tq, ch=ch, ksel=ksel),
        grid=(batch, nq),
        in_specs=[
            pl.BlockSpec((IDX_Q, tq), col),
            pl.BlockSpec((IDX_HEADS, tq), col),
            pl.BlockSpec((tq, DSA_Q), row),
            pl.BlockSpec((nkb, ch, IDX_DIM), lambda b, i: (b, 0, 0), pipeline_mode=pl.Buffered(1)),
            pl.BlockSpec((DSA_KV_HEADS, nkb, 4 * HEAD_DIM, ch), lambda b, i: (0, b, 0, 0), pipeline_mode=pl.Buffered(1)),
            pl.BlockSpec((DSA_KV_HEADS, seq_len, 4 * HEAD_DIM), lambda b, i: (0, b, 0), pipeline_mode=pl.Buffered(1)),
        ],
        out_specs=pl.BlockSpec((tq, DSA_Q), row),
        out_shape=jax.ShapeDtypeStruct((m, DSA_Q), BF16),
        scratch_shapes=[pltpu.VMEM((nkb, ch, tq), F32), pltpu.VMEM((nkb, tq, ch), F32)],
        compiler_params=pltpu.CompilerParams(dimension_semantics=("arbitrary", "arbitrary"), vmem_limit_bytes=VMEM_LIMIT),
        name="dsa_prompt",
    )(iqt, iwt, dq, ikb, dkt, dvq)


def _page_specs(block, layer, pages):
    def spec(k):
        return pl.BlockSpec(block, lambda b, j, pt: (layer, pt[b, j * pages + k]) + (0,) * (len(block) - 2))
    return [spec(k) for k in range(pages)]


def _page_tiles(page_refs):
    return jnp.concatenate([r[0, 0] for r in page_refs], axis=1).astype(BF16)


def _sample_softmax_step(s, vt, m_ref, l_ref, acc_ref):
    m = m_ref[...]
    m_new = jnp.maximum(m, jnp.max(s, axis=-1, keepdims=True))
    alpha = jnp.exp(m - m_new)
    p = jnp.exp(s - m_new)
    m_ref[...] = m_new
    l_ref[...] = alpha * l_ref[...] + jnp.sum(p, axis=-1, keepdims=True)
    acc_ref[...] = alpha * acc_ref[...] + _dot_nt(p.astype(BF16), vt)


def _fox_sample_body(pt_ref, q_ref, kn_ref, vn_ref, lfn_ref, *refs, dec_seq):
    np_ = FOX_PAGES
    kp, vp, lp = refs[0:np_], refs[np_:2 * np_], refs[2 * np_:3 * np_]
    o_ref, m_ref, l_ref, acc_ref, pre_ref = refs[3 * np_:]
    j = pl.program_id(1)
    rows = dec_seq * FOX_HEADS

    @pl.when(j == 0)
    def _():
        m_ref[...] = jnp.full_like(m_ref, NEG)
        l_ref[...] = jnp.zeros_like(l_ref)
        acc_ref[...] = jnp.zeros_like(acc_ref)
        pre_ref[...] = jnp.zeros_like(pre_ref)

    triu = (_row_iota((PAGE_SIZE, PAGE_SIZE)) <= _lane_iota((PAGE_SIZE, PAGE_SIZE))).astype(BF16)

    def cum_keys(lf):
        r = lf.shape[0]
        parts = _dot(jnp.concatenate(_split3(lf), axis=0), triu)
        return parts[0:r] + parts[r:2 * r] + parts[2 * r:3 * r]

    q = q_ref[0]
    within = cum_keys(jnp.concatenate([r[0, 0] for r in lp], axis=0))
    cs = []
    pre = pre_ref[...]
    for k in range(np_):
        c = within[k * FOX_HEADS:(k + 1) * FOX_HEADS, :] + pre
        pre = jnp.broadcast_to(c[:, PAGE_SIZE - 1:PAGE_SIZE], pre.shape)
        cs.append(c)
    pre_ref[...] = pre
    c_all = jnp.concatenate(cs, axis=1)
    s = _dot(q, _page_tiles(kp)) - jnp.concatenate([c_all] * dec_seq, axis=0)
    _sample_softmax_step(s, _page_tiles(vp), m_ref, l_ref, acc_ref)

    @pl.when(j == pl.num_programs(1) - 1)
    def _():
        c = cum_keys(lfn_ref[0]) + pre_ref[...]
        s = _dot(q, kn_ref[0].astype(BF16)) - jnp.concatenate([c] * dec_seq, axis=0)
        tok = _row_iota((rows, PAGE_SIZE)) >> (FOX_HEADS.bit_length() - 1)
        s = jnp.where(_lane_iota((rows, PAGE_SIZE)) <= tok, s, NEG)
        _sample_softmax_step(s, vn_ref[0].astype(BF16), m_ref, l_ref, acc_ref)
        o_ref[0] = acc_ref[...] / l_ref[...]


def _fox_sample(page_table, q_rows, k_new, v_new, lf_new, k_pool, v_pool, lf_pool, *, layer, dec_seq):
    db, n_pages = page_table.shape
    ng = n_pages // FOX_PAGES
    rows = dec_seq * FOX_HEADS
    per_b = lambda b, j, pt: (b, 0, 0)
    grid_spec = pltpu.PrefetchScalarGridSpec(
        num_scalar_prefetch=1,
        grid=(db, ng),
        in_specs=[
            pl.BlockSpec((1, rows, FOX_KV), per_b),
            pl.BlockSpec((1, FOX_KV, PAGE_SIZE), per_b),
            pl.BlockSpec((1, FOX_KV, PAGE_SIZE), per_b),
            pl.BlockSpec((1, FOX_HEADS, PAGE_SIZE), per_b),
        ] + _page_specs((1, 1, FOX_KV, PAGE_SIZE), layer, FOX_PAGES)
        + _page_specs((1, 1, FOX_KV, PAGE_SIZE), layer, FOX_PAGES)
        + _page_specs((1, 1, FOX_HEADS, PAGE_SIZE), layer, FOX_PAGES),
        out_specs=pl.BlockSpec((1, rows, FOX_KV), per_b),
        scratch_shapes=[pltpu.VMEM((rows, 1), F32), pltpu.VMEM((rows, 1), F32), pltpu.VMEM((rows, FOX_KV), F32),
                        pltpu.VMEM((FOX_HEADS, PAGE_SIZE), F32)],
    )
    return pl.pallas_call(
        functools.partial(_fox_sample_body, dec_seq=dec_seq),
        grid_spec=grid_spec,
        out_shape=jax.ShapeDtypeStruct((db, rows, FOX_KV), F32),
        compiler_params=pltpu.CompilerParams(dimension_semantics=("arbitrary", "arbitrary"), vmem_limit_bytes=VMEM_LIMIT),
        name="fox_sample",
    )(page_table, q_rows, k_new, v_new, lf_new, *([k_pool] * FOX_PAGES), *([v_pool] * FOX_PAGES),
      *([lf_pool] * FOX_PAGES))


def _idx_sample_body(pt_ref, q_ref, w_ref, kn_ref, *refs, dec_seq, past, ksel):
    np_ = DSA_PAGES
    kp = refs[0:np_]
    bias_ref, s_ref = refs[np_:]
    j = pl.program_id(1)
    ng = pl.num_programs(1)
    ch = np_ * PAGE_SIZE
    q = q_ref[0]
    w = w_ref[0]

    def scores(keys_t):
        d = jnp.maximum(_dot(q, keys_t), 0.0) * w
        shape = (SUBLANES, d.shape[1])
        tok = _row_iota(shape)
        out = jnp.full(shape, -jnp.inf, F32)
        for t in range(dec_seq):
            per_tok = jnp.sum(d[t * IDX_HEADS:(t + 1) * IDX_HEADS], axis=0, keepdims=True)
            out = jnp.where(tok == t, jnp.broadcast_to(per_tok, shape), out)
        return out

    s_ref[j] = scores(_page_tiles(kp))

    @pl.when(j == ng - 1)
    def _():
        sn = scores(kn_ref[0].astype(BF16))
        sn = jnp.where(_lane_iota((SUBLANES, PAGE_SIZE)) <= _row_iota((SUBLANES, PAGE_SIZE)), sn, -jnp.inf)
        s_ref[ng] = jnp.concatenate([sn, jnp.full((SUBLANES, ch - PAGE_SIZE), -jnp.inf, F32)], axis=1)
        tok = _row_iota((SUBLANES, 1))
        nvalid = jnp.where(tok < dec_seq, (past + 1 + tok).astype(F32), 0.0)
        idx_of = lambda c: (c * ch + _lane_iota((SUBLANES, ch))).astype(F32)
        thr, jcut = _topk_threshold(_row_major_reducer(s_ref, ng + 1), idx_of, nvalid, ksel, s_ref.shape[0] * ch)
        for c in range(s_ref.shape[0]):
            blk = s_ref[c]
            sel = (blk > thr) | ((blk == thr) & (idx_of(c) <= jcut))
            bias_ref[0, c] = jnp.where(sel, 0.0, NEG)


def _idx_sample(page_table, iq_rows, iw_rows, ik_new, ik_pool, *, layer, dec_seq, ksel):
    db, n_pages = page_table.shape
    ng = n_pages // DSA_PAGES
    ch = DSA_PAGES * PAGE_SIZE
    rows = dec_seq * IDX_HEADS
    per_b = lambda b, j, pt: (b, 0, 0)
    grid_spec = pltpu.PrefetchScalarGridSpec(
        num_scalar_prefetch=1,
        grid=(db, ng),
        in_specs=[
            pl.BlockSpec((1, rows, IDX_DIM), per_b),
            pl.BlockSpec((1, rows, 1), per_b),
            pl.BlockSpec((1, IDX_DIM, PAGE_SIZE), per_b),
        ] + _page_specs((1, 1, IDX_DIM, PAGE_SIZE), layer, DSA_PAGES),
        out_specs=pl.BlockSpec((1, ng + 1, SUBLANES, ch), lambda b, j, pt: (b, 0, 0, 0)),
        scratch_shapes=[pltpu.VMEM((ng + 1, SUBLANES, ch), F32)],
    )
    return pl.pallas_call(
        functools.partial(_idx_sample_body, dec_seq=dec_seq, past=n_pages * PAGE_SIZE, ksel=ksel),
        grid_spec=grid_spec,
        out_shape=jax.ShapeDtypeStruct((db, ng + 1, SUBLANES, ch), F32),
        compiler_params=pltpu.CompilerParams(dimension_semantics=("arbitrary", "arbitrary"), vmem_limit_bytes=VMEM_LIMIT),
        name="idx_sample",
    )(page_table, iq_rows, iw_rows, ik_new, *([ik_pool] * DSA_PAGES))


def _dsa_sample_body(pt_ref, q_ref, bias_ref, kn_ref, vn_ref, *refs, dec_seq):
    np_ = DSA_PAGES
    kp, vp = refs[0:np_], refs[np_:2 * np_]
    o_ref, m_ref, l_ref, acc_ref = refs[2 * np_:]
    j = pl.program_id(1)
    ng = pl.num_programs(1)
    rows = dec_seq * DSA_HEADS

    @pl.when(j == 0)
    def _():
        m_ref[...] = jnp.full_like(m_ref, NEG)
        l_ref[...] = jnp.zeros_like(l_ref)
        acc_ref[...] = jnp.zeros_like(acc_ref)

    def mask_rows(b):
        shape = (rows, b.shape[1])
        tok = _row_iota(shape) >> (DSA_HEADS.bit_length() - 1)
        out = jnp.broadcast_to(b[0:1, :], shape)
        for t in range(1, dec_seq):
            out = jnp.where(tok == t, jnp.broadcast_to(b[t:t + 1, :], shape), out)
        return out

    q = q_ref[0]
    s = _dot(q, _page_tiles(kp)) + mask_rows(bias_ref[0, j])
    _sample_softmax_step(s, _page_tiles(vp), m_ref, l_ref, acc_ref)

    @pl.when(j == ng - 1)
    def _():
        s = _dot(q, kn_ref[0].astype(BF16)) + mask_rows(bias_ref[0, ng, :, 0:PAGE_SIZE])
        _sample_softmax_step(s, vn_ref[0].astype(BF16), m_ref, l_ref, acc_ref)
        o_ref[0] = acc_ref[...] / l_ref[...]


def _dsa_sample(page_table, q_rows, bias, k_new, v_new, k_pool, v_pool, *, layer, dec_seq):
    db, n_pages = page_table.shape
    ng = n_pages // DSA_PAGES
    ch = DSA_PAGES * PAGE_SIZE
    rows = dec_seq * DSA_HEADS
    per_b = lambda b, j, pt: (b, 0, 0)
    grid_spec = pltpu.PrefetchScalarGridSpec(
        num_scalar_prefetch=1,
        grid=(db, ng),
        in_specs=[
            pl.BlockSpec((1, rows, DSA_KV), per_b),
            pl.BlockSpec((1, ng + 1, SUBLANES, ch), lambda b, j, pt: (b, 0, 0, 0)),
            pl.BlockSpec((1, DSA_KV, PAGE_SIZE), per_b),
            pl.BlockSpec((1, DSA_KV, PAGE_SIZE), per_b),
        ] + _page_specs((1, 1, DSA_KV, PAGE_SIZE), layer, DSA_PAGES)
        + _page_specs((1, 1, DSA_KV, PAGE_SIZE), layer, DSA_PAGES),
        out_specs=pl.BlockSpec((1, rows, DSA_KV), per_b),
        scratch_shapes=[pltpu.VMEM((rows, 1), F32), pltpu.VMEM((rows, 1), F32), pltpu.VMEM((rows, DSA_KV), F32)],
    )
    return pl.pallas_call(
        functools.partial(_dsa_sample_body, dec_seq=dec_seq),
        grid_spec=grid_spec,
        out_shape=jax.ShapeDtypeStruct((db, rows, DSA_KV), F32),
        compiler_params=pltpu.CompilerParams(dimension_semantics=("arbitrary", "arbitrary"), vmem_limit_bytes=VMEM_LIMIT),
        name="dsa_sample",
    )(page_table, q_rows, bias, k_new, v_new, *([k_pool] * DSA_PAGES), *([v_pool] * DSA_PAGES))


_FF_CHUNK = 1024


def _post_body(x_ref, of_ref, od_ref, g1_ref, wg_ref, wof_ref, wod_ref, wout_ref, g2_ref, wup_ref, wdn_ref, gf_ref, y_ref):
    x = x_ref[...]
    d = x.shape[1]
    n = _rmsnorm(x, g1_ref[...]).astype(BF16)
    gate_a = jax.nn.sigmoid(_dot(n, wg_ref[:, 0:d]))
    gate_b = jax.nn.sigmoid(_dot(n, wg_ref[:, d:2 * d]))
    mix = gate_a * _dot(of_ref[...], wof_ref[...]) + gate_b * _dot(od_ref[...], wod_ref[...])
    h = x + _dot(mix.astype(BF16), wout_ref[...])
    n2 = _rmsnorm(h, g2_ref[...]).astype(BF16)
    y = h
    for c in range(wup_ref.shape[1] // _FF_CHUNK):
        u = jnp.maximum(_dot(n2, wup_ref[:, c * _FF_CHUNK:(c + 1) * _FF_CHUNK]), 0.0)
        y = y + _dot((u * u).astype(BF16), wdn_ref[c * _FF_CHUNK:(c + 1) * _FF_CHUNK, :])
    y_ref[...] = _rmsnorm(y, gf_ref[...])


def _post(x2d, o_fox, o_dsa, g1, wg, wof, wod, wout, g2, wup, wdn, gf, *, tm):
    m, d = x2d.shape
    row = lambda i: (i, 0)
    const = lambda i: (0, 0)

    def whole(a):
        return pl.BlockSpec(a.shape, const, pipeline_mode=pl.Buffered(1))

    return pl.pallas_call(
        _post_body,
        grid=(m // tm,),
        in_specs=[pl.BlockSpec((tm, d), row), pl.BlockSpec((tm, FOX_Q), row), pl.BlockSpec((tm, DSA_Q), row),
                  whole(g1), whole(wg), whole(wof), whole(wod), whole(wout), whole(g2), whole(wup), whole(wdn), whole(gf)],
        out_specs=pl.BlockSpec((tm, d), row),
        out_shape=jax.ShapeDtypeStruct((m, d), F32),
        compiler_params=pltpu.CompilerParams(dimension_semantics=("arbitrary",), vmem_limit_bytes=VMEM_LIMIT),
        name="post",
    )(x2d, o_fox, o_dsa, g1, wg, wof, wod, wout, g2, wup, wdn, gf)


def _rope_tables(pos):
    r = HEAD_DIM // ROPE_FRACTION_DIV
    half = r // 2
    inv_freq = jnp.float32(ROPE_THETA) ** (-jnp.arange(half, dtype=F32) * (2.0 / r))
    ang = pos.astype(F32)[:, None] * inv_freq[None, :]
    cos, sin = jnp.cos(ang), jnp.sin(ang)
    p = pos.shape[0]
    one = jnp.ones((p, HEAD_DIM - r), F32)
    zero_r = jnp.zeros((p, HEAD_DIM - r), F32)
    zero_h = jnp.zeros((p, half), F32)
    c = jnp.concatenate([cos, cos, one], axis=1)
    sa = jnp.concatenate([-sin, zero_h, zero_r], axis=1)
    sb = jnp.concatenate([zero_h, sin, zero_r], axis=1)
    return jnp.stack([jnp.tile(t, (1, LANES // HEAD_DIM)) for t in (c, sa, sb)], axis=0)


def _pad_cols(w, width):
    return jnp.pad(w, ((0, 0), (0, width - w.shape[1])))


def _repack_w_in(w_in, d_model):
    sizes = (FOX_Q, FOX_KV, FOX_KV, FOX_HEADS, DSA_Q, DSA_KV, DSA_KV, IDX_Q, IDX_DIM, IDX_HEADS, d_model, d_model)
    cuts = [int(c) for c in np.cumsum(sizes)[:-1]]
    fq, fk, fv, ff, dq, dk, dv, iq, ik, iw, ga, gb = jnp.split(w_in, cuts, axis=-1)
    w1 = jnp.concatenate([fq, fk, fv, dq, dk, dv, iq, _pad_cols(ik, LANES), _pad_cols(ff, LANES), _pad_cols(iw, LANES)], axis=1)
    return w1.astype(BF16), jnp.concatenate([ga, gb], axis=1).astype(BF16)


def _new_page(x, db, dec_seq):
    xt = jnp.swapaxes(x.reshape(db, dec_seq, x.shape[1]), 1, 2)
    return jnp.pad(xt, ((0, 0), (0, 0), (0, PAGE_SIZE - dec_seq)))


def _head_rows(x, db, heads, kv_heads):
    g = heads // kv_heads
    xh = x.reshape(db, -1, kv_heads, g, 1, HEAD_DIM)
    eye = jnp.eye(kv_heads, dtype=x.dtype).reshape(1, 1, kv_heads, 1, kv_heads, 1)
    return (xh * eye).reshape(db, -1, kv_heads * HEAD_DIM)


def _own_features(o, heads, kv_heads):
    db = o.shape[0]
    g = heads // kv_heads
    t = o.shape[1] // heads
    oh = o.reshape(db, t, kv_heads, g, kv_heads, HEAD_DIM)
    idx = jnp.arange(kv_heads)
    return oh[:, :, idx, :, idx, :].transpose(1, 2, 0, 3, 4).reshape(db * t, heads * HEAD_DIM)


def _feature_major(pool):
    lead = pool.shape[:2]
    return jnp.moveaxis(pool.reshape(lead + (PAGE_SIZE, -1)), 2, 3)


def _token_major(xt, tail):
    b, _, t = xt.shape
    return jnp.moveaxis(xt, 1, 2).reshape((1, b, t) + tail)


def kernel(x_prompt, x_sample, cache_fox_k, cache_fox_v, cache_fox_logf, cache_dsa_k, cache_dsa_v, cache_idx_k,
           page_table, norm1_g, w_in, b_forget, w_o_fox, w_o_dsa, w_out, norm2_g, w_up, w_down, final_norm_g):
    batch, seq_len, d = x_prompt.shape
    db, dec_seq, _ = x_sample.shape
    depth = norm1_g.shape[0]
    n_pages = page_table.shape[1]
    past = n_pages * PAGE_SIZE
    assert dec_seq <= SUBLANES and dec_seq & (dec_seq - 1) == 0
    assert n_pages % FOX_PAGES == 0 and n_pages % DSA_PAGES == 0

    tm_p = min(TOKEN_TILE, seq_len)
    tm_s = db * dec_seq
    rope_p = _rope_tables(jnp.arange(seq_len))
    rope_s = _rope_tables(jnp.tile(past + jnp.arange(dec_seq), db))

    fox_k_pool, fox_v_pool, fox_lf_pool, dsa_k_pool, dsa_v_pool, idx_k_pool = (
        _feature_major(c) for c in (cache_fox_k, cache_fox_v, cache_fox_logf, cache_dsa_k, cache_dsa_v, cache_idx_k))

    xp = x_prompt.reshape(batch * seq_len, d)
    xs = x_sample.reshape(db * dec_seq, d)
    rows_p, rows_s = [], []
    for l in range(depth):
        w1, wg = _repack_w_in(w_in[l], d)
        bias = _pad_cols(b_forget[l][None, :], LANES)
        g1 = norm1_g[l][None, :]
        g2 = norm2_g[l][None, :]
        gf = final_norm_g[None, :] if l == depth - 1 else jnp.ones((1, d), F32)
        post_w = (g1, wg, w_o_fox[l].astype(BF16), w_o_dsa[l].astype(BF16), w_out[l].astype(BF16), g2,
                  w_up[l].astype(BF16), w_down[l].astype(BF16), gf)

        (fq, dq, fk, fv, lf, dk, dv, ik, negc, fkt, fvd, dkt, dvq, iqt, iwt, ikb) = _proj(
            xp, g1, w1, bias, rope_p, prompt=True, seq_len=seq_len, tm=tm_p)
        o_fox = _fox_prompt(fq, fkt, fvd, negc, batch=batch, seq_len=seq_len, tq=min(FOX_Q_TILE, seq_len), tk=tm_p)
        o_dsa = _dsa_prompt(iqt, iwt, dq, ikb, dkt, dvq, batch=batch, seq_len=seq_len, tq=min(DSA_Q_TILE, seq_len), ch=tm_p)
        xp = _post(xp, o_fox, o_dsa, *post_w, tm=tm_p)
        rows_p.append((fk, fv, lf, dk, dv, ik))

        (fq, dq, fk, fv, lf, dk, dv, ik, iq, iw) = _proj(xs, g1, w1, bias, rope_s, prompt=False, seq_len=tm_s, tm=tm_s)
        o_fox = _fox_sample(
            page_table, _head_rows(fq, db, FOX_HEADS, FOX_KV_HEADS), _new_page(fk, db, dec_seq), _new_page(fv, db, dec_seq),
            _new_page(lf, db, dec_seq), fox_k_pool, fox_v_pool, fox_lf_pool, layer=l, dec_seq=dec_seq)
        ksel = min(TOPK_MAX, (past + dec_seq) // 4)
        sel_bias = _idx_sample(
            page_table, iq.reshape(db, dec_seq * IDX_HEADS, IDX_DIM), iw.reshape(db, dec_seq * IDX_HEADS, 1),
            _new_page(ik, db, dec_seq), idx_k_pool, layer=l, dec_seq=dec_seq, ksel=ksel)
        o_dsa = _dsa_sample(
            page_table, _head_rows(dq, db, DSA_HEADS, DSA_KV_HEADS), sel_bias, _new_page(dk, db, dec_seq),
            _new_page(dv, db, dec_seq), dsa_k_pool, dsa_v_pool, layer=l, dec_seq=dec_seq)
        xs = _post(xs, _own_features(o_fox, FOX_HEADS, FOX_KV_HEADS).astype(BF16),
                   _own_features(o_dsa, DSA_HEADS, DSA_KV_HEADS).astype(BF16), *post_w, tm=tm_s)
        rows_s.append((fk, fv, lf, dk, dv, ik))

    tails = ((FOX_KV_HEADS, HEAD_DIM), (FOX_KV_HEADS, HEAD_DIM), (FOX_HEADS,), (DSA_KV_HEADS, HEAD_DIM),
             (DSA_KV_HEADS, HEAD_DIM), (IDX_DIM,))
    dtypes = tuple(c.dtype for c in (cache_fox_k, cache_fox_v, cache_fox_logf, cache_dsa_k, cache_dsa_v, cache_idx_k))
    new_p = tuple(jnp.concatenate([_token_major(r[k], tails[k]) for r in rows_p], axis=0).astype(dtypes[k])
                  for k in range(len(tails)))
    new_s = tuple(jnp.stack([r[k].reshape((db, dec_seq) + tails[k]) for r in rows_s], axis=0).astype(dtypes[k])
                  for k in range(len(tails)))
    return (xp.reshape(batch, seq_len, d), xs.reshape(db, dec_seq, d)) + new_p + new_s
```

```python
import functools

import jax
import jax.numpy as jnp
import numpy as np
from jax import lax
from jax.experimental import pallas as pl
from jax.experimental.pallas import tpu as pltpu

HEAD_DIM = 64
FOX_HEADS = 8
FOX_KV_HEADS = 4
DSA_HEADS = 8
DSA_KV_HEADS = 2
IDX_HEADS = 8
IDX_DIM = 64
TOPK_MAX = 256
ROPE_THETA = 500000.0
ROPE_FRACTION_DIV = 4
NORM_EPS = 1e-6
PAGE_SIZE = 128

FOX_Q = FOX_HEADS * HEAD_DIM
FOX_KV = FOX_KV_HEADS * HEAD_DIM
DSA_Q = DSA_HEADS * HEAD_DIM
DSA_KV = DSA_KV_HEADS * HEAD_DIM
IDX_Q = IDX_HEADS * IDX_DIM
FOX_GROUPS = FOX_HEADS // FOX_KV_HEADS
DSA_GROUPS = DSA_HEADS // DSA_KV_HEADS

LANES = 128
SUBLANES = 8
NEG = -(2.0 ** 100)
QK_SCALE = HEAD_DIM ** -0.5
VMEM_LIMIT = 56 * 1024 * 1024

TOKEN_TILE = 512
FOX_Q_TILE = 256
DSA_Q_TILE = 128
FOX_PAGES = 16
DSA_PAGES = 16
BISECT_PASSES = 16

F32 = jnp.float32
BF16 = jnp.bfloat16

_C_FOX = 0
_C_DSA = _C_FOX + FOX_Q + 2 * FOX_KV
_C_IQ = _C_DSA + DSA_Q + 2 * DSA_KV
_C_IK = _C_IQ + IDX_Q
_C_FF = _C_IK + LANES
_C_IW = _C_FF + LANES
_C_END = _C_IW + LANES


def _dot(a, b):
    return jnp.dot(a, b, preferred_element_type=F32)


def _dot_nt(a, b):
    return lax.dot_general(a, b, (((1,), (1,)), ((), ())), preferred_element_type=F32)


def _split3(x):
    hi = x.astype(BF16)
    r1 = x - hi.astype(F32)
    mid = r1.astype(BF16)
    lo = (r1 - mid.astype(F32)).astype(BF16)
    return hi, mid, lo


def _lane_iota(shape):
    return lax.broadcasted_iota(jnp.int32, shape, len(shape) - 1)


def _row_iota(shape):
    return lax.broadcasted_iota(jnp.int32, shape, len(shape) - 2)


def _rope(x, cos, sa, sb):
    w = x.shape[1]
    reps = w // LANES
    if reps > 1:
        cos = jnp.concatenate([cos] * reps, axis=1)
        sa = jnp.concatenate([sa] * reps, axis=1)
        sb = jnp.concatenate([sb] * reps, axis=1)
    up = pltpu.roll(x, w - 8, 1)
    dn = pltpu.roll(x, 8, 1)
    return x * cos + up * sa + dn * sb


def _rmsnorm(x, g):
    ms = jnp.mean(x * x, axis=-1, keepdims=True)
    return x * lax.rsqrt(ms + NORM_EPS) * g


def _proj_body(x_ref, g_ref, w_ref, b_ref, rope_ref, *refs, prompt, blocks_per_seq):
    fq_ref, dq_ref = refs[:2]
    tm = x_ref.shape[0]
    n = _rmsnorm(x_ref[...], g_ref[...]).astype(BF16)
    cos, sa, sb = rope_ref[0], rope_ref[1], rope_ref[2]

    def mm(lo, hi):
        return _dot(n, w_ref[:, lo:hi])

    fq = mm(_C_FOX, _C_FOX + FOX_Q)
    fq_ref[...] = (fq * QK_SCALE).astype(BF16)
    fk = mm(_C_FOX + FOX_Q, _C_FOX + FOX_Q + FOX_KV)
    fv = mm(_C_FOX + FOX_Q + FOX_KV, _C_DSA)
    dq = _rope(mm(_C_DSA, _C_DSA + DSA_Q), cos, sa, sb)
    dq_ref[...] = (dq * QK_SCALE).astype(BF16)
    dk = _rope(mm(_C_DSA + DSA_Q, _C_DSA + DSA_Q + DSA_KV), cos, sa, sb)
    dv = mm(_C_DSA + DSA_Q + DSA_KV, _C_IQ)
    iq = _rope(mm(_C_IQ, _C_IK), cos, sa, sb) * (IDX_DIM ** -0.5)
    ikg = _rope(mm(_C_IK, _C_FF), cos, sa, sb)
    ff = mm(_C_FF, _C_IW) + b_ref[...]
    lf = jnp.minimum(ff, 0.0) - jnp.log(1.0 + jnp.exp(-jnp.abs(ff)))
    iw = mm(_C_IW, _C_END) * (IDX_HEADS ** -0.5)

    if not prompt:
        fk_ref, fv_ref, lf_ref, dk_ref, dv_ref, ik_ref, iq_ref, iw_ref = refs[2:]
        fk_ref[...] = fk
        fv_ref[...] = fv
        lf_ref[...] = lf[:, :FOX_HEADS]
        dk_ref[...] = dk
        dv_ref[...] = dv
        ik_ref[...] = ikg[:, :IDX_DIM]
        iq_ref[...] = iq.astype(BF16)
        iw_ref[...] = iw[:, :IDX_HEADS]
        return
    (fkx_ref, fvx_ref, lfx_ref, dkx_ref, dvx_ref, ikx_ref,
     negc_ref, fkt_ref, fvd_ref, dkt_ref, dvd_ref, iqt_ref, iwt_ref, ikb_ref, carry_ref) = refs[2:]
    low = _lane_iota((tm, LANES)) < HEAD_DIM

    fk_t, dk_t, ik_t = fk.T, dk.T, ikg.T
    fkx_ref[0] = fk_t
    fvx_ref[0] = fv.T
    lfx_ref[0] = lf.T[:FOX_HEADS, :]
    dkx_ref[0] = dk_t
    dvx_ref[0] = dv.T
    ikx_ref[0] = ik_t[:IDX_DIM, :]

    @pl.when(pl.program_id(0) % blocks_per_seq == 0)
    def _():
        carry_ref[...] = jnp.zeros_like(carry_ref)

    tri = (_row_iota((tm, tm)) >= _lane_iota((tm, tm))).astype(BF16)
    hi, mid, lo = _split3(lf)
    csum = _dot(tri, hi) + _dot(tri, mid) + _dot(tri, lo) + carry_ref[0:1, :]
    carry_ref[0:1, :] = csum[tm - 1:tm, :]
    negc_ref[0] = -(csum.T[:FOX_HEADS, :])

    fkt = fk_t.astype(BF16)
    for h in range(FOX_KV_HEADS):
        kt = fkt[h * HEAD_DIM:(h + 1) * HEAD_DIM, :]
        fkt_ref[h, 0] = jnp.concatenate([kt, kt], axis=0)
    for p in range(FOX_KV_HEADS // 2):
        pair = fv[:, p * LANES:(p + 1) * LANES]
        swap = pltpu.roll(pair, HEAD_DIM, 1)
        fvd_ref[2 * p] = jnp.where(low, pair, swap).astype(BF16)
        fvd_ref[2 * p + 1] = jnp.where(low, swap, pair).astype(BF16)

    dkt = dk_t.astype(BF16)
    swap = pltpu.roll(dv, HEAD_DIM, 1)
    dv_dup = (jnp.where(low, dv, swap).astype(BF16), jnp.where(low, swap, dv).astype(BF16))
    for h in range(DSA_KV_HEADS):
        kt = dkt[h * HEAD_DIM:(h + 1) * HEAD_DIM, :]
        dkt_ref[h, 0] = jnp.concatenate([kt, kt, kt, kt], axis=0)
        dvd_ref[h] = dv_dup[h]

    iqt_ref[...] = iq.T.astype(BF16)
    iwt_ref[...] = iw.T[:IDX_HEADS, :]
    ikb_ref[0] = ikg[:, :IDX_DIM].astype(BF16)


def _proj(x2d, g1, w1, bias, rope_tab, *, prompt, seq_len, tm):
    m, d = x2d.shape
    nblk = m // tm
    npos = rope_tab.shape[1] // tm
    row = lambda i: (i, 0)
    col = lambda i: (0, i)
    outs = [
        (jax.ShapeDtypeStruct((m, FOX_Q), BF16), pl.BlockSpec((tm, FOX_Q), row)),
        (jax.ShapeDtypeStruct((m, DSA_Q), BF16), pl.BlockSpec((tm, DSA_Q), row)),
    ]
    cache_widths = (FOX_KV, FOX_KV, FOX_HEADS, DSA_KV, DSA_KV, IDX_DIM)
    scratch = []
    if prompt:
        bps = seq_len // tm
        feat_major = lambda i: (i // bps, 0, i % bps)
        outs += [(jax.ShapeDtypeStruct((m // seq_len, w, seq_len), F32), pl.BlockSpec((1, w, tm), feat_major))
                 for w in cache_widths]
        outs += [
            (jax.ShapeDtypeStruct((nblk, FOX_HEADS, tm), F32), pl.BlockSpec((1, FOX_HEADS, tm), lambda i: (i, 0, 0))),
            (jax.ShapeDtypeStruct((FOX_KV_HEADS, nblk, 2 * HEAD_DIM, tm), BF16),
             pl.BlockSpec((FOX_KV_HEADS, 1, 2 * HEAD_DIM, tm), lambda i: (0, i, 0, 0))),
            (jax.ShapeDtypeStruct((FOX_KV_HEADS, m, 2 * HEAD_DIM), BF16),
             pl.BlockSpec((FOX_KV_HEADS, tm, 2 * HEAD_DIM), lambda i: (0, i, 0))),
            (jax.ShapeDtypeStruct((DSA_KV_HEADS, nblk, 4 * HEAD_DIM, tm), BF16),
             pl.BlockSpec((DSA_KV_HEADS, 1, 4 * HEAD_DIM, tm), lambda i: (0, i, 0, 0))),
            (jax.ShapeDtypeStruct((DSA_KV_HEADS, m, 2 * HEAD_DIM), BF16),
             pl.BlockSpec((DSA_KV_HEADS, tm, 2 * HEAD_DIM), lambda i: (0, i, 0))),
            (jax.ShapeDtypeStruct((IDX_Q, m), BF16), pl.BlockSpec((IDX_Q, tm), col)),
            (jax.ShapeDtypeStruct((IDX_HEADS, m), F32), pl.BlockSpec((IDX_HEADS, tm), col)),
            (jax.ShapeDtypeStruct((nblk, tm, IDX_DIM), BF16), pl.BlockSpec((1, tm, IDX_DIM), lambda i: (i, 0, 0))),
        ]
        scratch = [pltpu.VMEM((SUBLANES, LANES), F32)]
    else:
        outs += [(jax.ShapeDtypeStruct((m, w), F32), pl.BlockSpec((tm, w), row)) for w in cache_widths]
        outs += [
            (jax.ShapeDtypeStruct((m, IDX_Q), BF16), pl.BlockSpec((tm, IDX_Q), row)),
            (jax.ShapeDtypeStruct((m, IDX_HEADS), F32), pl.BlockSpec((tm, IDX_HEADS), row)),
        ]
    const = lambda i: (0, 0)
    return pl.pallas_call(
        functools.partial(_proj_body, prompt=prompt, blocks_per_seq=seq_len // tm),
        grid=(nblk,),
        in_specs=[
            pl.BlockSpec((tm, d), row),
            pl.BlockSpec((1, d), const),
            pl.BlockSpec((d, _C_END), const, pipeline_mode=pl.Buffered(1)),
            pl.BlockSpec((1, LANES), const),
            pl.BlockSpec((3, tm, LANES), lambda i: (0, i % npos, 0)),
        ],
        out_specs=[o[1] for o in outs],
        out_shape=[o[0] for o in outs],
        scratch_shapes=scratch,
        compiler_params=pltpu.CompilerParams(dimension_semantics=("arbitrary",), vmem_limit_bytes=VMEM_LIMIT),
        name="proj_prompt" if prompt else "proj_sample",
    )(x2d, g1, w1, bias, rope_tab)


def _softmax_step(s, m, l, acc, v):
    m_new = jnp.maximum(m, jnp.max(s, axis=-1, keepdims=True))
    alpha = jnp.exp(m - m_new)
    p = jnp.exp(s - m_new)
    l = alpha * l + jnp.sum(p, axis=-1, keepdims=True)
    acc = alpha * acc + _dot(p.astype(BF16), v)
    return m_new, l, acc


def _softmax_init(rows, width):
    return jnp.full((rows, 1), NEG, F32), jnp.zeros((rows, 1), F32), jnp.zeros((rows, width), F32)


def _pick_head_lanes(acc, tq, groups):
    lane = _lane_iota((tq, groups * HEAD_DIM))
    out = acc[0:tq]
    for g in range(1, groups):
        out = jnp.where(lane >= g * HEAD_DIM, acc[g * tq:(g + 1) * tq], out)
    return out


def _stack_heads(q, tq, groups):
    lane = _lane_iota((tq, groups * HEAD_DIM))
    zero = jnp.zeros_like(q)
    return jnp.concatenate(
        [jnp.where((lane >= g * HEAD_DIM) & (lane < (g + 1) * HEAD_DIM), q, zero) for g in range(groups)], axis=0)


def _fox_prompt_body(q_ref, kt_ref, v_ref, negc_ref, o_ref, *, tq, tk):
    i = pl.program_id(1)
    nfull = (i * tq) // tk
    g = FOX_GROUPS
    w = g * HEAD_DIM
    qpos = i * tq + (_row_iota((g * tq, tk)) & (tq - 1))
    kloc = _lane_iota((g * tq, tk))
    qs = [_stack_heads(q_ref[:, h * w:(h + 1) * w], tq, g) for h in range(FOX_KV_HEADS)]

    def step(j, carry, masked):
        nb = negc_ref[j]
        off = pl.multiple_of(j * tk, tk)
        out = []
        for h in range(FOX_KV_HEADS):
            bias = jnp.concatenate(
                [jnp.broadcast_to(nb[h * g + k:h * g + k + 1, :], (tq, tk)) for k in range(g)], axis=0)
            s = _dot(qs[h], kt_ref[h, j]) + bias
            if masked:
                s = jnp.where(j * tk + kloc <= qpos, s, NEG)
            out.append(_softmax_step(s, *carry[h], v_ref[h, pl.ds(off, tk), :]))
        return tuple(out)

    init = tuple(_softmax_init(g * tq, w) for _ in range(FOX_KV_HEADS))
    carry = lax.fori_loop(0, nfull, lambda j, c: step(j, c, False), init)
    carry = step(nfull, carry, True)
    for h in range(FOX_KV_HEADS):
        m, l, acc = carry[h]
        o_ref[:, h * w:(h + 1) * w] = _pick_head_lanes(acc / l, tq, g).astype(BF16)


def _fox_prompt(fq, fkt, fvd, negc, *, batch, seq_len, tq, tk):
    m = fq.shape[0]
    nq = seq_len // tq
    nkb = seq_len // tk
    return pl.pallas_call(
        functools.partial(_fox_prompt_body, tq=tq, tk=tk),
        grid=(batch, nq),
        in_specs=[
            pl.BlockSpec((tq, FOX_Q), lambda b, i: (b * nq + i, 0)),
            pl.BlockSpec((FOX_KV_HEADS, nkb, 2 * HEAD_DIM, tk), lambda b, i: (0, b, 0, 0), pipeline_mode=pl.Buffered(1)),
            pl.BlockSpec((FOX_KV_HEADS, seq_len, 2 * HEAD_DIM), lambda b, i: (0, b, 0), pipeline_mode=pl.Buffered(1)),
            pl.BlockSpec((nkb, FOX_HEADS, tk), lambda b, i: (b, 0, 0), pipeline_mode=pl.Buffered(1)),
        ],
        out_specs=pl.BlockSpec((tq, FOX_Q), lambda b, i: (b * nq + i, 0)),
        out_shape=jax.ShapeDtypeStruct((m, FOX_Q), BF16),
        compiler_params=pltpu.CompilerParams(dimension_semantics=("arbitrary", "arbitrary"), vmem_limit_bytes=VMEM_LIMIT),
        name="fox_prompt",
    )(fq, fkt, fvd, negc)


def _topk_threshold(reduce_fn, idx_of, nvalid, ksel, n_idx):
    kf = jnp.float32(ksel)
    inf = jnp.float32(jnp.inf)
    one, zero = jnp.float32(1.0), jnp.float32(0.0)
    shape = nvalid.shape

    def count_ge(t):
        return reduce_fn(lambda c, blk: jnp.where(blk >= t, one, zero), "sum")

    row_max = reduce_fn(lambda c, blk: blk, "max")
    row_min = reduce_fn(lambda c, blk: jnp.where(blk > -inf, blk, inf), "min")
    all_sel = jnp.where(nvalid <= kf, one, zero)

    def bisect(_, st):
        lo, clo, hi, chi, done = st
        t = jnp.where(hi == inf, row_max, 0.5 * (lo + hi))
        cnt = count_ge(t)
        upd = (done < 0.5) & (t > lo) & (t < hi)
        ge = cnt >= kf
        lo = jnp.where(upd & ge, t, lo)
        clo = jnp.where(upd & ge, cnt, clo)
        hi = jnp.where(upd & (cnt < kf), t, hi)
        chi = jnp.where(upd & (cnt < kf), cnt, chi)
        return lo, clo, hi, chi, jnp.where(clo == kf, one, done)

    st = (row_min, nvalid, jnp.full(shape, inf, F32), jnp.zeros(shape, F32), all_sel)
    st = lax.fori_loop(0, BISECT_PASSES, bisect, st)

    def snap(st):
        lo, clo, hi, chi, done = st
        v = reduce_fn(lambda c, blk: jnp.where(blk < hi, blk, -inf), "max")
        cv = count_ge(v)
        fin = (done < 0.5) & (cv >= kf)
        go = (done < 0.5) & (cv < kf)
        lo = jnp.where(fin, v, lo)
        clo = jnp.where(fin, cv, clo)
        hi = jnp.where(go, v, hi)
        chi = jnp.where(go, cv, chi)
        return lo, clo, hi, chi, jnp.where(fin, one, done)

    lo, clo, hi, chi, done = lax.while_loop(lambda st: jnp.min(st[4]) < 0.5, snap, st)

    thr = jnp.where(all_sel > 0.5, -inf, lo)
    need_cut = jnp.where((all_sel < 0.5) & (clo > kf), one, zero)
    need = kf - chi
    big = jnp.float32(n_idx)

    def cut_search():
        def body(_, st):
            jlo, jhi = st
            mid = jnp.floor(0.5 * (jlo + jhi))
            cnt = reduce_fn(lambda c, blk: jnp.where((blk == thr) & (idx_of(c) <= mid), one, zero), "sum")
            ok = cnt >= need
            return jnp.where(ok, jlo, mid), jnp.where(ok, mid, jhi)

        npass = int(np.ceil(np.log2(n_idx))) + 1
        _, jhi = lax.fori_loop(0, npass, body, (jnp.full(shape, -1.0, F32), jnp.full(shape, n_idx - 1.0, F32)))
        return jhi

    jcut = lax.cond(jnp.max(need_cut) > 0.5, cut_search, lambda: jnp.full(shape, -1.0, F32))
    jcut = jnp.where(need_cut > 0.5, jcut, jnp.where(all_sel > 0.5, -1.0, big))
    return thr, jcut


_REDUCE_OPS = {"sum": (jnp.add, 0.0, jnp.sum), "max": (jnp.maximum, -np.inf, jnp.max), "min": (jnp.minimum, np.inf, jnp.min)}


def _tree(parts, op):
    while len(parts) > 1:
        parts = [op(parts[k], parts[k + 1]) if k + 1 < len(parts) else parts[k] for k in range(0, len(parts), 2)]
    return parts[0]


def _row_major_reducer(s_ref, nch):
    _, rows, ch = s_ref.shape

    def reduce_fn(elem_fn, kind):
        op, init, lane_reduce = _REDUCE_OPS[kind]

        def body(c, acc):
            x = elem_fn(c, s_ref[c])
            return op(acc, _tree([x[:, k * LANES:(k + 1) * LANES] for k in range(ch // LANES)], op))

        acc = jnp.full((rows, LANES), init, F32)
        if isinstance(nch, int):
            for c in range(nch):
                acc = body(c, acc)
        else:
            acc = lax.fori_loop(0, nch, body, acc)
        return lane_reduce(acc, axis=1, keepdims=True)

    return reduce_fn


def _lane_major_reducer(s_ref, nch):
    _, ch, _ = s_ref.shape

    def reduce_fn(elem_fn, kind):
        op, init, sub_reduce = _REDUCE_OPS[kind]

        def body(c, acc):
            x = elem_fn(c, s_ref[c])
            return op(acc, _tree([x[k * SUBLANES:(k + 1) * SUBLANES, :] for k in range(ch // SUBLANES)], op))

        acc = lax.fori_loop(0, nch, body, jnp.full((SUBLANES, LANES), init, F32))
        return sub_reduce(acc, axis=0, keepdims=True)

    return reduce_fn


def _dsa_prompt_body(iqt_ref, iwt_ref, dq_ref, ikb_ref, dkt_ref, dv_ref, o_ref, s_ref, bias_ref, *, tq, ch, ksel):
    i = pl.program_id(1)
    nch = (i * tq + tq - 1) // ch + 1
    g = DSA_GROUPS
    w = g * HEAD_DIM

    iqt = iqt_ref[...]
    rhs = [jnp.concatenate([iqt[(2 * p) * IDX_DIM:(2 * p + 1) * IDX_DIM, :], iqt[(2 * p + 1) * IDX_DIM:(2 * p + 2) * IDX_DIM, :]],
                           axis=1) for p in range(IDX_HEADS // 2)]
    iwt = iwt_ref[...]
    qpos = i * tq + _lane_iota((ch, tq))
    kloc = _row_iota((ch, tq))

    def score_chunk(c, _):
        keys = ikb_ref[c]
        acc = None
        for p in range(IDX_HEADS // 2):
            d = jnp.maximum(_dot(keys, rhs[p]), 0.0)
            term = d[:, :tq] * iwt[2 * p:2 * p + 1, :] + d[:, tq:] * iwt[2 * p + 1:2 * p + 2, :]
            acc = term if acc is None else acc + term
        s_ref[c] = jnp.where(c * ch + kloc <= qpos, acc, -jnp.inf)
        return 0

    lax.fori_loop(0, nch, score_chunk, 0)

    nvalid = (i * tq + _lane_iota((1, tq)) + 1).astype(F32)
    idx_of = lambda c: (c * ch + kloc).astype(F32)
    thr, jcut = _topk_threshold(_lane_major_reducer(s_ref, nch), idx_of, nvalid, ksel, s_ref.shape[0] * ch)

    def write_bias(c, _):
        blk = s_ref[c]
        sel = (blk > thr) | ((blk == thr) & (idx_of(c) <= jcut))
        bias_ref[c] = jnp.where(sel, 0.0, NEG).T
        return 0

    lax.fori_loop(0, nch, write_bias, 0)

    qs = [_stack_heads(dq_ref[:, h * w:(h + 1) * w], tq, g) for h in range(DSA_KV_HEADS)]

    def step(c, carry):
        b = bias_ref[c]
        bias = jnp.concatenate([b] * g, axis=0)
        off = pl.multiple_of(c * ch, ch)
        return tuple(_softmax_step(_dot(qs[h], dkt_ref[h, c]) + bias, *carry[h], dv_ref[h, pl.ds(off, ch), :])
                     for h in range(DSA_KV_HEADS))

    carry = lax.fori_loop(0, nch, step, tuple(_softmax_init(g * tq, 2 * HEAD_DIM) for _ in range(DSA_KV_HEADS)))
    low = _lane_iota((tq, 2 * HEAD_DIM)) < HEAD_DIM
    for h in range(DSA_KV_HEADS):
        m, l, acc = carry[h]
        o = acc / l
        pairs = [jnp.where(low, o[(2 * p) * tq:(2 * p + 1) * tq], o[(2 * p + 1) * tq:(2 * p + 2) * tq]) for p in range(g // 2)]
        o_ref[:, h * w:(h + 1) * w] = jnp.concatenate(pairs, axis=1).astype(BF16)


def _dsa_prompt(iqt, iwt, dq, ikb, dkt, dvd, *, batch, seq_len, tq, ch):
    m = dq.shape[0]
    nq = seq_len // tq
    nkb = seq_len // ch
    ksel = min(TOPK_MAX, seq_len // 4)
    row = lambda b, i: (b * nq + i, 0)
    col = lambda b, i: (0, b * nq + i)
    return pl.pallas_call(
        functools.partial(_dsa_prompt_body, tq=tq, ch=ch, ksel=ksel),
        grid=(batch, nq),
        in_specs=[
            pl.BlockSpec((IDX_Q, tq), col),
            pl.BlockSpec((IDX_HEADS, tq), col),
            pl.BlockSpec((tq, DSA_Q), row),
            pl.BlockSpec((nkb, ch, IDX_DIM), lambda b, i: (b, 0, 0), pipeline_mode=pl.Buffered(1)),
            pl.BlockSpec((DSA_KV_HEADS, nkb, 4 * HEAD_DIM, ch), lambda b, i: (0, b, 0, 0), pipeline_mode=pl.Buffered(1)),
            pl.BlockSpec((DSA_KV_HEADS, seq_len, 2 * HEAD_DIM), lambda b, i: (0, b, 0), pipeline_mode=pl.Buffered(1)),
        ],
        out_specs=pl.BlockSpec((tq, DSA_Q), row),
        out_shape=jax.ShapeDtypeStruct((m, DSA_Q), BF16),
        scratch_shapes=[pltpu.VMEM((nkb, ch, tq), F32), pltpu.VMEM((nkb, tq, ch), F32)],
        compiler_params=pltpu.CompilerParams(dimension_semantics=("arbitrary", "arbitrary"), vmem_limit_bytes=VMEM_LIMIT),
        name="dsa_prompt",
    )(iqt, iwt, dq, ikb, dkt, dvd)


def _page_ring(pt_ref, pools, bufs, sem, layer, pages):
    ng = pl.num_programs(1)
    step = pl.program_id(0) * ng + pl.program_id(1)
    last = pl.num_programs(0) * ng - 1

    def copies(s):
        b, j, slot = lax.div(s, ng), lax.rem(s, ng), lax.rem(s, 2)
        return [pltpu.make_async_copy(pool.at[layer, pt_ref[b, j * pages + k]], buf.at[slot, k], sem.at[slot, a])
                for a, (pool, buf) in enumerate(zip(pools, bufs)) for k in range(pages)]

    @pl.when(step == 0)
    def _():
        for c in copies(step):
            c.start()

    @pl.when(step < last)
    def _():
        for c in copies(step + 1):
            c.start()

    for c in copies(step):
        c.wait()
    return lax.rem(step, 2)


def _page_ring_scratch(features, pages):
    return [pltpu.VMEM((2, pages, f, PAGE_SIZE), F32) for f in features] + [pltpu.SemaphoreType.DMA((2, len(features)))]


def _page_tiles(buf, slot):
    return jnp.concatenate([buf[slot, k] for k in range(buf.shape[1])], axis=1).astype(BF16)


def _sample_softmax_step(s, vt, m_ref, l_ref, acc_ref):
    m = m_ref[...]
    m_new = jnp.maximum(m, jnp.max(s, axis=-1, keepdims=True))
    alpha = jnp.exp(m - m_new)
    p = jnp.exp(s - m_new)
    m_ref[...] = m_new
    l_ref[...] = alpha * l_ref[...] + jnp.sum(p, axis=-1, keepdims=True)
    acc_ref[...] = alpha * acc_ref[...] + _dot_nt(p.astype(BF16), vt)


def _fox_sample_body(pt_ref, q_ref, kn_ref, vn_ref, lfn_ref, k_hbm, v_hbm, lf_hbm, o_ref, m_ref, l_ref, acc_ref, pre_ref,
                     kbuf, vbuf, lbuf, sem, *, layer, dec_seq):
    np_ = FOX_PAGES
    j = pl.program_id(1)
    rows = dec_seq * FOX_HEADS
    slot = _page_ring(pt_ref, (k_hbm, v_hbm, lf_hbm), (kbuf, vbuf, lbuf), sem, layer, np_)

    @pl.when(j == 0)
    def _():
        m_ref[...] = jnp.full_like(m_ref, NEG)
        l_ref[...] = jnp.zeros_like(l_ref)
        acc_ref[...] = jnp.zeros_like(acc_ref)
        pre_ref[...] = jnp.zeros_like(pre_ref)

    triu = (_row_iota((PAGE_SIZE, PAGE_SIZE)) <= _lane_iota((PAGE_SIZE, PAGE_SIZE))).astype(BF16)

    def cum_keys(lf):
        r = lf.shape[0]
        parts = _dot(jnp.concatenate(_split3(lf), axis=0), triu)
        return parts[0:r] + parts[r:2 * r] + parts[2 * r:3 * r]

    q = q_ref[0]
    nrow = np_ * FOX_HEADS
    within = cum_keys(jnp.concatenate([lbuf[slot, k] for k in range(np_)], axis=0))
    r, c = _row_iota((nrow, nrow)), _lane_iota((nrow, nrow))
    earlier = (((r ^ c) & (FOX_HEADS - 1)) == 0) & (c < r)
    totals = jnp.broadcast_to(within[:, PAGE_SIZE - 1:PAGE_SIZE], (nrow, PAGE_SIZE))
    carried = _dot(earlier.astype(BF16), jnp.concatenate(_split3(totals), axis=1))
    carried = carried[:, 0:PAGE_SIZE] + carried[:, PAGE_SIZE:2 * PAGE_SIZE] + carried[:, 2 * PAGE_SIZE:3 * PAGE_SIZE]
    cum = within + carried + jnp.concatenate([pre_ref[...]] * np_, axis=0)
    pre_ref[...] = jnp.broadcast_to(cum[nrow - FOX_HEADS:nrow, PAGE_SIZE - 1:PAGE_SIZE], pre_ref.shape)
    c_all = jnp.concatenate([cum[k * FOX_HEADS:(k + 1) * FOX_HEADS, :] for k in range(np_)], axis=1)
    s = _dot(q, _page_tiles(kbuf, slot)) - jnp.concatenate([c_all] * dec_seq, axis=0)
    _sample_softmax_step(s, _page_tiles(vbuf, slot), m_ref, l_ref, acc_ref)

    @pl.when(j == pl.num_programs(1) - 1)
    def _():
        c = cum_keys(lfn_ref[0]) + pre_ref[...]
        s = _dot(q, kn_ref[0].astype(BF16)) - jnp.concatenate([c] * dec_seq, axis=0)
        tok = _row_iota((rows, PAGE_SIZE)) >> (FOX_HEADS.bit_length() - 1)
        s = jnp.where(_lane_iota((rows, PAGE_SIZE)) <= tok, s, NEG)
        _sample_softmax_step(s, vn_ref[0].astype(BF16), m_ref, l_ref, acc_ref)
        o_ref[0] = acc_ref[...] / l_ref[...]


def _fox_sample(page_table, q_rows, k_new, v_new, lf_new, k_pool, v_pool, lf_pool, *, layer, dec_seq):
    db, n_pages = page_table.shape
    ng = n_pages // FOX_PAGES
    rows = dec_seq * FOX_HEADS
    per_b = lambda b, j, pt: (b, 0, 0)
    grid_spec = pltpu.PrefetchScalarGridSpec(
        num_scalar_prefetch=1,
        grid=(db, ng),
        in_specs=[
            pl.BlockSpec((1, rows, FOX_KV), per_b),
            pl.BlockSpec((1, FOX_KV, PAGE_SIZE), per_b),
            pl.BlockSpec((1, FOX_KV, PAGE_SIZE), per_b),
            pl.BlockSpec((1, FOX_HEADS, PAGE_SIZE), per_b),
        ] + [pl.BlockSpec(memory_space=pl.ANY)] * 3,
        out_specs=pl.BlockSpec((1, rows, FOX_KV), per_b),
        scratch_shapes=[pltpu.VMEM((rows, 1), F32), pltpu.VMEM((rows, 1), F32), pltpu.VMEM((rows, FOX_KV), F32),
                        pltpu.VMEM((FOX_HEADS, PAGE_SIZE), F32)]
        + _page_ring_scratch((FOX_KV, FOX_KV, FOX_HEADS), FOX_PAGES),
    )
    return pl.pallas_call(
        functools.partial(_fox_sample_body, layer=layer, dec_seq=dec_seq),
        grid_spec=grid_spec,
        out_shape=jax.ShapeDtypeStruct((db, rows, FOX_KV), F32),
        compiler_params=pltpu.CompilerParams(dimension_semantics=("arbitrary", "arbitrary"), vmem_limit_bytes=VMEM_LIMIT),
        name="fox_sample",
    )(page_table, q_rows, k_new, v_new, lf_new, k_pool, v_pool, lf_pool)


def _idx_sample_body(pt_ref, q_ref, w_ref, kn_ref, k_hbm, bias_ref, s_ref, kbuf, sem, *, layer, dec_seq, past, ksel):
    np_ = DSA_PAGES
    j = pl.program_id(1)
    slot = _page_ring(pt_ref, (k_hbm,), (kbuf,), sem, layer, np_)
    ng = pl.num_programs(1)
    ch = np_ * PAGE_SIZE
    q = q_ref[0]
    w = w_ref[0]

    def scores(keys_t):
        d = jnp.maximum(_dot(q, keys_t), 0.0) * w
        shape = (SUBLANES, d.shape[1])
        tok = _row_iota(shape)
        out = jnp.full(shape, -jnp.inf, F32)
        for t in range(dec_seq):
            per_tok = jnp.sum(d[t * IDX_HEADS:(t + 1) * IDX_HEADS], axis=0, keepdims=True)
            out = jnp.where(tok == t, jnp.broadcast_to(per_tok, shape), out)
        return out

    s_ref[j] = scores(_page_tiles(kbuf, slot))

    @pl.when(j == ng - 1)
    def _():
        sn = scores(kn_ref[0].astype(BF16))
        sn = jnp.where(_lane_iota((SUBLANES, PAGE_SIZE)) <= _row_iota((SUBLANES, PAGE_SIZE)), sn, -jnp.inf)
        s_ref[ng] = jnp.concatenate([sn, jnp.full((SUBLANES, ch - PAGE_SIZE), -jnp.inf, F32)], axis=1)
        tok = _row_iota((SUBLANES, 1))
        nvalid = jnp.where(tok < dec_seq, (past + 1 + tok).astype(F32), 0.0)
        idx_of = lambda c: (c * ch + _lane_iota((SUBLANES, ch))).astype(F32)
        thr, jcut = _topk_threshold(_row_major_reducer(s_ref, ng + 1), idx_of, nvalid, ksel, s_ref.shape[0] * ch)
        for c in range(s_ref.shape[0]):
            blk = s_ref[c]
            sel = (blk > thr) | ((blk == thr) & (idx_of(c) <= jcut))
            bias_ref[0, c] = jnp.where(sel, 0.0, NEG)


def _idx_sample(page_table, iq_rows, iw_rows, ik_new, ik_pool, *, layer, dec_seq, ksel):
    db, n_pages = page_table.shape
    ng = n_pages // DSA_PAGES
    ch = DSA_PAGES * PAGE_SIZE
    rows = dec_seq * IDX_HEADS
    per_b = lambda b, j, pt: (b, 0, 0)
    grid_spec = pltpu.PrefetchScalarGridSpec(
        num_scalar_prefetch=1,
        grid=(db, ng),
        in_specs=[
            pl.BlockSpec((1, rows, IDX_DIM), per_b),
            pl.BlockSpec((1, rows, 1), per_b),
            pl.BlockSpec((1, IDX_DIM, PAGE_SIZE), per_b),
            pl.BlockSpec(memory_space=pl.ANY),
        ],
        out_specs=pl.BlockSpec((1, ng + 1, SUBLANES, ch), lambda b, j, pt: (b, 0, 0, 0)),
        scratch_shapes=[pltpu.VMEM((ng + 1, SUBLANES, ch), F32)] + _page_ring_scratch((IDX_DIM,), DSA_PAGES),
    )
    return pl.pallas_call(
        functools.partial(_idx_sample_body, layer=layer, dec_seq=dec_seq, past=n_pages * PAGE_SIZE, ksel=ksel),
        grid_spec=grid_spec,
        out_shape=jax.ShapeDtypeStruct((db, ng + 1, SUBLANES, ch), F32),
        compiler_params=pltpu.CompilerParams(dimension_semantics=("arbitrary", "arbitrary"), vmem_limit_bytes=VMEM_LIMIT),
        name="idx_sample",
    )(page_table, iq_rows, iw_rows, ik_new, ik_pool)


def _dsa_sample_body(pt_ref, q_ref, bias_ref, kn_ref, vn_ref, k_hbm, v_hbm, o_ref, m_ref, l_ref, acc_ref, kbuf, vbuf, sem,
                     *, layer, dec_seq):
    np_ = DSA_PAGES
    j = pl.program_id(1)
    ng = pl.num_programs(1)
    slot = _page_ring(pt_ref, (k_hbm, v_hbm), (kbuf, vbuf), sem, layer, np_)
    rows = dec_seq * DSA_HEADS

    @pl.when(j == 0)
    def _():
        m_ref[...] = jnp.full_like(m_ref, NEG)
        l_ref[...] = jnp.zeros_like(l_ref)
        acc_ref[...] = jnp.zeros_like(acc_ref)

    def mask_rows(b):
        shape = (rows, b.shape[1])
        tok = _row_iota(shape) >> (DSA_HEADS.bit_length() - 1)
        out = jnp.broadcast_to(b[0:1, :], shape)
        for t in range(1, dec_seq):
            out = jnp.where(tok == t, jnp.broadcast_to(b[t:t + 1, :], shape), out)
        return out

    q = q_ref[0]
    s = _dot(q, _page_tiles(kbuf, slot)) + mask_rows(bias_ref[0, j])
    _sample_softmax_step(s, _page_tiles(vbuf, slot), m_ref, l_ref, acc_ref)

    @pl.when(j == ng - 1)
    def _():
        s = _dot(q, kn_ref[0].astype(BF16)) + mask_rows(bias_ref[0, ng, :, 0:PAGE_SIZE])
        _sample_softmax_step(s, vn_ref[0].astype(BF16), m_ref, l_ref, acc_ref)
        o_ref[0] = acc_ref[...] / l_ref[...]


def _dsa_sample(page_table, q_rows, bias, k_new, v_new, k_pool, v_pool, *, layer, dec_seq):
    db, n_pages = page_table.shape
    ng = n_pages // DSA_PAGES
    ch = DSA_PAGES * PAGE_SIZE
    rows = dec_seq * DSA_HEADS
    per_b = lambda b, j, pt: (b, 0, 0)
    grid_spec = pltpu.PrefetchScalarGridSpec(
        num_scalar_prefetch=1,
        grid=(db, ng),
        in_specs=[
            pl.BlockSpec((1, rows, DSA_KV), per_b),
            pl.BlockSpec((1, ng + 1, SUBLANES, ch), lambda b, j, pt: (b, 0, 0, 0)),
            pl.BlockSpec((1, DSA_KV, PAGE_SIZE), per_b),
            pl.BlockSpec((1, DSA_KV, PAGE_SIZE), per_b),
        ] + [pl.BlockSpec(memory_space=pl.ANY)] * 2,
        out_specs=pl.BlockSpec((1, rows, DSA_KV), per_b),
        scratch_shapes=[pltpu.VMEM((rows, 1), F32), pltpu.VMEM((rows, 1), F32), pltpu.VMEM((rows, DSA_KV), F32)]
        + _page_ring_scratch((DSA_KV, DSA_KV), DSA_PAGES),
    )
    return pl.pallas_call(
        functools.partial(_dsa_sample_body, layer=layer, dec_seq=dec_seq),
        grid_spec=grid_spec,
        out_shape=jax.ShapeDtypeStruct((db, rows, DSA_KV), F32),
        compiler_params=pltpu.CompilerParams(dimension_semantics=("arbitrary", "arbitrary"), vmem_limit_bytes=VMEM_LIMIT),
        name="dsa_sample",
    )(page_table, q_rows, bias, k_new, v_new, k_pool, v_pool)


_FF_CHUNK = 1024


def _post_body(x_ref, of_ref, od_ref, g1_ref, wg_ref, wof_ref, wod_ref, wout_ref, g2_ref, wup_ref, wdn_ref, gf_ref, y_ref):
    x = x_ref[...]
    d = x.shape[1]
    n = _rmsnorm(x, g1_ref[...]).astype(BF16)
    gate_a = jax.nn.sigmoid(_dot(n, wg_ref[:, 0:d]))
    gate_b = jax.nn.sigmoid(_dot(n, wg_ref[:, d:2 * d]))
    mix = gate_a * _dot(of_ref[...], wof_ref[...]) + gate_b * _dot(od_ref[...], wod_ref[...])
    h = x + _dot(mix.astype(BF16), wout_ref[...])
    n2 = _rmsnorm(h, g2_ref[...]).astype(BF16)
    y = h
    for c in range(wup_ref.shape[1] // _FF_CHUNK):
        u = jnp.maximum(_dot(n2, wup_ref[:, c * _FF_CHUNK:(c + 1) * _FF_CHUNK]), 0.0)
        y = y + _dot((u * u).astype(BF16), wdn_ref[c * _FF_CHUNK:(c + 1) * _FF_CHUNK, :])
    y_ref[...] = _rmsnorm(y, gf_ref[...])


def _post(x2d, o_fox, o_dsa, g1, wg, wof, wod, wout, g2, wup, wdn, gf, *, tm):
    m, d = x2d.shape
    row = lambda i: (i, 0)
    const = lambda i: (0, 0)

    def whole(a):
        return pl.BlockSpec(a.shape, const, pipeline_mode=pl.Buffered(1))

    return pl.pallas_call(
        _post_body,
        grid=(m // tm,),
        in_specs=[pl.BlockSpec((tm, d), row), pl.BlockSpec((tm, FOX_Q), row), pl.BlockSpec((tm, DSA_Q), row),
                  whole(g1), whole(wg), whole(wof), whole(wod), whole(wout), whole(g2), whole(wup), whole(wdn), whole(gf)],
        out_specs=pl.BlockSpec((tm, d), row),
        out_shape=jax.ShapeDtypeStruct((m, d), F32),
        compiler_params=pltpu.CompilerParams(dimension_semantics=("arbitrary",), vmem_limit_bytes=VMEM_LIMIT),
        name="post",
    )(x2d, o_fox, o_dsa, g1, wg, wof, wod, wout, g2, wup, wdn, gf)


def _rope_tables(pos):
    r = HEAD_DIM // ROPE_FRACTION_DIV
    half = r // 2
    inv_freq = jnp.float32(ROPE_THETA) ** (-jnp.arange(half, dtype=F32) * (2.0 / r))
    ang = pos.astype(F32)[:, None] * inv_freq[None, :]
    cos, sin = jnp.cos(ang), jnp.sin(ang)
    p = pos.shape[0]
    one = jnp.ones((p, HEAD_DIM - r), F32)
    zero_r = jnp.zeros((p, HEAD_DIM - r), F32)
    zero_h = jnp.zeros((p, half), F32)
    c = jnp.concatenate([cos, cos, one], axis=1)
    sa = jnp.concatenate([-sin, zero_h, zero_r], axis=1)
    sb = jnp.concatenate([zero_h, sin, zero_r], axis=1)
    return jnp.stack([jnp.tile(t, (1, LANES // HEAD_DIM)) for t in (c, sa, sb)], axis=0)


def _pad_cols(w, width):
    return jnp.pad(w, ((0, 0), (0, width - w.shape[1])))


def _repack_w_in(w_in, d_model):
    sizes = (FOX_Q, FOX_KV, FOX_KV, FOX_HEADS, DSA_Q, DSA_KV, DSA_KV, IDX_Q, IDX_DIM, IDX_HEADS, d_model, d_model)
    cuts = [int(c) for c in np.cumsum(sizes)[:-1]]
    fq, fk, fv, ff, dq, dk, dv, iq, ik, iw, ga, gb = jnp.split(w_in, cuts, axis=-1)
    w1 = jnp.concatenate([fq, fk, fv, dq, dk, dv, iq, _pad_cols(ik, LANES), _pad_cols(ff, LANES), _pad_cols(iw, LANES)], axis=1)
    return w1.astype(BF16), jnp.concatenate([ga, gb], axis=1).astype(BF16)


def _new_page(x, db, dec_seq):
    xt = jnp.swapaxes(x.reshape(db, dec_seq, x.shape[1]), 1, 2)
    return jnp.pad(xt, ((0, 0), (0, 0), (0, PAGE_SIZE - dec_seq)))


def _head_rows(x, db, heads, kv_heads):
    g = heads // kv_heads
    xh = x.reshape(db, -1, kv_heads, g, 1, HEAD_DIM)
    eye = jnp.eye(kv_heads, dtype=x.dtype).reshape(1, 1, kv_heads, 1, kv_heads, 1)
    return (xh * eye).reshape(db, -1, kv_heads * HEAD_DIM)


def _own_features(o, heads, kv_heads):
    db = o.shape[0]
    g = heads // kv_heads
    t = o.shape[1] // heads
    oh = o.reshape(db, t, kv_heads, g, kv_heads, HEAD_DIM)
    idx = jnp.arange(kv_heads)
    return oh[:, :, idx, :, idx, :].transpose(1, 2, 0, 3, 4).reshape(db * t, heads * HEAD_DIM)


def _feature_major(pool):
    lead = pool.shape[:2]
    return jnp.moveaxis(pool.reshape(lead + (PAGE_SIZE, -1)), 2, 3)


def _token_major(xt, tail):
    b, _, t = xt.shape
    return jnp.moveaxis(xt, 1, 2).reshape((1, b, t) + tail)


def kernel(x_prompt, x_sample, cache_fox_k, cache_fox_v, cache_fox_logf, cache_dsa_k, cache_dsa_v, cache_idx_k,
           page_table, norm1_g, w_in, b_forget, w_o_fox, w_o_dsa, w_out, norm2_g, w_up, w_down, final_norm_g):
    batch, seq_len, d = x_prompt.shape
    db, dec_seq, _ = x_sample.shape
    depth = norm1_g.shape[0]
    n_pages = page_table.shape[1]
    past = n_pages * PAGE_SIZE
    assert dec_seq <= SUBLANES and dec_seq & (dec_seq - 1) == 0
    assert n_pages % FOX_PAGES == 0 and n_pages % DSA_PAGES == 0

    tm_p = min(TOKEN_TILE, seq_len)
    tm_s = db * dec_seq
    rope_p = _rope_tables(jnp.arange(seq_len))
    rope_s = _rope_tables(jnp.tile(past + jnp.arange(dec_seq), db))

    fox_k_pool, fox_v_pool, fox_lf_pool, dsa_k_pool, dsa_v_pool, idx_k_pool = (
        _feature_major(c) for c in (cache_fox_k, cache_fox_v, cache_fox_logf, cache_dsa_k, cache_dsa_v, cache_idx_k))

    xp = x_prompt.reshape(batch * seq_len, d)
    xs = x_sample.reshape(db * dec_seq, d)
    rows_p, rows_s = [], []
    for l in range(depth):
        w1, wg = _repack_w_in(w_in[l], d)
        bias = _pad_cols(b_forget[l][None, :], LANES)
        g1 = norm1_g[l][None, :]
        g2 = norm2_g[l][None, :]
        gf = final_norm_g[None, :] if l == depth - 1 else jnp.ones((1, d), F32)
        post_w = (g1, wg, w_o_fox[l].astype(BF16), w_o_dsa[l].astype(BF16), w_out[l].astype(BF16), g2,
                  w_up[l].astype(BF16), w_down[l].astype(BF16), gf)

        (fq, dq, fk, fv, lf, dk, dv, ik, negc, fkt, fvd, dkt, dvd, iqt, iwt, ikb) = _proj(
            xp, g1, w1, bias, rope_p, prompt=True, seq_len=seq_len, tm=tm_p)
        o_fox = _fox_prompt(fq, fkt, fvd, negc, batch=batch, seq_len=seq_len, tq=min(FOX_Q_TILE, seq_len), tk=tm_p)
        o_dsa = _dsa_prompt(iqt, iwt, dq, ikb, dkt, dvd, batch=batch, seq_len=seq_len, tq=min(DSA_Q_TILE, seq_len), ch=tm_p)
        xp = _post(xp, o_fox, o_dsa, *post_w, tm=tm_p)
        rows_p.append((fk, fv, lf, dk, dv, ik))

        (fq, dq, fk, fv, lf, dk, dv, ik, iq, iw) = _proj(xs, g1, w1, bias, rope_s, prompt=False, seq_len=tm_s, tm=tm_s)
        o_fox = _fox_sample(
            page_table, _head_rows(fq, db, FOX_HEADS, FOX_KV_HEADS), _new_page(fk, db, dec_seq), _new_page(fv, db, dec_seq),
            _new_page(lf, db, dec_seq), fox_k_pool, fox_v_pool, fox_lf_pool, layer=l, dec_seq=dec_seq)
        ksel = min(TOPK_MAX, (past + dec_seq) // 4)
        sel_bias = _idx_sample(
            page_table, iq.reshape(db, dec_seq * IDX_HEADS, IDX_DIM), iw.reshape(db, dec_seq * IDX_HEADS, 1),
            _new_page(ik, db, dec_seq), idx_k_pool, layer=l, dec_seq=dec_seq, ksel=ksel)
        o_dsa = _dsa_sample(
            page_table, _head_rows(dq, db, DSA_HEADS, DSA_KV_HEADS), sel_bias, _new_page(dk, db, dec_seq),
            _new_page(dv, db, dec_seq), dsa_k_pool, dsa_v_pool, layer=l, dec_seq=dec_seq)
        xs = _post(xs, _own_features(o_fox, FOX_HEADS, FOX_KV_HEADS).astype(BF16),
                   _own_features(o_dsa, DSA_HEADS, DSA_KV_HEADS).astype(BF16), *post_w, tm=tm_s)
        rows_s.append((fk, fv, lf, dk, dv, ik))

    tails = ((FOX_KV_HEADS, HEAD_DIM), (FOX_KV_HEADS, HEAD_DIM), (FOX_HEADS,), (DSA_KV_HEADS, HEAD_DIM),
             (DSA_KV_HEADS, HEAD_DIM), (IDX_DIM,))
    dtypes = tuple(c.dtype for c in (cache_fox_k, cache_fox_v, cache_fox_logf, cache_dsa_k, cache_dsa_v, cache_idx_k))
    new_p = tuple(jnp.concatenate([_token_major(r[k], tails[k]) for r in rows_p], axis=0).astype(dtypes[k])
                  for k in range(len(tails)))
    new_s = tuple(jnp.stack([r[k].reshape((db, dec_seq) + tails[k]) for r in rows_s], axis=0).astype(dtypes[k])
                  for k in range(len(tails)))
    return (xp.reshape(batch, seq_len, d), xs.reshape(db, dec_seq, d)) + new_p + new_s
```

```python
import functools

import jax
import jax.numpy as jnp
import numpy as np
from jax import lax
from jax.experimental import pallas as pl
from jax.experimental.pallas import tpu as pltpu

HEAD_DIM = 64
FOX_HEADS = 8
FOX_KV_HEADS = 4
DSA_HEADS = 8
DSA_KV_HEADS = 2
IDX_HEADS = 8
IDX_DIM = 64
TOPK_MAX = 256
ROPE_THETA = 500000.0
ROPE_FRACTION_DIV = 4
NORM_EPS = 1e-6
PAGE_SIZE = 128

FOX_Q = FOX_HEADS * HEAD_DIM
FOX_KV = FOX_KV_HEADS * HEAD_DIM
DSA_Q = DSA_HEADS * HEAD_DIM
DSA_KV = DSA_KV_HEADS * HEAD_DIM
IDX_Q = IDX_HEADS * IDX_DIM
FOX_GROUPS = FOX_HEADS // FOX_KV_HEADS
DSA_GROUPS = DSA_HEADS // DSA_KV_HEADS

LANES = 128
SUBLANES = 8
NEG = -(2.0 ** 100)
QK_SCALE = HEAD_DIM ** -0.5
VMEM_LIMIT = 56 * 1024 * 1024

TOKEN_TILE = 512
FOX_Q_TILE = 256
DSA_Q_TILE = 128
FOX_PAGES = 16
DSA_PAGES = 16
BISECT_PASSES = 16
SKIP_MARGIN = 110.0
NORM_SLACK = 1.05

F32 = jnp.float32
BF16 = jnp.bfloat16

_C_FOX = 0
_C_DSA = _C_FOX + FOX_Q + 2 * FOX_KV
_C_IQ = _C_DSA + DSA_Q + 2 * DSA_KV
_C_IK = _C_IQ + IDX_Q
_C_FF = _C_IK + LANES
_C_IW = _C_FF + LANES
_C_END = _C_IW + LANES


def _dot(a, b):
    return jnp.dot(a, b, preferred_element_type=F32)


def _dot_nt(a, b):
    return lax.dot_general(a, b, (((1,), (1,)), ((), ())), preferred_element_type=F32)


def _split3(x):
    hi = x.astype(BF16)
    r1 = x - hi.astype(F32)
    mid = r1.astype(BF16)
    lo = (r1 - mid.astype(F32)).astype(BF16)
    return hi, mid, lo


def _lane_iota(shape):
    return lax.broadcasted_iota(jnp.int32, shape, len(shape) - 1)


def _row_iota(shape):
    return lax.broadcasted_iota(jnp.int32, shape, len(shape) - 2)


def _rope(x, cos, sa, sb):
    w = x.shape[1]
    reps = w // LANES
    if reps > 1:
        cos = jnp.concatenate([cos] * reps, axis=1)
        sa = jnp.concatenate([sa] * reps, axis=1)
        sb = jnp.concatenate([sb] * reps, axis=1)
    up = pltpu.roll(x, w - 8, 1)
    dn = pltpu.roll(x, 8, 1)
    return x * cos + up * sa + dn * sb


def _rmsnorm(x, g):
    ms = jnp.mean(x * x, axis=-1, keepdims=True)
    return x * lax.rsqrt(ms + NORM_EPS) * g


def _proj_body(x_ref, g_ref, w_ref, b_ref, rope_ref, *refs, prompt, blocks_per_seq):
    fq_ref, dq_ref = refs[:2]
    tm = x_ref.shape[0]
    n = _rmsnorm(x_ref[...], g_ref[...]).astype(BF16)
    cos, sa, sb = rope_ref[0], rope_ref[1], rope_ref[2]

    def mm(lo, hi):
        return _dot(n, w_ref[:, lo:hi])

    fq = mm(_C_FOX, _C_FOX + FOX_Q)
    fq_ref[...] = (fq * QK_SCALE).astype(BF16)
    fk = mm(_C_FOX + FOX_Q, _C_FOX + FOX_Q + FOX_KV)
    fv = mm(_C_FOX + FOX_Q + FOX_KV, _C_DSA)
    dq = _rope(mm(_C_DSA, _C_DSA + DSA_Q), cos, sa, sb)
    dq_ref[...] = (dq * QK_SCALE).astype(BF16)
    dk = _rope(mm(_C_DSA + DSA_Q, _C_DSA + DSA_Q + DSA_KV), cos, sa, sb)
    dv = mm(_C_DSA + DSA_Q + DSA_KV, _C_IQ)
    iq = _rope(mm(_C_IQ, _C_IK), cos, sa, sb) * (IDX_DIM ** -0.5)
    ikg = _rope(mm(_C_IK, _C_FF), cos, sa, sb)
    ff = mm(_C_FF, _C_IW) + b_ref[...]
    lf = jnp.minimum(ff, 0.0) - jnp.log(1.0 + jnp.exp(-jnp.abs(ff)))
    iw = mm(_C_IW, _C_END) * (IDX_HEADS ** -0.5)

    if not prompt:
        fk_ref, fv_ref, lf_ref, dk_ref, dv_ref, ik_ref, iq_ref, iw_ref = refs[2:]
        fk_ref[...] = fk
        fv_ref[...] = fv
        lf_ref[...] = lf[:, :FOX_HEADS]
        dk_ref[...] = dk
        dv_ref[...] = dv
        ik_ref[...] = ikg[:, :IDX_DIM]
        iq_ref[...] = iq.astype(BF16)
        iw_ref[...] = iw[:, :IDX_HEADS]
        return
    (fkx_ref, fvx_ref, lfx_ref, dkx_ref, dvx_ref, ikx_ref,
     negc_ref, fkt_ref, fvd_ref, dkt_ref, dvd_ref, iqt_ref, iwt_ref, ikb_ref, fkn_ref, carry_ref) = refs[2:]
    low = _lane_iota((tm, LANES)) < HEAD_DIM

    fk_t, dk_t, ik_t = fk.T, dk.T, ikg.T
    fkx_ref[0] = fk_t
    fvx_ref[0] = fv.T
    lfx_ref[0] = lf.T[:FOX_HEADS, :]
    dkx_ref[0] = dk_t
    dvx_ref[0] = dv.T
    ikx_ref[0] = ik_t[:IDX_DIM, :]

    @pl.when(pl.program_id(0) % blocks_per_seq == 0)
    def _():
        carry_ref[...] = jnp.zeros_like(carry_ref)

    tri = (_row_iota((tm, tm)) >= _lane_iota((tm, tm))).astype(BF16)
    hi, mid, lo = _split3(lf)
    csum = _dot(tri, hi) + _dot(tri, mid) + _dot(tri, lo) + carry_ref[0:1, :]
    carry_ref[0:1, :] = csum[tm - 1:tm, :]
    negc_ref[0] = -(csum.T[:FOX_HEADS, :])

    grp = ((_row_iota((FOX_KV, LANES)) >> (HEAD_DIM.bit_length() - 1))
           == (_lane_iota((FOX_KV, LANES)) >> (FOX_GROUPS.bit_length() - 1)))
    fkn_ref[0] = jnp.max(_dot((fk * fk).astype(BF16), grp.astype(BF16)), axis=0, keepdims=True)

    fkt = fk_t.astype(BF16)
    for h in range(FOX_KV_HEADS):
        kt = fkt[h * HEAD_DIM:(h + 1) * HEAD_DIM, :]
        fkt_ref[h, 0] = jnp.concatenate([kt, kt], axis=0)
    for p in range(FOX_KV_HEADS // 2):
        pair = fv[:, p * LANES:(p + 1) * LANES]
        swap = pltpu.roll(pair, HEAD_DIM, 1)
        fvd_ref[2 * p] = jnp.where(low, pair, swap).astype(BF16)
        fvd_ref[2 * p + 1] = jnp.where(low, swap, pair).astype(BF16)

    dkt = dk_t.astype(BF16)
    swap = pltpu.roll(dv, HEAD_DIM, 1)
    dv_dup = (jnp.where(low, dv, swap).astype(BF16), jnp.where(low, swap, dv).astype(BF16))
    for h in range(DSA_KV_HEADS):
        kt = dkt[h * HEAD_DIM:(h + 1) * HEAD_DIM, :]
        dkt_ref[h, 0] = jnp.concatenate([kt, kt, kt, kt], axis=0)
        dvd_ref[h] = dv_dup[h]

    iqt_ref[...] = iq.T.astype(BF16)
    iwt_ref[...] = iw.T[:IDX_HEADS, :]
    ikb_ref[0] = ikg[:, :IDX_DIM].astype(BF16)


def _proj(x2d, g1, w1, bias, rope_tab, *, prompt, seq_len, tm):
    m, d = x2d.shape
    nblk = m // tm
    npos = rope_tab.shape[1] // tm
    row = lambda i: (i, 0)
    col = lambda i: (0, i)
    outs = [
        (jax.ShapeDtypeStruct((m, FOX_Q), BF16), pl.BlockSpec((tm, FOX_Q), row)),
        (jax.ShapeDtypeStruct((m, DSA_Q), BF16), pl.BlockSpec((tm, DSA_Q), row)),
    ]
    cache_widths = (FOX_KV, FOX_KV, FOX_HEADS, DSA_KV, DSA_KV, IDX_DIM)
    scratch = []
    if prompt:
        bps = seq_len // tm
        feat_major = lambda i: (i // bps, 0, i % bps)
        outs += [(jax.ShapeDtypeStruct((m // seq_len, w, seq_len), F32), pl.BlockSpec((1, w, tm), feat_major))
                 for w in cache_widths]
        outs += [
            (jax.ShapeDtypeStruct((nblk, FOX_HEADS, tm), F32), pl.BlockSpec((1, FOX_HEADS, tm), lambda i: (i, 0, 0))),
            (jax.ShapeDtypeStruct((FOX_KV_HEADS, nblk, 2 * HEAD_DIM, tm), BF16),
             pl.BlockSpec((FOX_KV_HEADS, 1, 2 * HEAD_DIM, tm), lambda i: (0, i, 0, 0))),
            (jax.ShapeDtypeStruct((FOX_KV_HEADS, m, 2 * HEAD_DIM), BF16),
             pl.BlockSpec((FOX_KV_HEADS, tm, 2 * HEAD_DIM), lambda i: (0, i, 0))),
            (jax.ShapeDtypeStruct((DSA_KV_HEADS, nblk, 4 * HEAD_DIM, tm), BF16),
             pl.BlockSpec((DSA_KV_HEADS, 1, 4 * HEAD_DIM, tm), lambda i: (0, i, 0, 0))),
            (jax.ShapeDtypeStruct((DSA_KV_HEADS, m, 2 * HEAD_DIM), BF16),
             pl.BlockSpec((DSA_KV_HEADS, tm, 2 * HEAD_DIM), lambda i: (0, i, 0))),
            (jax.ShapeDtypeStruct((IDX_Q, m), BF16), pl.BlockSpec((IDX_Q, tm), col)),
            (jax.ShapeDtypeStruct((IDX_HEADS, m), F32), pl.BlockSpec((IDX_HEADS, tm), col)),
            (jax.ShapeDtypeStruct((nblk, tm, IDX_DIM), BF16), pl.BlockSpec((1, tm, IDX_DIM), lambda i: (i, 0, 0))),
            (jax.ShapeDtypeStruct((nblk, 1, LANES), F32), pl.BlockSpec((1, 1, LANES), lambda i: (i, 0, 0))),
        ]
        scratch = [pltpu.VMEM((SUBLANES, LANES), F32)]
    else:
        outs += [(jax.ShapeDtypeStruct((m, w), F32), pl.BlockSpec((tm, w), row)) for w in cache_widths]
        outs += [
            (jax.ShapeDtypeStruct((m, IDX_Q), BF16), pl.BlockSpec((tm, IDX_Q), row)),
            (jax.ShapeDtypeStruct((m, IDX_HEADS), F32), pl.BlockSpec((tm, IDX_HEADS), row)),
        ]
    const = lambda i: (0, 0)
    return pl.pallas_call(
        functools.partial(_proj_body, prompt=prompt, blocks_per_seq=seq_len // tm),
        grid=(nblk,),
        in_specs=[
            pl.BlockSpec((tm, d), row),
            pl.BlockSpec((1, d), const),
            pl.BlockSpec((d, _C_END), const, pipeline_mode=pl.Buffered(1)),
            pl.BlockSpec((1, LANES), const),
            pl.BlockSpec((3, tm, LANES), lambda i: (0, i % npos, 0)),
        ],
        out_specs=[o[1] for o in outs],
        out_shape=[o[0] for o in outs],
        scratch_shapes=scratch,
        compiler_params=pltpu.CompilerParams(dimension_semantics=("arbitrary",), vmem_limit_bytes=VMEM_LIMIT),
        name="proj_prompt" if prompt else "proj_sample",
    )(x2d, g1, w1, bias, rope_tab)


def _softmax_step(s, m, l, acc, v):
    m_new = jnp.maximum(m, jnp.max(s, axis=-1, keepdims=True))
    alpha = jnp.exp(m - m_new)
    p = jnp.exp(s - m_new)
    l = alpha * l + jnp.sum(p, axis=-1, keepdims=True)
    acc = alpha * acc + _dot(p.astype(BF16), v)
    return m_new, l, acc


def _softmax_init(rows, width):
    return jnp.full((rows, 1), NEG, F32), jnp.zeros((rows, 1), F32), jnp.zeros((rows, width), F32)


def _pick_head_lanes(acc, tq, groups):
    lane = _lane_iota((tq, groups * HEAD_DIM))
    out = acc[0:tq]
    for g in range(1, groups):
        out = jnp.where(lane >= g * HEAD_DIM, acc[g * tq:(g + 1) * tq], out)
    return out


def _stack_heads(q, tq, groups):
    lane = _lane_iota((tq, groups * HEAD_DIM))
    zero = jnp.zeros_like(q)
    return jnp.concatenate(
        [jnp.where((lane >= g * HEAD_DIM) & (lane < (g + 1) * HEAD_DIM), q, zero) for g in range(groups)], axis=0)


def _fox_skip_count(q_ref, kn_ref, first_ref, last_ref, nfull):
    qf = q_ref[...].astype(F32)
    head_of = ((_row_iota((FOX_Q, LANES)) >> (HEAD_DIM.bit_length() - 1)) == _lane_iota((FOX_Q, LANES))).astype(BF16)
    qn2 = jnp.max(_dot((qf * qf).astype(BF16), head_of), axis=0, keepdims=True)
    kn2 = jnp.max(kn_ref[:, 0, :], axis=0, keepdims=True)
    bound = NORM_SLACK * jnp.sqrt(qn2 * kn2)
    thr = jnp.where(_lane_iota((1, LANES)) < FOX_HEADS, 2.0 * bound + SKIP_MARGIN, -1.0)
    gap = first_ref[0] - last_ref[:, 0, :]
    skip = jnp.min(jnp.where(gap > thr, 1.0, 0.0), axis=1, keepdims=True)
    blk = _row_iota(skip.shape)
    return jnp.sum(jnp.where(blk < nfull, skip, 0.0)).astype(jnp.int32)


def _fox_prompt_body(q_ref, kt_ref, v_ref, negc_ref, kn_ref, first_ref, last_ref, o_ref, *, tq, tk):
    i = pl.program_id(1)
    nfull = (i * tq) // tk
    nskip = _fox_skip_count(q_ref, kn_ref, first_ref, last_ref, nfull)
    g = FOX_GROUPS
    w = g * HEAD_DIM
    qpos = i * tq + (_row_iota((g * tq, tk)) & (tq - 1))
    kloc = _lane_iota((g * tq, tk))
    qs = [_stack_heads(q_ref[:, h * w:(h + 1) * w], tq, g) for h in range(FOX_KV_HEADS)]

    def step(j, carry, masked):
        nb = negc_ref[j]
        off = pl.multiple_of(j * tk, tk)
        out = []
        for h in range(FOX_KV_HEADS):
            bias = jnp.concatenate(
                [jnp.broadcast_to(nb[h * g + k:h * g + k + 1, :], (tq, tk)) for k in range(g)], axis=0)
            s = _dot(qs[h], kt_ref[h, j]) + bias
            if masked:
                s = jnp.where(j * tk + kloc <= qpos, s, NEG)
            out.append(_softmax_step(s, *carry[h], v_ref[h, pl.ds(off, tk), :]))
        return tuple(out)

    init = tuple(_softmax_init(g * tq, w) for _ in range(FOX_KV_HEADS))
    carry = lax.fori_loop(nskip, nfull, lambda j, c: step(j, c, False), init)
    carry = step(nfull, carry, True)
    for h in range(FOX_KV_HEADS):
        m, l, acc = carry[h]
        o_ref[:, h * w:(h + 1) * w] = _pick_head_lanes(acc / l, tq, g).astype(BF16)


def _fox_prompt(fq, fkt, fvd, negc, fkn, *, batch, seq_len, tq, tk):
    m = fq.shape[0]
    nq = seq_len // tq
    nkb = seq_len // tk
    lane_pad = lambda a: jnp.pad(a, ((0, 0), (0, LANES - FOX_HEADS)))[:, None, :]
    a_last = lane_pad(negc[:, :, tk - 1])
    a_first = lane_pad(jnp.swapaxes(negc[:, :, ::tq], 1, 2).reshape(batch * nq, FOX_HEADS))
    return pl.pallas_call(
        functools.partial(_fox_prompt_body, tq=tq, tk=tk),
        grid=(batch, nq),
        in_specs=[
            pl.BlockSpec((tq, FOX_Q), lambda b, i: (b * nq + i, 0)),
            pl.BlockSpec((FOX_KV_HEADS, nkb, 2 * HEAD_DIM, tk), lambda b, i: (0, b, 0, 0), pipeline_mode=pl.Buffered(1)),
            pl.BlockSpec((FOX_KV_HEADS, seq_len, 2 * HEAD_DIM), lambda b, i: (0, b, 0), pipeline_mode=pl.Buffered(1)),
            pl.BlockSpec((nkb, FOX_HEADS, tk), lambda b, i: (b, 0, 0), pipeline_mode=pl.Buffered(1)),
            pl.BlockSpec((nkb, 1, LANES), lambda b, i: (b, 0, 0)),
            pl.BlockSpec((1, 1, LANES), lambda b, i: (b * nq + i, 0, 0)),
            pl.BlockSpec((nkb, 1, LANES), lambda b, i: (b, 0, 0)),
        ],
        out_specs=pl.BlockSpec((tq, FOX_Q), lambda b, i: (b * nq + i, 0)),
        out_shape=jax.ShapeDtypeStruct((m, FOX_Q), BF16),
        compiler_params=pltpu.CompilerParams(dimension_semantics=("arbitrary", "arbitrary"), vmem_limit_bytes=VMEM_LIMIT),
        name="fox_prompt",
    )(fq, fkt, fvd, negc, fkn, a_first, a_last)


def _topk_threshold(reduce_fn, idx_of, nvalid, ksel, n_idx):
    kf = jnp.float32(ksel)
    inf = jnp.float32(jnp.inf)
    one, zero = jnp.float32(1.0), jnp.float32(0.0)
    shape = nvalid.shape

    def count_ge(t):
        return reduce_fn(lambda c, blk: jnp.where(blk >= t, one, zero), "sum")

    row_max = reduce_fn(lambda c, blk: blk, "max")
    row_min = reduce_fn(lambda c, blk: jnp.where(blk > -inf, blk, inf), "min")
    all_sel = jnp.where(nvalid <= kf, one, zero)

    def bisect(_, st):
        lo, clo, hi, chi, done = st
        t = jnp.where(hi == inf, row_max, 0.5 * (lo + hi))
        cnt = count_ge(t)
        upd = (done < 0.5) & (t > lo) & (t < hi)
        ge = cnt >= kf
        lo = jnp.where(upd & ge, t, lo)
        clo = jnp.where(upd & ge, cnt, clo)
        hi = jnp.where(upd & (cnt < kf), t, hi)
        chi = jnp.where(upd & (cnt < kf), cnt, chi)
        return lo, clo, hi, chi, jnp.where(clo == kf, one, done)

    st = (row_min, nvalid, jnp.full(shape, inf, F32), jnp.zeros(shape, F32), all_sel)
    st = lax.fori_loop(0, BISECT_PASSES, bisect, st)

    def snap(st):
        lo, clo, hi, chi, done = st
        v = reduce_fn(lambda c, blk: jnp.where(blk < hi, blk, -inf), "max")
        cv = count_ge(v)
        fin = (done < 0.5) & (cv >= kf)
        go = (done < 0.5) & (cv < kf)
        lo = jnp.where(fin, v, lo)
        clo = jnp.where(fin, cv, clo)
        hi = jnp.where(go, v, hi)
        chi = jnp.where(go, cv, chi)
        return lo, clo, hi, chi, jnp.where(fin, one, done)

    lo, clo, hi, chi, done = lax.while_loop(lambda st: jnp.min(st[4]) < 0.5, snap, st)

    thr = jnp.where(all_sel > 0.5, -inf, lo)
    need_cut = jnp.where((all_sel < 0.5) & (clo > kf), one, zero)
    need = kf - chi
    big = jnp.float32(n_idx)

    def cut_search():
        def body(_, st):
            jlo, jhi = st
            mid = jnp.floor(0.5 * (jlo + jhi))
            cnt = reduce_fn(lambda c, blk: jnp.where((blk == thr) & (idx_of(c) <= mid), one, zero), "sum")
            ok = cnt >= need
            return jnp.where(ok, jlo, mid), jnp.where(ok, mid, jhi)

        npass = int(np.ceil(np.log2(n_idx))) + 1
        _, jhi = lax.fori_loop(0, npass, body, (jnp.full(shape, -1.0, F32), jnp.full(shape, n_idx - 1.0, F32)))
        return jhi

    jcut = lax.cond(jnp.max(need_cut) > 0.5, cut_search, lambda: jnp.full(shape, -1.0, F32))
    jcut = jnp.where(need_cut > 0.5, jcut, jnp.where(all_sel > 0.5, -1.0, big))
    return thr, jcut


_REDUCE_OPS = {"sum": (jnp.add, 0.0, jnp.sum), "max": (jnp.maximum, -np.inf, jnp.max), "min": (jnp.minimum, np.inf, jnp.min)}


def _tree(parts, op):
    while len(parts) > 1:
        parts = [op(parts[k], parts[k + 1]) if k + 1 < len(parts) else parts[k] for k in range(0, len(parts), 2)]
    return parts[0]


def _row_major_reducer(s_ref, nch):
    _, rows, ch = s_ref.shape

    def reduce_fn(elem_fn, kind):
        op, init, lane_reduce = _REDUCE_OPS[kind]

        def body(c, acc):
            x = elem_fn(c, s_ref[c])
            return op(acc, _tree([x[:, k * LANES:(k + 1) * LANES] for k in range(ch // LANES)], op))

        acc = jnp.full((rows, LANES), init, F32)
        if isinstance(nch, int):
            for c in range(nch):
                acc = body(c, acc)
        else:
            acc = lax.fori_loop(0, nch, body, acc)
        return lane_reduce(acc, axis=1, keepdims=True)

    return reduce_fn


def _lane_major_reducer(s_ref, nch):
    _, ch, _ = s_ref.shape

    def reduce_fn(elem_fn, kind):
        op, init, sub_reduce = _REDUCE_OPS[kind]

        def body(c, acc):
            x = elem_fn(c, s_ref[c])
            return op(acc, _tree([x[k * SUBLANES:(k + 1) * SUBLANES, :] for k in range(ch // SUBLANES)], op))

        acc = lax.fori_loop(0, nch, body, jnp.full((SUBLANES, LANES), init, F32))
        return sub_reduce(acc, axis=0, keepdims=True)

    return reduce_fn


def _dsa_prompt_body(iqt_ref, iwt_ref, dq_ref, ikb_ref, dkt_ref, dv_ref, o_ref, s_ref, bias_ref, *, tq, ch, ksel):
    i = pl.program_id(1)
    nch = (i * tq + tq - 1) // ch + 1
    g = DSA_GROUPS
    w = g * HEAD_DIM

    iqt = iqt_ref[...]
    rhs = [jnp.concatenate([iqt[(2 * p) * IDX_DIM:(2 * p + 1) * IDX_DIM, :], iqt[(2 * p + 1) * IDX_DIM:(2 * p + 2) * IDX_DIM, :]],
                           axis=1) for p in range(IDX_HEADS // 2)]
    iwt = iwt_ref[...]
    qpos = i * tq + _lane_iota((ch, tq))
    kloc = _row_iota((ch, tq))

    def score_chunk(c, _):
        keys = ikb_ref[c]
        acc = None
        for p in range(IDX_HEADS // 2):
            d = jnp.maximum(_dot(keys, rhs[p]), 0.0)
            term = d[:, :tq] * iwt[2 * p:2 * p + 1, :] + d[:, tq:] * iwt[2 * p + 1:2 * p + 2, :]
            acc = term if acc is None else acc + term
        s_ref[c] = jnp.where(c * ch + kloc <= qpos, acc, -jnp.inf)
        return 0

    lax.fori_loop(0, nch, score_chunk, 0)

    nvalid = (i * tq + _lane_iota((1, tq)) + 1).astype(F32)
    idx_of = lambda c: (c * ch + kloc).astype(F32)
    thr, jcut = _topk_threshold(_lane_major_reducer(s_ref, nch), idx_of, nvalid, ksel, s_ref.shape[0] * ch)

    def write_bias(c, _):
        blk = s_ref[c]
        sel = (blk > thr) | ((blk == thr) & (idx_of(c) <= jcut))
        bias_ref[c] = jnp.where(sel, 0.0, NEG).T
        return 0

    lax.fori_loop(0, nch, write_bias, 0)

    qs = [_stack_heads(dq_ref[:, h * w:(h + 1) * w], tq, g) for h in range(DSA_KV_HEADS)]

    def step(c, carry):
        b = bias_ref[c]
        bias = jnp.concatenate([b] * g, axis=0)
        off = pl.multiple_of(c * ch, ch)
        return tuple(_softmax_step(_dot(qs[h], dkt_ref[h, c]) + bias, *carry[h], dv_ref[h, pl.ds(off, ch), :])
                     for h in range(DSA_KV_HEADS))

    carry = lax.fori_loop(0, nch, step, tuple(_softmax_init(g * tq, 2 * HEAD_DIM) for _ in range(DSA_KV_HEADS)))
    low = _lane_iota((tq, 2 * HEAD_DIM)) < HEAD_DIM
    for h in range(DSA_KV_HEADS):
        m, l, acc = carry[h]
        o = acc / l
        pairs = [jnp.where(low, o[(2 * p) * tq:(2 * p + 1) * tq], o[(2 * p + 1) * tq:(2 * p + 2) * tq]) for p in range(g // 2)]
        o_ref[:, h * w:(h + 1) * w] = jnp.concatenate(pairs, axis=1).astype(BF16)


def _dsa_prompt(iqt, iwt, dq, ikb, dkt, dvd, *, batch, seq_len, tq, ch):
    m = dq.shape[0]
    nq = seq_len // tq
    nkb = seq_len // ch
    ksel = min(TOPK_MAX, seq_len // 4)
    row = lambda b, i: (b * nq + i, 0)
    col = lambda b, i: (0, b * nq + i)
    return pl.pallas_call(
        functools.partial(_dsa_prompt_body, tq=tq, ch=ch, ksel=ksel),
        grid=(batch, nq),
        in_specs=[
            pl.BlockSpec((IDX_Q, tq), col),
            pl.BlockSpec((IDX_HEADS, tq), col),
            pl.BlockSpec((tq, DSA_Q), row),
            pl.BlockSpec((nkb, ch, IDX_DIM), lambda b, i: (b, 0, 0), pipeline_mode=pl.Buffered(1)),
            pl.BlockSpec((DSA_KV_HEADS, nkb, 4 * HEAD_DIM, ch), lambda b, i: (0, b, 0, 0), pipeline_mode=pl.Buffered(1)),
            pl.BlockSpec((DSA_KV_HEADS, seq_len, 2 * HEAD_DIM), lambda b, i: (0, b, 0), pipeline_mode=pl.Buffered(1)),
        ],
        out_specs=pl.BlockSpec((tq, DSA_Q), row),
        out_shape=jax.ShapeDtypeStruct((m, DSA_Q), BF16),
        scratch_shapes=[pltpu.VMEM((nkb, ch, tq), F32), pltpu.VMEM((nkb, tq, ch), F32)],
        compiler_params=pltpu.CompilerParams(dimension_semantics=("arbitrary", "arbitrary"), vmem_limit_bytes=VMEM_LIMIT),
        name="dsa_prompt",
    )(iqt, iwt, dq, ikb, dkt, dvd)


def _page_ring(pt_ref, pools, bufs, sem, layer, pages):
    ng = pl.num_programs(1)
    step = pl.program_id(0) * ng + pl.program_id(1)
    last = pl.num_programs(0) * ng - 1

    def copies(s):
        b, j, slot = lax.div(s, ng), lax.rem(s, ng), lax.rem(s, 2)
        return [pltpu.make_async_copy(pool.at[layer, pt_ref[b, j * pages + k]], buf.at[slot, k], sem.at[slot, a])
                for a, (pool, buf) in enumerate(zip(pools, bufs)) for k in range(pages)]

    @pl.when(step == 0)
    def _():
        for c in copies(step):
            c.start()

    @pl.when(step < last)
    def _():
        for c in copies(step + 1):
            c.start()

    for c in copies(step):
        c.wait()
    return lax.rem(step, 2)


def _page_ring_scratch(features, pages):
    return [pltpu.VMEM((2, pages, f, PAGE_SIZE), F32) for f in features] + [pltpu.SemaphoreType.DMA((2, len(features)))]


def _page_tiles(buf, slot):
    return jnp.concatenate([buf[slot, k] for k in range(buf.shape[1])], axis=1).astype(BF16)


def _sample_softmax_step(s, vt, m_ref, l_ref, acc_ref):
    m = m_ref[...]
    m_new = jnp.maximum(m, jnp.max(s, axis=-1, keepdims=True))
    alpha = jnp.exp(m - m_new)
    p = jnp.exp(s - m_new)
    m_ref[...] = m_new
    l_ref[...] = alpha * l_ref[...] + jnp.sum(p, axis=-1, keepdims=True)
    acc_ref[...] = alpha * acc_ref[...] + _dot_nt(p.astype(BF16), vt)


def _fox_sample_body(pt_ref, q_ref, kn_ref, vn_ref, lfn_ref, k_hbm, v_hbm, lf_hbm, o_ref, m_ref, l_ref, acc_ref, pre_ref,
                     kbuf, vbuf, lbuf, sem, *, layer, dec_seq):
    np_ = FOX_PAGES
    j = pl.program_id(1)
    rows = dec_seq * FOX_HEADS
    slot = _page_ring(pt_ref, (k_hbm, v_hbm, lf_hbm), (kbuf, vbuf, lbuf), sem, layer, np_)

    @pl.when(j == 0)
    def _():
        m_ref[...] = jnp.full_like(m_ref, NEG)
        l_ref[...] = jnp.zeros_like(l_ref)
        acc_ref[...] = jnp.zeros_like(acc_ref)
        pre_ref[...] = jnp.zeros_like(pre_ref)

    triu = (_row_iota((PAGE_SIZE, PAGE_SIZE)) <= _lane_iota((PAGE_SIZE, PAGE_SIZE))).astype(BF16)

    def cum_keys(lf):
        r = lf.shape[0]
        parts = _dot(jnp.concatenate(_split3(lf), axis=0), triu)
        return parts[0:r] + parts[r:2 * r] + parts[2 * r:3 * r]

    q = q_ref[0]
    nrow = np_ * FOX_HEADS
    within = cum_keys(jnp.concatenate([lbuf[slot, k] for k in range(np_)], axis=0))
    r, c = _row_iota((nrow, nrow)), _lane_iota((nrow, nrow))
    earlier = (((r ^ c) & (FOX_HEADS - 1)) == 0) & (c < r)
    totals = jnp.broadcast_to(within[:, PAGE_SIZE - 1:PAGE_SIZE], (nrow, PAGE_SIZE))
    carried = _dot(earlier.astype(BF16), jnp.concatenate(_split3(totals), axis=1))
    carried = carried[:, 0:PAGE_SIZE] + carried[:, PAGE_SIZE:2 * PAGE_SIZE] + carried[:, 2 * PAGE_SIZE:3 * PAGE_SIZE]
    cum = within + carried + jnp.concatenate([pre_ref[...]] * np_, axis=0)
    pre_ref[...] = jnp.broadcast_to(cum[nrow - FOX_HEADS:nrow, PAGE_SIZE - 1:PAGE_SIZE], pre_ref.shape)
    c_all = jnp.concatenate([cum[k * FOX_HEADS:(k + 1) * FOX_HEADS, :] for k in range(np_)], axis=1)
    s = _dot(q, _page_tiles(kbuf, slot)) - jnp.concatenate([c_all] * dec_seq, axis=0)
    _sample_softmax_step(s, _page_tiles(vbuf, slot), m_ref, l_ref, acc_ref)

    @pl.when(j == pl.num_programs(1) - 1)
    def _():
        c = cum_keys(lfn_ref[0]) + pre_ref[...]
        s = _dot(q, kn_ref[0].astype(BF16)) - jnp.concatenate([c] * dec_seq, axis=0)
        tok = _row_iota((rows, PAGE_SIZE)) >> (FOX_HEADS.bit_length() - 1)
        s = jnp.where(_lane_iota((rows, PAGE_SIZE)) <= tok, s, NEG)
        _sample_softmax_step(s, vn_ref[0].astype(BF16), m_ref, l_ref, acc_ref)
        o_ref[0] = acc_ref[...] / l_ref[...]


def _fox_sample(page_table, q_rows, k_new, v_new, lf_new, k_pool, v_pool, lf_pool, *, layer, dec_seq):
    db, n_pages = page_table.shape
    ng = n_pages // FOX_PAGES
    rows = dec_seq * FOX_HEADS
    per_b = lambda b, j, pt: (b, 0, 0)
    grid_spec = pltpu.PrefetchScalarGridSpec(
        num_scalar_prefetch=1,
        grid=(db, ng),
        in_specs=[
            pl.BlockSpec((1, rows, FOX_KV), per_b),
            pl.BlockSpec((1, FOX_KV, PAGE_SIZE), per_b),
            pl.BlockSpec((1, FOX_KV, PAGE_SIZE), per_b),
            pl.BlockSpec((1, FOX_HEADS, PAGE_SIZE), per_b),
        ] + [pl.BlockSpec(memory_space=pl.ANY)] * 3,
        out_specs=pl.BlockSpec((1, rows, FOX_KV), per_b),
        scratch_shapes=[pltpu.VMEM((rows, 1), F32), pltpu.VMEM((rows, 1), F32), pltpu.VMEM((rows, FOX_KV), F32),
                        pltpu.VMEM((FOX_HEADS, PAGE_SIZE), F32)]
        + _page_ring_scratch((FOX_KV, FOX_KV, FOX_HEADS), FOX_PAGES),
    )
    return pl.pallas_call(
        functools.partial(_fox_sample_body, layer=layer, dec_seq=dec_seq),
        grid_spec=grid_spec,
        out_shape=jax.ShapeDtypeStruct((db, rows, FOX_KV), F32),
        compiler_params=pltpu.CompilerParams(dimension_semantics=("arbitrary", "arbitrary"), vmem_limit_bytes=VMEM_LIMIT),
        name="fox_sample",
    )(page_table, q_rows, k_new, v_new, lf_new, k_pool, v_pool, lf_pool)


def _idx_sample_body(pt_ref, q_ref, w_ref, kn_ref, k_hbm, bias_ref, s_ref, kbuf, sem, *, layer, dec_seq, past, ksel):
    np_ = DSA_PAGES
    j = pl.program_id(1)
    slot = _page_ring(pt_ref, (k_hbm,), (kbuf,), sem, layer, np_)
    ng = pl.num_programs(1)
    ch = np_ * PAGE_SIZE
    q = q_ref[0]
    w = w_ref[0]

    def scores(keys_t):
        d = jnp.maximum(_dot(q, keys_t), 0.0) * w
        shape = (SUBLANES, d.shape[1])
        tok = _row_iota(shape)
        out = jnp.full(shape, -jnp.inf, F32)
        for t in range(dec_seq):
            per_tok = jnp.sum(d[t * IDX_HEADS:(t + 1) * IDX_HEADS], axis=0, keepdims=True)
            out = jnp.where(tok == t, jnp.broadcast_to(per_tok, shape), out)
        return out

    s_ref[j] = scores(_page_tiles(kbuf, slot))

    @pl.when(j == ng - 1)
    def _():
        sn = scores(kn_ref[0].astype(BF16))
        sn = jnp.where(_lane_iota((SUBLANES, PAGE_SIZE)) <= _row_iota((SUBLANES, PAGE_SIZE)), sn, -jnp.inf)
        s_ref[ng] = jnp.concatenate([sn, jnp.full((SUBLANES, ch - PAGE_SIZE), -jnp.inf, F32)], axis=1)
        tok = _row_iota((SUBLANES, 1))
        nvalid = jnp.where(tok < dec_seq, (past + 1 + tok).astype(F32), 0.0)
        idx_of = lambda c: (c * ch + _lane_iota((SUBLANES, ch))).astype(F32)
        thr, jcut = _topk_threshold(_row_major_reducer(s_ref, ng + 1), idx_of, nvalid, ksel, s_ref.shape[0] * ch)
        for c in range(s_ref.shape[0]):
            blk = s_ref[c]
            sel = (blk > thr) | ((blk == thr) & (idx_of(c) <= jcut))
            bias_ref[0, c] = jnp.where(sel, 0.0, NEG)


def _idx_sample(page_table, iq_rows, iw_rows, ik_new, ik_pool, *, layer, dec_seq, ksel):
    db, n_pages = page_table.shape
    ng = n_pages // DSA_PAGES
    ch = DSA_PAGES * PAGE_SIZE
    rows = dec_seq * IDX_HEADS
    per_b = lambda b, j, pt: (b, 0, 0)
    grid_spec = pltpu.PrefetchScalarGridSpec(
        num_scalar_prefetch=1,
        grid=(db, ng),
        in_specs=[
            pl.BlockSpec((1, rows, IDX_DIM), per_b),
            pl.BlockSpec((1, rows, 1), per_b),
            pl.BlockSpec((1, IDX_DIM, PAGE_SIZE), per_b),
            pl.BlockSpec(memory_space=pl.ANY),
        ],
        out_specs=pl.BlockSpec((1, ng + 1, SUBLANES, ch), lambda b, j, pt: (b, 0, 0, 0)),
        scratch_shapes=[pltpu.VMEM((ng + 1, SUBLANES, ch), F32)] + _page_ring_scratch((IDX_DIM,), DSA_PAGES),
    )
    return pl.pallas_call(
        functools.partial(_idx_sample_body, layer=layer, dec_seq=dec_seq, past=n_pages * PAGE_SIZE, ksel=ksel),
        grid_spec=grid_spec,
        out_shape=jax.ShapeDtypeStruct((db, ng + 1, SUBLANES, ch), F32),
        compiler_params=pltpu.CompilerParams(dimension_semantics=("arbitrary", "arbitrary"), vmem_limit_bytes=VMEM_LIMIT),
        name="idx_sample",
    )(page_table, iq_rows, iw_rows, ik_new, ik_pool)


def _dsa_sample_body(pt_ref, q_ref, bias_ref, kn_ref, vn_ref, k_hbm, v_hbm, o_ref, m_ref, l_ref, acc_ref, kbuf, vbuf, sem,
                     *, layer, dec_seq):
    np_ = DSA_PAGES
    j = pl.program_id(1)
    ng = pl.num_programs(1)
    slot = _page_ring(pt_ref, (k_hbm, v_hbm), (kbuf, vbuf), sem, layer, np_)
    rows = dec_seq * DSA_HEADS

    @pl.when(j == 0)
    def _():
        m_ref[...] = jnp.full_like(m_ref, NEG)
        l_ref[...] = jnp.zeros_like(l_ref)
        acc_ref[...] = jnp.zeros_like(acc_ref)

    def mask_rows(b):
        shape = (rows, b.shape[1])
        tok = _row_iota(shape) >> (DSA_HEADS.bit_length() - 1)
        out = jnp.broadcast_to(b[0:1, :], shape)
        for t in range(1, dec_seq):
            out = jnp.where(tok == t, jnp.broadcast_to(b[t:t + 1, :], shape), out)
        return out

    q = q_ref[0]
    s = _dot(q, _page_tiles(kbuf, slot)) + mask_rows(bias_ref[0, j])
    _sample_softmax_step(s, _page_tiles(vbuf, slot), m_ref, l_ref, acc_ref)

    @pl.when(j == ng - 1)
    def _():
        s = _dot(q, kn_ref[0].astype(BF16)) + mask_rows(bias_ref[0, ng, :, 0:PAGE_SIZE])
        _sample_softmax_step(s, vn_ref[0].astype(BF16), m_ref, l_ref, acc_ref)
        o_ref[0] = acc_ref[...] / l_ref[...]


def _dsa_sample(page_table, q_rows, bias, k_new, v_new, k_pool, v_pool, *, layer, dec_seq):
    db, n_pages = page_table.shape
    ng = n_pages // DSA_PAGES
    ch = DSA_PAGES * PAGE_SIZE
    rows = dec_seq * DSA_HEADS
    per_b = lambda b, j, pt: (b, 0, 0)
    grid_spec = pltpu.PrefetchScalarGridSpec(
        num_scalar_prefetch=1,
        grid=(db, ng),
        in_specs=[
            pl.BlockSpec((1, rows, DSA_KV), per_b),
            pl.BlockSpec((1, ng + 1, SUBLANES, ch), lambda b, j, pt: (b, 0, 0, 0)),
            pl.BlockSpec((1, DSA_KV, PAGE_SIZE), per_b),
            pl.BlockSpec((1, DSA_KV, PAGE_SIZE), per_b),
        ] + [pl.BlockSpec(memory_space=pl.ANY)] * 2,
        out_specs=pl.BlockSpec((1, rows, DSA_KV), per_b),
        scratch_shapes=[pltpu.VMEM((rows, 1), F32), pltpu.VMEM((rows, 1), F32), pltpu.VMEM((rows, DSA_KV), F32)]
        + _page_ring_scratch((DSA_KV, DSA_KV), DSA_PAGES),
    )
    return pl.pallas_call(
        functools.partial(_dsa_sample_body, layer=layer, dec_seq=dec_seq),
        grid_spec=grid_spec,
        out_shape=jax.ShapeDtypeStruct((db, rows, DSA_KV), F32),
        compiler_params=pltpu.CompilerParams(dimension_semantics=("arbitrary", "arbitrary"), vmem_limit_bytes=VMEM_LIMIT),
        name="dsa_sample",
    )(page_table, q_rows, bias, k_new, v_new, k_pool, v_pool)


_FF_CHUNK = 1024


def _post_body(x_ref, of_ref, od_ref, g1_ref, wg_ref, wof_ref, wod_ref, wout_ref, g2_ref, wup_ref, wdn_ref, gf_ref, y_ref):
    x = x_ref[...]
    d = x.shape[1]
    n = _rmsnorm(x, g1_ref[...]).astype(BF16)
    gate_a = jax.nn.sigmoid(_dot(n, wg_ref[:, 0:d]))
    gate_b = jax.nn.sigmoid(_dot(n, wg_ref[:, d:2 * d]))
    mix = gate_a * _dot(of_ref[...], wof_ref[...]) + gate_b * _dot(od_ref[...], wod_ref[...])
    h = x + _dot(mix.astype(BF16), wout_ref[...])
    n2 = _rmsnorm(h, g2_ref[...]).astype(BF16)
    y = h
    for c in range(wup_ref.shape[1] // _FF_CHUNK):
        u = jnp.maximum(_dot(n2, wup_ref[:, c * _FF_CHUNK:(c + 1) * _FF_CHUNK]), 0.0)
        y = y + _dot((u * u).astype(BF16), wdn_ref[c * _FF_CHUNK:(c + 1) * _FF_CHUNK, :])
    y_ref[...] = _rmsnorm(y, gf_ref[...])


def _post(x2d, o_fox, o_dsa, g1, wg, wof, wod, wout, g2, wup, wdn, gf, *, tm):
    m, d = x2d.shape
    row = lambda i: (i, 0)
    const = lambda i: (0, 0)

    def whole(a):
        return pl.BlockSpec(a.shape, const, pipeline_mode=pl.Buffered(1))

    return pl.pallas_call(
        _post_body,
        grid=(m // tm,),
        in_specs=[pl.BlockSpec((tm, d), row), pl.BlockSpec((tm, FOX_Q), row), pl.BlockSpec((tm, DSA_Q), row),
                  whole(g1), whole(wg), whole(wof), whole(wod), whole(wout), whole(g2), whole(wup), whole(wdn), whole(gf)],
        out_specs=pl.BlockSpec((tm, d), row),
        out_shape=jax.ShapeDtypeStruct((m, d), F32),
        compiler_params=pltpu.CompilerParams(dimension_semantics=("arbitrary",), vmem_limit_bytes=VMEM_LIMIT),
        name="post",
    )(x2d, o_fox, o_dsa, g1, wg, wof, wod, wout, g2, wup, wdn, gf)


def _rope_tables(pos):
    r = HEAD_DIM // ROPE_FRACTION_DIV
    half = r // 2
    inv_freq = jnp.float32(ROPE_THETA) ** (-jnp.arange(half, dtype=F32) * (2.0 / r))
    ang = pos.astype(F32)[:, None] * inv_freq[None, :]
    cos, sin = jnp.cos(ang), jnp.sin(ang)
    p = pos.shape[0]
    one = jnp.ones((p, HEAD_DIM - r), F32)
    zero_r = jnp.zeros((p, HEAD_DIM - r), F32)
    zero_h = jnp.zeros((p, half), F32)
    c = jnp.concatenate([cos, cos, one], axis=1)
    sa = jnp.concatenate([-sin, zero_h, zero_r], axis=1)
    sb = jnp.concatenate([zero_h, sin, zero_r], axis=1)
    return jnp.stack([jnp.tile(t, (1, LANES // HEAD_DIM)) for t in (c, sa, sb)], axis=0)


def _pad_cols(w, width):
    return jnp.pad(w, ((0, 0), (0, width - w.shape[1])))


def _repack_w_in(w_in, d_model):
    sizes = (FOX_Q, FOX_KV, FOX_KV, FOX_HEADS, DSA_Q, DSA_KV, DSA_KV, IDX_Q, IDX_DIM, IDX_HEADS, d_model, d_model)
    cuts = [int(c) for c in np.cumsum(sizes)[:-1]]
    fq, fk, fv, ff, dq, dk, dv, iq, ik, iw, ga, gb = jnp.split(w_in, cuts, axis=-1)
    w1 = jnp.concatenate([fq, fk, fv, dq, dk, dv, iq, _pad_cols(ik, LANES), _pad_cols(ff, LANES), _pad_cols(iw, LANES)], axis=1)
    return w1.astype(BF16), jnp.concatenate([ga, gb], axis=1).astype(BF16)


def _new_page(x, db, dec_seq):
    xt = jnp.swapaxes(x.reshape(db, dec_seq, x.shape[1]), 1, 2)
    return jnp.pad(xt, ((0, 0), (0, 0), (0, PAGE_SIZE - dec_seq)))


def _head_rows(x, db, heads, kv_heads):
    g = heads // kv_heads
    xh = x.reshape(db, -1, kv_heads, g, 1, HEAD_DIM)
    eye = jnp.eye(kv_heads, dtype=x.dtype).reshape(1, 1, kv_heads, 1, kv_heads, 1)
    return (xh * eye).reshape(db, -1, kv_heads * HEAD_DIM)


def _own_features(o, heads, kv_heads):
    db = o.shape[0]
    g = heads // kv_heads
    t = o.shape[1] // heads
    oh = o.reshape(db, t, kv_heads, g, kv_heads, HEAD_DIM)
    idx = jnp.arange(kv_heads)
    return oh[:, :, idx, :, idx, :].transpose(1, 2, 0, 3, 4).reshape(db * t, heads * HEAD_DIM)


def _feature_major(pool):
    lead = pool.shape[:2]
    return jnp.moveaxis(pool.reshape(lead + (PAGE_SIZE, -1)), 2, 3)


def _token_major(xt, tail):
    b, _, t = xt.shape
    return jnp.moveaxis(xt, 1, 2).reshape((1, b, t) + tail)


def kernel(x_prompt, x_sample, cache_fox_k, cache_fox_v, cache_fox_logf, cache_dsa_k, cache_dsa_v, cache_idx_k,
           page_table, norm1_g, w_in, b_forget, w_o_fox, w_o_dsa, w_out, norm2_g, w_up, w_down, final_norm_g):
    batch, seq_len, d = x_prompt.shape
    db, dec_seq, _ = x_sample.shape
    depth = norm1_g.shape[0]
    n_pages = page_table.shape[1]
    past = n_pages * PAGE_SIZE
    assert dec_seq <= SUBLANES and dec_seq & (dec_seq - 1) == 0
    assert n_pages % FOX_PAGES == 0 and n_pages % DSA_PAGES == 0

    tm_p = min(TOKEN_TILE, seq_len)
    tm_s = db * dec_seq
    rope_p = _rope_tables(jnp.arange(seq_len))
    rope_s = _rope_tables(jnp.tile(past + jnp.arange(dec_seq), db))

    fox_k_pool, fox_v_pool, fox_lf_pool, dsa_k_pool, dsa_v_pool, idx_k_pool = (
        _feature_major(c) for c in (cache_fox_k, cache_fox_v, cache_fox_logf, cache_dsa_k, cache_dsa_v, cache_idx_k))

    xp = x_prompt.reshape(batch * seq_len, d)
    xs = x_sample.reshape(db * dec_seq, d)
    rows_p, rows_s = [], []
    for l in range(depth):
        w1, wg = _repack_w_in(w_in[l], d)
        bias = _pad_cols(b_forget[l][None, :], LANES)
        g1 = norm1_g[l][None, :]
        g2 = norm2_g[l][None, :]
        gf = final_norm_g[None, :] if l == depth - 1 else jnp.ones((1, d), F32)
        post_w = (g1, wg, w_o_fox[l].astype(BF16), w_o_dsa[l].astype(BF16), w_out[l].astype(BF16), g2,
                  w_up[l].astype(BF16), w_down[l].astype(BF16), gf)

        (fq, dq, fk, fv, lf, dk, dv, ik, negc, fkt, fvd, dkt, dvd, iqt, iwt, ikb, fkn) = _proj(
            xp, g1, w1, bias, rope_p, prompt=True, seq_len=seq_len, tm=tm_p)
        o_fox = _fox_prompt(fq, fkt, fvd, negc, fkn, batch=batch, seq_len=seq_len, tq=min(FOX_Q_TILE, seq_len), tk=tm_p)
        o_dsa = _dsa_prompt(iqt, iwt, dq, ikb, dkt, dvd, batch=batch, seq_len=seq_len, tq=min(DSA_Q_TILE, seq_len), ch=tm_p)
        xp = _post(xp, o_fox, o_dsa, *post_w, tm=tm_p)
        rows_p.append((fk, fv, lf, dk, dv, ik))

        (fq, dq, fk, fv, lf, dk, dv, ik, iq, iw) = _proj(xs, g1, w1, bias, rope_s, prompt=False, seq_len=tm_s, tm=tm_s)
        o_fox = _fox_sample(
            page_table, _head_rows(fq, db, FOX_HEADS, FOX_KV_HEADS), _new_page(fk, db, dec_seq), _new_page(fv, db, dec_seq),
            _new_page(lf, db, dec_seq), fox_k_pool, fox_v_pool, fox_lf_pool, layer=l, dec_seq=dec_seq)
        ksel = min(TOPK_MAX, (past + dec_seq) // 4)
        sel_bias = _idx_sample(
            page_table, iq.reshape(db, dec_seq * IDX_HEADS, IDX_DIM), iw.reshape(db, dec_seq * IDX_HEADS, 1),
            _new_page(ik, db, dec_seq), idx_k_pool, layer=l, dec_seq=dec_seq, ksel=ksel)
        o_dsa = _dsa_sample(
            page_table, _head_rows(dq, db, DSA_HEADS, DSA_KV_HEADS), sel_bias, _new_page(dk, db, dec_seq),
            _new_page(dv, db, dec_seq), dsa_k_pool, dsa_v_pool, layer=l, dec_seq=dec_seq)
        xs = _post(xs, _own_features(o_fox, FOX_HEADS, FOX_KV_HEADS).astype(BF16),
                   _own_features(o_dsa, DSA_HEADS, DSA_KV_HEADS).astype(BF16), *post_w, tm=tm_s)
        rows_s.append((fk, fv, lf, dk, dv, ik))

    tails = ((FOX_KV_HEADS, HEAD_DIM), (FOX_KV_HEADS, HEAD_DIM), (FOX_HEADS,), (DSA_KV_HEADS, HEAD_DIM),
             (DSA_KV_HEADS, HEAD_DIM), (IDX_DIM,))
    dtypes = tuple(c.dtype for c in (cache_fox_k, cache_fox_v, cache_fox_logf, cache_dsa_k, cache_dsa_v, cache_idx_k))
    new_p = tuple(jnp.concatenate([_token_major(r[k], tails[k]) for r in rows_p], axis=0).astype(dtypes[k])
                  for k in range(len(tails)))
    new_s = tuple(jnp.stack([r[k].reshape((db, dec_seq) + tails[k]) for r in rows_s], axis=0).astype(dtypes[k])
                  for k in range(len(tails)))
    return (xp.reshape(batch, seq_len, d), xs.reshape(db, dec_seq, d)) + new_p + new_s
```

```python
import functools

import jax
import jax.numpy as jnp
import numpy as np
from jax import lax
from jax.experimental import pallas as pl
from jax.experimental.pallas import tpu as pltpu

HEAD_DIM = 64
FOX_HEADS = 8
FOX_KV_HEADS = 4
DSA_HEADS = 8
DSA_KV_HEADS = 2
IDX_HEADS = 8
IDX_DIM = 64
TOPK_MAX = 256
ROPE_THETA = 500000.0
ROPE_FRACTION_DIV = 4
NORM_EPS = 1e-6
PAGE_SIZE = 128

FOX_Q = FOX_HEADS * HEAD_DIM
FOX_KV = FOX_KV_HEADS * HEAD_DIM
DSA_Q = DSA_HEADS * HEAD_DIM
DSA_KV = DSA_KV_HEADS * HEAD_DIM
IDX_Q = IDX_HEADS * IDX_DIM
FOX_GROUPS = FOX_HEADS // FOX_KV_HEADS
DSA_GROUPS = DSA_HEADS // DSA_KV_HEADS

LANES = 128
SUBLANES = 8
NEG = -(2.0 ** 100)
QK_SCALE = HEAD_DIM ** -0.5
VMEM_LIMIT = 56 * 1024 * 1024

TOKEN_TILE = 512
FOX_Q_TILE = 256
DSA_Q_TILE = 128
FOX_PAGES = 16
DSA_PAGES = 16
BISECT_PASSES = 16
SKIP_MARGIN = 110.0
NORM_SLACK = 1.05

F32 = jnp.float32
BF16 = jnp.bfloat16

_C_FOX = 0
_C_DSA = _C_FOX + FOX_Q + 2 * FOX_KV
_C_IQ = _C_DSA + DSA_Q + 2 * DSA_KV
_C_IK = _C_IQ + IDX_Q
_C_FF = _C_IK + LANES
_C_IW = _C_FF + LANES
_C_END = _C_IW + LANES


def _dot(a, b):
    return jnp.dot(a, b, preferred_element_type=F32)


def _dot_nt(a, b):
    return lax.dot_general(a, b, (((1,), (1,)), ((), ())), preferred_element_type=F32)


def _split3(x):
    hi = x.astype(BF16)
    r1 = x - hi.astype(F32)
    mid = r1.astype(BF16)
    lo = (r1 - mid.astype(F32)).astype(BF16)
    return hi, mid, lo


def _lane_iota(shape):
    return lax.broadcasted_iota(jnp.int32, shape, len(shape) - 1)


def _row_iota(shape):
    return lax.broadcasted_iota(jnp.int32, shape, len(shape) - 2)


def _rope(x, cos, sa, sb):
    w = x.shape[1]
    reps = w // LANES
    if reps > 1:
        cos = jnp.concatenate([cos] * reps, axis=1)
        sa = jnp.concatenate([sa] * reps, axis=1)
        sb = jnp.concatenate([sb] * reps, axis=1)
    up = pltpu.roll(x, w - 8, 1)
    dn = pltpu.roll(x, 8, 1)
    return x * cos + up * sa + dn * sb


def _rmsnorm(x, g):
    ms = jnp.mean(x * x, axis=-1, keepdims=True)
    return x * lax.rsqrt(ms + NORM_EPS) * g


def _proj_body(x_ref, g_ref, w_ref, b_ref, rope_ref, *refs, prompt, blocks_per_seq):
    fq_ref, dq_ref = refs[:2]
    tm = x_ref.shape[0]
    n = _rmsnorm(x_ref[...], g_ref[...]).astype(BF16)
    cos, sa, sb = rope_ref[0], rope_ref[1], rope_ref[2]

    def mm(lo, hi):
        return _dot(n, w_ref[:, lo:hi])

    fq = mm(_C_FOX, _C_FOX + FOX_Q)
    fq_ref[...] = (fq * QK_SCALE).astype(BF16)
    fk = mm(_C_FOX + FOX_Q, _C_FOX + FOX_Q + FOX_KV)
    fv = mm(_C_FOX + FOX_Q + FOX_KV, _C_DSA)
    dq = _rope(mm(_C_DSA, _C_DSA + DSA_Q), cos, sa, sb)
    dq_ref[...] = (dq * QK_SCALE).astype(BF16)
    dk = _rope(mm(_C_DSA + DSA_Q, _C_DSA + DSA_Q + DSA_KV), cos, sa, sb)
    dv = mm(_C_DSA + DSA_Q + DSA_KV, _C_IQ)
    iq = _rope(mm(_C_IQ, _C_IK), cos, sa, sb) * (IDX_DIM ** -0.5)
    ikg = _rope(mm(_C_IK, _C_FF), cos, sa, sb)
    ff = mm(_C_FF, _C_IW) + b_ref[...]
    lf = jnp.minimum(ff, 0.0) - jnp.log(1.0 + jnp.exp(-jnp.abs(ff)))
    iw = mm(_C_IW, _C_END) * (IDX_HEADS ** -0.5)

    if not prompt:
        fk_ref, fv_ref, lf_ref, dk_ref, dv_ref, ik_ref, iq_ref, iw_ref = refs[2:]
        fk_ref[...] = fk
        fv_ref[...] = fv
        lf_ref[...] = lf[:, :FOX_HEADS]
        dk_ref[...] = dk
        dv_ref[...] = dv
        ik_ref[...] = ikg[:, :IDX_DIM]
        iq_ref[...] = iq.astype(BF16)
        iw_ref[...] = iw[:, :IDX_HEADS]
        return
    (fkx_ref, fvx_ref, lfx_ref, dkx_ref, dvx_ref, ikx_ref,
     negc_ref, fkt_ref, fvd_ref, dkt_ref, dvd_ref, iqt_ref, iwt_ref, ikb_ref, fkn_ref, carry_ref) = refs[2:]
    low = _lane_iota((tm, LANES)) < HEAD_DIM

    fk_t, dk_t, ik_t = fk.T, dk.T, ikg.T
    fkx_ref[0] = fk_t
    fvx_ref[0] = fv.T
    lfx_ref[0] = lf.T[:FOX_HEADS, :]
    dkx_ref[0] = dk_t
    dvx_ref[0] = dv.T
    ikx_ref[0] = ik_t[:IDX_DIM, :]

    @pl.when(pl.program_id(0) % blocks_per_seq == 0)
    def _():
        carry_ref[...] = jnp.zeros_like(carry_ref)

    tri = (_row_iota((tm, tm)) >= _lane_iota((tm, tm))).astype(BF16)
    hi, mid, lo = _split3(lf)
    csum = _dot(tri, hi) + _dot(tri, mid) + _dot(tri, lo) + carry_ref[0:1, :]
    carry_ref[0:1, :] = csum[tm - 1:tm, :]
    negc_ref[0] = -(csum.T[:FOX_HEADS, :])

    grp = ((_row_iota((FOX_KV, LANES)) >> (HEAD_DIM.bit_length() - 1))
           == (_lane_iota((FOX_KV, LANES)) >> (FOX_GROUPS.bit_length() - 1)))
    fkn_ref[0] = jnp.max(_dot((fk * fk).astype(BF16), grp.astype(BF16)), axis=0, keepdims=True)

    ones_lane = jnp.where(_lane_iota((tm, LANES)) == HEAD_DIM, 1.0, 0.0)

    def value_tiles(pair):
        swap = pltpu.roll(pair, HEAD_DIM, 1)
        return jnp.where(low, pair, ones_lane).astype(BF16), jnp.where(low, swap, ones_lane).astype(BF16)

    fkt = fk_t.astype(BF16)
    for h in range(FOX_KV_HEADS):
        kt = fkt[h * HEAD_DIM:(h + 1) * HEAD_DIM, :]
        fkt_ref[h, 0] = jnp.concatenate([kt, kt], axis=0)
    for p in range(FOX_KV_HEADS // 2):
        fvd_ref[2 * p], fvd_ref[2 * p + 1] = value_tiles(fv[:, p * LANES:(p + 1) * LANES])

    dkt = dk_t.astype(BF16)
    dv_tiles = value_tiles(dv)
    for h in range(DSA_KV_HEADS):
        kt = dkt[h * HEAD_DIM:(h + 1) * HEAD_DIM, :]
        dkt_ref[h, 0] = jnp.concatenate([kt, kt, kt, kt], axis=0)
        dvd_ref[h] = dv_tiles[h]

    iqt_ref[...] = iq.T.astype(BF16)
    iwt_ref[...] = iw.T[:IDX_HEADS, :]
    ikb_ref[0] = ikg[:, :IDX_DIM].astype(BF16)


def _proj(x2d, g1, w1, bias, rope_tab, *, prompt, seq_len, tm):
    m, d = x2d.shape
    nblk = m // tm
    npos = rope_tab.shape[1] // tm
    row = lambda i: (i, 0)
    col = lambda i: (0, i)
    outs = [
        (jax.ShapeDtypeStruct((m, FOX_Q), BF16), pl.BlockSpec((tm, FOX_Q), row)),
        (jax.ShapeDtypeStruct((m, DSA_Q), BF16), pl.BlockSpec((tm, DSA_Q), row)),
    ]
    cache_widths = (FOX_KV, FOX_KV, FOX_HEADS, DSA_KV, DSA_KV, IDX_DIM)
    scratch = []
    if prompt:
        bps = seq_len // tm
        feat_major = lambda i: (i // bps, 0, i % bps)
        outs += [(jax.ShapeDtypeStruct((m // seq_len, w, seq_len), F32), pl.BlockSpec((1, w, tm), feat_major))
                 for w in cache_widths]
        outs += [
            (jax.ShapeDtypeStruct((nblk, FOX_HEADS, tm), F32), pl.BlockSpec((1, FOX_HEADS, tm), lambda i: (i, 0, 0))),
            (jax.ShapeDtypeStruct((FOX_KV_HEADS, nblk, 2 * HEAD_DIM, tm), BF16),
             pl.BlockSpec((FOX_KV_HEADS, 1, 2 * HEAD_DIM, tm), lambda i: (0, i, 0, 0))),
            (jax.ShapeDtypeStruct((FOX_KV_HEADS, m, 2 * HEAD_DIM), BF16),
             pl.BlockSpec((FOX_KV_HEADS, tm, 2 * HEAD_DIM), lambda i: (0, i, 0))),
            (jax.ShapeDtypeStruct((DSA_KV_HEADS, nblk, 4 * HEAD_DIM, tm), BF16),
             pl.BlockSpec((DSA_KV_HEADS, 1, 4 * HEAD_DIM, tm), lambda i: (0, i, 0, 0))),
            (jax.ShapeDtypeStruct((DSA_KV_HEADS, m, 2 * HEAD_DIM), BF16),
             pl.BlockSpec((DSA_KV_HEADS, tm, 2 * HEAD_DIM), lambda i: (0, i, 0))),
            (jax.ShapeDtypeStruct((IDX_Q, m), BF16), pl.BlockSpec((IDX_Q, tm), col)),
            (jax.ShapeDtypeStruct((IDX_HEADS, m), F32), pl.BlockSpec((IDX_HEADS, tm), col)),
            (jax.ShapeDtypeStruct((nblk, tm, IDX_DIM), BF16), pl.BlockSpec((1, tm, IDX_DIM), lambda i: (i, 0, 0))),
            (jax.ShapeDtypeStruct((nblk, 1, LANES), F32), pl.BlockSpec((1, 1, LANES), lambda i: (i, 0, 0))),
        ]
        scratch = [pltpu.VMEM((SUBLANES, LANES), F32)]
    else:
        outs += [(jax.ShapeDtypeStruct((m, w), F32), pl.BlockSpec((tm, w), row)) for w in cache_widths]
        outs += [
            (jax.ShapeDtypeStruct((m, IDX_Q), BF16), pl.BlockSpec((tm, IDX_Q), row)),
            (jax.ShapeDtypeStruct((m, IDX_HEADS), F32), pl.BlockSpec((tm, IDX_HEADS), row)),
        ]
    const = lambda i: (0, 0)
    return pl.pallas_call(
        functools.partial(_proj_body, prompt=prompt, blocks_per_seq=seq_len // tm),
        grid=(nblk,),
        in_specs=[
            pl.BlockSpec((tm, d), row),
            pl.BlockSpec((1, d), const),
            pl.BlockSpec((d, _C_END), const, pipeline_mode=pl.Buffered(1)),
            pl.BlockSpec((1, LANES), const),
            pl.BlockSpec((3, tm, LANES), lambda i: (0, i % npos, 0)),
        ],
        out_specs=[o[1] for o in outs],
        out_shape=[o[0] for o in outs],
        scratch_shapes=scratch,
        compiler_params=pltpu.CompilerParams(dimension_semantics=("arbitrary",), vmem_limit_bytes=VMEM_LIMIT),
        name="proj_prompt" if prompt else "proj_sample",
    )(x2d, g1, w1, bias, rope_tab)


def _softmax_step(s, m, acc, v):
    m_new = jnp.maximum(m, jnp.max(s, axis=-1, keepdims=True))
    acc = jnp.exp(m - m_new) * acc + _dot(jnp.exp(s - m_new).astype(BF16), v)
    return m_new, acc


def _softmax_init(rows):
    return jnp.full((rows, 1), NEG, F32), jnp.zeros((rows, LANES), F32)


def _head_outputs(acc, tq, groups):
    low = _lane_iota((tq, LANES)) < HEAD_DIM
    pairs = []
    for g in range(0, groups, 2):
        even, odd = (acc[k * tq:(k + 1) * tq] for k in (g, g + 1))
        even = even / even[:, HEAD_DIM:HEAD_DIM + 1]
        odd = odd / odd[:, HEAD_DIM:HEAD_DIM + 1]
        pairs.append(jnp.where(low, even, pltpu.roll(odd, HEAD_DIM, 1)))
    return pairs[0] if len(pairs) == 1 else jnp.concatenate(pairs, axis=1)


def _stack_heads(q, tq, groups):
    lane = _lane_iota((tq, groups * HEAD_DIM))
    zero = jnp.zeros_like(q)
    return jnp.concatenate(
        [jnp.where((lane >= g * HEAD_DIM) & (lane < (g + 1) * HEAD_DIM), q, zero) for g in range(groups)], axis=0)


def _fox_skip_count(q_ref, kn_ref, first_ref, last_ref, nfull):
    qf = q_ref[...].astype(F32)
    head_of = ((_row_iota((FOX_Q, LANES)) >> (HEAD_DIM.bit_length() - 1)) == _lane_iota((FOX_Q, LANES))).astype(BF16)
    qn2 = jnp.max(_dot((qf * qf).astype(BF16), head_of), axis=0, keepdims=True)
    kn2 = jnp.max(kn_ref[:, 0, :], axis=0, keepdims=True)
    bound = NORM_SLACK * jnp.sqrt(qn2 * kn2)
    thr = jnp.where(_lane_iota((1, LANES)) < FOX_HEADS, 2.0 * bound + SKIP_MARGIN, -1.0)
    gap = first_ref[0] - last_ref[:, 0, :]
    skip = jnp.min(jnp.where(gap > thr, 1.0, 0.0), axis=1, keepdims=True)
    blk = _row_iota(skip.shape)
    return jnp.sum(jnp.where(blk < nfull, skip, 0.0)).astype(jnp.int32)


def _fox_prompt_body(q_ref, kt_ref, v_ref, negc_ref, kn_ref, first_ref, last_ref, o_ref, *, tq, tk):
    i = pl.program_id(1)
    nfull = (i * tq) // tk
    nskip = _fox_skip_count(q_ref, kn_ref, first_ref, last_ref, nfull)
    g = FOX_GROUPS
    w = g * HEAD_DIM
    qpos = i * tq + (_row_iota((g * tq, tk)) & (tq - 1))
    kloc = _lane_iota((g * tq, tk))
    qs = [_stack_heads(q_ref[:, h * w:(h + 1) * w], tq, g) for h in range(FOX_KV_HEADS)]

    def step(j, carry, masked):
        nb = negc_ref[j]
        off = pl.multiple_of(j * tk, tk)
        out = []
        for h in range(FOX_KV_HEADS):
            bias = jnp.concatenate(
                [jnp.broadcast_to(nb[h * g + k:h * g + k + 1, :], (tq, tk)) for k in range(g)], axis=0)
            s = _dot(qs[h], kt_ref[h, j]) + bias
            if masked:
                s = jnp.where(j * tk + kloc <= qpos, s, NEG)
            out.append(_softmax_step(s, *carry[h], v_ref[h, pl.ds(off, tk), :]))
        return tuple(out)

    init = tuple(_softmax_init(g * tq) for _ in range(FOX_KV_HEADS))
    carry = lax.fori_loop(nskip, nfull, lambda j, c: step(j, c, False), init)
    carry = step(nfull, carry, True)
    for h in range(FOX_KV_HEADS):
        o_ref[:, h * w:(h + 1) * w] = _head_outputs(carry[h][1], tq, g).astype(BF16)


def _fox_prompt(fq, fkt, fvd, negc, fkn, *, batch, seq_len, tq, tk):
    m = fq.shape[0]
    nq = seq_len // tq
    nkb = seq_len // tk
    lane_pad = lambda a: jnp.pad(a, ((0, 0), (0, LANES - FOX_HEADS)))[:, None, :]
    a_last = lane_pad(negc[:, :, tk - 1])
    a_first = lane_pad(jnp.swapaxes(negc[:, :, ::tq], 1, 2).reshape(batch * nq, FOX_HEADS))
    return pl.pallas_call(
        functools.partial(_fox_prompt_body, tq=tq, tk=tk),
        grid=(batch, nq),
        in_specs=[
            pl.BlockSpec((tq, FOX_Q), lambda b, i: (b * nq + i, 0)),
            pl.BlockSpec((FOX_KV_HEADS, nkb, 2 * HEAD_DIM, tk), lambda b, i: (0, b, 0, 0), pipeline_mode=pl.Buffered(1)),
            pl.BlockSpec((FOX_KV_HEADS, seq_len, 2 * HEAD_DIM), lambda b, i: (0, b, 0), pipeline_mode=pl.Buffered(1)),
            pl.BlockSpec((nkb, FOX_HEADS, tk), lambda b, i: (b, 0, 0), pipeline_mode=pl.Buffered(1)),
            pl.BlockSpec((nkb, 1, LANES), lambda b, i: (b, 0, 0)),
            pl.BlockSpec((1, 1, LANES), lambda b, i: (b * nq + i, 0, 0)),
            pl.BlockSpec((nkb, 1, LANES), lambda b, i: (b, 0, 0)),
        ],
        out_specs=pl.BlockSpec((tq, FOX_Q), lambda b, i: (b * nq + i, 0)),
        out_shape=jax.ShapeDtypeStruct((m, FOX_Q), BF16),
        compiler_params=pltpu.CompilerParams(dimension_semantics=("arbitrary", "arbitrary"), vmem_limit_bytes=VMEM_LIMIT),
        name="fox_prompt",
    )(fq, fkt, fvd, negc, fkn, a_first, a_last)


def _topk_threshold(reduce_fn, idx_of, nvalid, ksel, n_idx):
    kf = jnp.float32(ksel)
    inf = jnp.float32(jnp.inf)
    one, zero = jnp.float32(1.0), jnp.float32(0.0)
    shape = nvalid.shape

    def count_ge(t):
        return reduce_fn(lambda c, blk: jnp.where(blk >= t, one, zero), "sum")

    row_max = reduce_fn(lambda c, blk: blk, "max")
    row_min = reduce_fn(lambda c, blk: jnp.where(blk > -inf, blk, inf), "min")
    all_sel = jnp.where(nvalid <= kf, one, zero)

    def bisect(_, st):
        lo, clo, hi, chi, done = st
        t = jnp.where(hi == inf, row_max, 0.5 * (lo + hi))
        cnt = count_ge(t)
        upd = (done < 0.5) & (t > lo) & (t < hi)
        ge = cnt >= kf
        lo = jnp.where(upd & ge, t, lo)
        clo = jnp.where(upd & ge, cnt, clo)
        hi = jnp.where(upd & (cnt < kf), t, hi)
        chi = jnp.where(upd & (cnt < kf), cnt, chi)
        return lo, clo, hi, chi, jnp.where(clo == kf, one, done)

    st = (row_min, nvalid, jnp.full(shape, inf, F32), jnp.zeros(shape, F32), all_sel)
    st = lax.fori_loop(0, BISECT_PASSES, bisect, st)

    def snap(st):
        lo, clo, hi, chi, done = st
        v = reduce_fn(lambda c, blk: jnp.where(blk < hi, blk, -inf), "max")
        cv = count_ge(v)
        fin = (done < 0.5) & (cv >= kf)
        go = (done < 0.5) & (cv < kf)
        lo = jnp.where(fin, v, lo)
        clo = jnp.where(fin, cv, clo)
        hi = jnp.where(go, v, hi)
        chi = jnp.where(go, cv, chi)
        return lo, clo, hi, chi, jnp.where(fin, one, done)

    lo, clo, hi, chi, done = lax.while_loop(lambda st: jnp.min(st[4]) < 0.5, snap, st)

    thr = jnp.where(all_sel > 0.5, -inf, lo)
    need_cut = jnp.where((all_sel < 0.5) & (clo > kf), one, zero)
    need = kf - chi
    big = jnp.float32(n_idx)

    def cut_search():
        def body(_, st):
            jlo, jhi = st
            mid = jnp.floor(0.5 * (jlo + jhi))
            cnt = reduce_fn(lambda c, blk: jnp.where((blk == thr) & (idx_of(c) <= mid), one, zero), "sum")
            ok = cnt >= need
            return jnp.where(ok, jlo, mid), jnp.where(ok, mid, jhi)

        npass = int(np.ceil(np.log2(n_idx))) + 1
        _, jhi = lax.fori_loop(0, npass, body, (jnp.full(shape, -1.0, F32), jnp.full(shape, n_idx - 1.0, F32)))
        return jhi

    jcut = lax.cond(jnp.max(need_cut) > 0.5, cut_search, lambda: jnp.full(shape, -1.0, F32))
    jcut = jnp.where(need_cut > 0.5, jcut, jnp.where(all_sel > 0.5, -1.0, big))
    return thr, jcut


_REDUCE_OPS = {"sum": (jnp.add, 0.0, jnp.sum), "max": (jnp.maximum, -np.inf, jnp.max), "min": (jnp.minimum, np.inf, jnp.min)}


def _tree(parts, op):
    while len(parts) > 1:
        parts = [op(parts[k], parts[k + 1]) if k + 1 < len(parts) else parts[k] for k in range(0, len(parts), 2)]
    return parts[0]


def _row_major_reducer(s_ref, nch):
    _, rows, ch = s_ref.shape

    def reduce_fn(elem_fn, kind):
        op, init, lane_reduce = _REDUCE_OPS[kind]

        def body(c, acc):
            x = elem_fn(c, s_ref[c])
            return op(acc, _tree([x[:, k * LANES:(k + 1) * LANES] for k in range(ch // LANES)], op))

        acc = jnp.full((rows, LANES), init, F32)
        if isinstance(nch, int):
            for c in range(nch):
                acc = body(c, acc)
        else:
            acc = lax.fori_loop(0, nch, body, acc)
        return lane_reduce(acc, axis=1, keepdims=True)

    return reduce_fn


def _lane_major_reducer(s_ref, nch):
    _, ch, _ = s_ref.shape

    def reduce_fn(elem_fn, kind):
        op, init, sub_reduce = _REDUCE_OPS[kind]

        def body(c, acc):
            x = elem_fn(c, s_ref[c])
            return op(acc, _tree([x[k * SUBLANES:(k + 1) * SUBLANES, :] for k in range(ch // SUBLANES)], op))

        acc = lax.fori_loop(0, nch, body, jnp.full((SUBLANES, LANES), init, F32))
        return sub_reduce(acc, axis=0, keepdims=True)

    return reduce_fn


def _dsa_prompt_body(iqt_ref, iwt_ref, dq_ref, ikb_ref, dkt_ref, dv_ref, o_ref, s_ref, bias_ref, *, tq, ch, ksel):
    i = pl.program_id(1)
    nch = (i * tq + tq - 1) // ch + 1
    g = DSA_GROUPS
    w = g * HEAD_DIM

    iqt = iqt_ref[...]
    rhs = [jnp.concatenate([iqt[(2 * p) * IDX_DIM:(2 * p + 1) * IDX_DIM, :], iqt[(2 * p + 1) * IDX_DIM:(2 * p + 2) * IDX_DIM, :]],
                           axis=1) for p in range(IDX_HEADS // 2)]
    iwt = iwt_ref[...]
    qpos = i * tq + _lane_iota((ch, tq))
    kloc = _row_iota((ch, tq))

    def score_chunk(c, _):
        keys = ikb_ref[c]
        acc = None
        for p in range(IDX_HEADS // 2):
            d = jnp.maximum(_dot(keys, rhs[p]), 0.0)
            term = d[:, :tq] * iwt[2 * p:2 * p + 1, :] + d[:, tq:] * iwt[2 * p + 1:2 * p + 2, :]
            acc = term if acc is None else acc + term
        s_ref[c] = jnp.where(c * ch + kloc <= qpos, acc, -jnp.inf)
        return 0

    lax.fori_loop(0, nch, score_chunk, 0)

    nvalid = (i * tq + _lane_iota((1, tq)) + 1).astype(F32)
    idx_of = lambda c: (c * ch + kloc).astype(F32)
    thr, jcut = _topk_threshold(_lane_major_reducer(s_ref, nch), idx_of, nvalid, ksel, s_ref.shape[0] * ch)

    def write_bias(c, _):
        blk = s_ref[c]
        sel = (blk > thr) | ((blk == thr) & (idx_of(c) <= jcut))
        bias_ref[c] = jnp.where(sel, 0.0, NEG).T
        return 0

    lax.fori_loop(0, nch, write_bias, 0)

    qs = [_stack_heads(dq_ref[:, h * w:(h + 1) * w], tq, g) for h in range(DSA_KV_HEADS)]

    def step(c, carry):
        b = bias_ref[c]
        bias = jnp.concatenate([b] * g, axis=0)
        off = pl.multiple_of(c * ch, ch)
        return tuple(_softmax_step(_dot(qs[h], dkt_ref[h, c]) + bias, *carry[h], dv_ref[h, pl.ds(off, ch), :])
                     for h in range(DSA_KV_HEADS))

    carry = lax.fori_loop(0, nch, step, tuple(_softmax_init(g * tq) for _ in range(DSA_KV_HEADS)))
    for h in range(DSA_KV_HEADS):
        o_ref[:, h * w:(h + 1) * w] = _head_outputs(carry[h][1], tq, g).astype(BF16)


def _dsa_prompt(iqt, iwt, dq, ikb, dkt, dvd, *, batch, seq_len, tq, ch):
    m = dq.shape[0]
    nq = seq_len // tq
    nkb = seq_len // ch
    ksel = min(TOPK_MAX, seq_len // 4)
    row = lambda b, i: (b * nq + i, 0)
    col = lambda b, i: (0, b * nq + i)
    return pl.pallas_call(
        functools.partial(_dsa_prompt_body, tq=tq, ch=ch, ksel=ksel),
        grid=(batch, nq),
        in_specs=[
            pl.BlockSpec((IDX_Q, tq), col),
            pl.BlockSpec((IDX_HEADS, tq), col),
            pl.BlockSpec((tq, DSA_Q), row),
            pl.BlockSpec((nkb, ch, IDX_DIM), lambda b, i: (b, 0, 0), pipeline_mode=pl.Buffered(1)),
            pl.BlockSpec((DSA_KV_HEADS, nkb, 4 * HEAD_DIM, ch), lambda b, i: (0, b, 0, 0), pipeline_mode=pl.Buffered(1)),
            pl.BlockSpec((DSA_KV_HEADS, seq_len, 2 * HEAD_DIM), lambda b, i: (0, b, 0), pipeline_mode=pl.Buffered(1)),
        ],
        out_specs=pl.BlockSpec((tq, DSA_Q), row),
        out_shape=jax.ShapeDtypeStruct((m, DSA_Q), BF16),
        scratch_shapes=[pltpu.VMEM((nkb, ch, tq), F32), pltpu.VMEM((nkb, tq, ch), F32)],
        compiler_params=pltpu.CompilerParams(dimension_semantics=("arbitrary", "arbitrary"), vmem_limit_bytes=VMEM_LIMIT),
        name="dsa_prompt",
    )(iqt, iwt, dq, ikb, dkt, dvd)


def _page_ring(pt_ref, pools, bufs, sem, layer, pages):
    ng = pl.num_programs(1)
    step = pl.program_id(0) * ng + pl.program_id(1)
    last = pl.num_programs(0) * ng - 1

    def copies(s):
        b, j, slot = lax.div(s, ng), lax.rem(s, ng), lax.rem(s, 2)
        return [pltpu.make_async_copy(pool.at[layer, pt_ref[b, j * pages + k]], buf.at[slot, k], sem.at[slot, a])
                for a, (pool, buf) in enumerate(zip(pools, bufs)) for k in range(pages)]

    @pl.when(step == 0)
    def _():
        for c in copies(step):
            c.start()

    @pl.when(step < last)
    def _():
        for c in copies(step + 1):
            c.start()

    for c in copies(step):
        c.wait()
    return lax.rem(step, 2)


def _page_ring_scratch(features, pages):
    return [pltpu.VMEM((2, pages, f, PAGE_SIZE), F32) for f in features] + [pltpu.SemaphoreType.DMA((2, len(features)))]


def _page_tiles(buf, slot):
    return jnp.concatenate([buf[slot, k] for k in range(buf.shape[1])], axis=1).astype(BF16)


def _sample_softmax_step(s, vt, m_ref, l_ref, acc_ref):
    m = m_ref[...]
    m_new = jnp.maximum(m, jnp.max(s, axis=-1, keepdims=True))
    alpha = jnp.exp(m - m_new)
    p = jnp.exp(s - m_new)
    m_ref[...] = m_new
    l_ref[...] = alpha * l_ref[...] + jnp.sum(p, axis=-1, keepdims=True)
    acc_ref[...] = alpha * acc_ref[...] + _dot_nt(p.astype(BF16), vt)


def _fox_sample_body(pt_ref, q_ref, kn_ref, vn_ref, lfn_ref, k_hbm, v_hbm, lf_hbm, o_ref, m_ref, l_ref, acc_ref, pre_ref,
                     kbuf, vbuf, lbuf, sem, *, layer, dec_seq):
    np_ = FOX_PAGES
    j = pl.program_id(1)
    rows = dec_seq * FOX_HEADS
    slot = _page_ring(pt_ref, (k_hbm, v_hbm, lf_hbm), (kbuf, vbuf, lbuf), sem, layer, np_)

    @pl.when(j == 0)
    def _():
        m_ref[...] = jnp.full_like(m_ref, NEG)
        l_ref[...] = jnp.zeros_like(l_ref)
        acc_ref[...] = jnp.zeros_like(acc_ref)
        pre_ref[...] = jnp.zeros_like(pre_ref)

    triu = (_row_iota((PAGE_SIZE, PAGE_SIZE)) <= _lane_iota((PAGE_SIZE, PAGE_SIZE))).astype(BF16)

    def cum_keys(lf):
        r = lf.shape[0]
        parts = _dot(jnp.concatenate(_split3(lf), axis=0), triu)
        return parts[0:r] + parts[r:2 * r] + parts[2 * r:3 * r]

    q = q_ref[0]
    nrow = np_ * FOX_HEADS
    within = cum_keys(jnp.concatenate([lbuf[slot, k] for k in range(np_)], axis=0))
    r, c = _row_iota((nrow, nrow)), _lane_iota((nrow, nrow))
    earlier = (((r ^ c) & (FOX_HEADS - 1)) == 0) & (c < r)
    totals = jnp.broadcast_to(within[:, PAGE_SIZE - 1:PAGE_SIZE], (nrow, PAGE_SIZE))
    carried = _dot(earlier.astype(BF16), jnp.concatenate(_split3(totals), axis=1))
    carried = carried[:, 0:PAGE_SIZE] + carried[:, PAGE_SIZE:2 * PAGE_SIZE] + carried[:, 2 * PAGE_SIZE:3 * PAGE_SIZE]
    cum = within + carried + jnp.concatenate([pre_ref[...]] * np_, axis=0)
    pre_ref[...] = jnp.broadcast_to(cum[nrow - FOX_HEADS:nrow, PAGE_SIZE - 1:PAGE_SIZE], pre_ref.shape)
    c_all = jnp.concatenate([cum[k * FOX_HEADS:(k + 1) * FOX_HEADS, :] for k in range(np_)], axis=1)
    s = _dot(q, _page_tiles(kbuf, slot)) - jnp.concatenate([c_all] * dec_seq, axis=0)
    _sample_softmax_step(s, _page_tiles(vbuf, slot), m_ref, l_ref, acc_ref)

    @pl.when(j == pl.num_programs(1) - 1)
    def _():
        c = cum_keys(lfn_ref[0]) + pre_ref[...]
        s = _dot(q, kn_ref[0].astype(BF16)) - jnp.concatenate([c] * dec_seq, axis=0)
        tok = _row_iota((rows, PAGE_SIZE)) >> (FOX_HEADS.bit_length() - 1)
        s = jnp.where(_lane_iota((rows, PAGE_SIZE)) <= tok, s, NEG)
        _sample_softmax_step(s, vn_ref[0].astype(BF16), m_ref, l_ref, acc_ref)
        o_ref[0] = acc_ref[...] / l_ref[...]


def _fox_sample(page_table, q_rows, k_new, v_new, lf_new, k_pool, v_pool, lf_pool, *, layer, dec_seq):
    db, n_pages = page_table.shape
    ng = n_pages // FOX_PAGES
    rows = dec_seq * FOX_HEADS
    per_b = lambda b, j, pt: (b, 0, 0)
    grid_spec = pltpu.PrefetchScalarGridSpec(
        num_scalar_prefetch=1,
        grid=(db, ng),
        in_specs=[
            pl.BlockSpec((1, rows, FOX_KV), per_b),
            pl.BlockSpec((1, FOX_KV, PAGE_SIZE), per_b),
            pl.BlockSpec((1, FOX_KV, PAGE_SIZE), per_b),
            pl.BlockSpec((1, FOX_HEADS, PAGE_SIZE), per_b),
        ] + [pl.BlockSpec(memory_space=pl.ANY)] * 3,
        out_specs=pl.BlockSpec((1, rows, FOX_KV), per_b),
        scratch_shapes=[pltpu.VMEM((rows, 1), F32), pltpu.VMEM((rows, 1), F32), pltpu.VMEM((rows, FOX_KV), F32),
                        pltpu.VMEM((FOX_HEADS, PAGE_SIZE), F32)]
        + _page_ring_scratch((FOX_KV, FOX_KV, FOX_HEADS), FOX_PAGES),
    )
    return pl.pallas_call(
        functools.partial(_fox_sample_body, layer=layer, dec_seq=dec_seq),
        grid_spec=grid_spec,
        out_shape=jax.ShapeDtypeStruct((db, rows, FOX_KV), F32),
        compiler_params=pltpu.CompilerParams(dimension_semantics=("arbitrary", "arbitrary"), vmem_limit_bytes=VMEM_LIMIT),
        name="fox_sample",
    )(page_table, q_rows, k_new, v_new, lf_new, k_pool, v_pool, lf_pool)


def _idx_sample_body(pt_ref, q_ref, w_ref, kn_ref, k_hbm, bias_ref, s_ref, kbuf, sem, *, layer, dec_seq, past, ksel):
    np_ = DSA_PAGES
    j = pl.program_id(1)
    slot = _page_ring(pt_ref, (k_hbm,), (kbuf,), sem, layer, np_)
    ng = pl.num_programs(1)
    ch = np_ * PAGE_SIZE
    q = q_ref[0]
    w = w_ref[0]

    def scores(keys_t):
        d = jnp.maximum(_dot(q, keys_t), 0.0) * w
        shape = (SUBLANES, d.shape[1])
        tok = _row_iota(shape)
        out = jnp.full(shape, -jnp.inf, F32)
        for t in range(dec_seq):
            per_tok = jnp.sum(d[t * IDX_HEADS:(t + 1) * IDX_HEADS], axis=0, keepdims=True)
            out = jnp.where(tok == t, jnp.broadcast_to(per_tok, shape), out)
        return out

    s_ref[j] = scores(_page_tiles(kbuf, slot))

    @pl.when(j == ng - 1)
    def _():
        sn = scores(kn_ref[0].astype(BF16))
        sn = jnp.where(_lane_iota((SUBLANES, PAGE_SIZE)) <= _row_iota((SUBLANES, PAGE_SIZE)), sn, -jnp.inf)
        s_ref[ng] = jnp.concatenate([sn, jnp.full((SUBLANES, ch - PAGE_SIZE), -jnp.inf, F32)], axis=1)
        tok = _row_iota((SUBLANES, 1))
        nvalid = jnp.where(tok < dec_seq, (past + 1 + tok).astype(F32), 0.0)
        idx_of = lambda c: (c * ch + _lane_iota((SUBLANES, ch))).astype(F32)
        thr, jcut = _topk_threshold(_row_major_reducer(s_ref, ng + 1), idx_of, nvalid, ksel, s_ref.shape[0] * ch)
        for c in range(s_ref.shape[0]):
            blk = s_ref[c]
            sel = (blk > thr) | ((blk == thr) & (idx_of(c) <= jcut))
            bias_ref[0, c] = jnp.where(sel, 0.0, NEG)


def _idx_sample(page_table, iq_rows, iw_rows, ik_new, ik_pool, *, layer, dec_seq, ksel):
    db, n_pages = page_table.shape
    ng = n_pages // DSA_PAGES
    ch = DSA_PAGES * PAGE_SIZE
    rows = dec_seq * IDX_HEADS
    per_b = lambda b, j, pt: (b, 0, 0)
    grid_spec = pltpu.PrefetchScalarGridSpec(
        num_scalar_prefetch=1,
        grid=(db, ng),
        in_specs=[
            pl.BlockSpec((1, rows, IDX_DIM), per_b),
            pl.BlockSpec((1, rows, 1), per_b),
            pl.BlockSpec((1, IDX_DIM, PAGE_SIZE), per_b),
            pl.BlockSpec(memory_space=pl.ANY),
        ],
        out_specs=pl.BlockSpec((1, ng + 1, SUBLANES, ch), lambda b, j, pt: (b, 0, 0, 0)),
        scratch_shapes=[pltpu.VMEM((ng + 1, SUBLANES, ch), F32)] + _page_ring_scratch((IDX_DIM,), DSA_PAGES),
    )
    return pl.pallas_call(
        functools.partial(_idx_sample_body, layer=layer, dec_seq=dec_seq, past=n_pages * PAGE_SIZE, ksel=ksel),
        grid_spec=grid_spec,
        out_shape=jax.ShapeDtypeStruct((db, ng + 1, SUBLANES, ch), F32),
        compiler_params=pltpu.CompilerParams(dimension_semantics=("arbitrary", "arbitrary"), vmem_limit_bytes=VMEM_LIMIT),
        name="idx_sample",
    )(page_table, iq_rows, iw_rows, ik_new, ik_pool)


def _dsa_sample_body(pt_ref, q_ref, bias_ref, kn_ref, vn_ref, k_hbm, v_hbm, o_ref, m_ref, l_ref, acc_ref, kbuf, vbuf, sem,
                     *, layer, dec_seq):
    np_ = DSA_PAGES
    j = pl.program_id(1)
    ng = pl.num_programs(1)
    slot = _page_ring(pt_ref, (k_hbm, v_hbm), (kbuf, vbuf), sem, layer, np_)
    rows = dec_seq * DSA_HEADS

    @pl.when(j == 0)
    def _():
        m_ref[...] = jnp.full_like(m_ref, NEG)
        l_ref[...] = jnp.zeros_like(l_ref)
        acc_ref[...] = jnp.zeros_like(acc_ref)

    def mask_rows(b):
        shape = (rows, b.shape[1])
        tok = _row_iota(shape) >> (DSA_HEADS.bit_length() - 1)
        out = jnp.broadcast_to(b[0:1, :], shape)
        for t in range(1, dec_seq):
            out = jnp.where(tok == t, jnp.broadcast_to(b[t:t + 1, :], shape), out)
        return out

    q = q_ref[0]
    s = _dot(q, _page_tiles(kbuf, slot)) + mask_rows(bias_ref[0, j])
    _sample_softmax_step(s, _page_tiles(vbuf, slot), m_ref, l_ref, acc_ref)

    @pl.when(j == ng - 1)
    def _():
        s = _dot(q, kn_ref[0].astype(BF16)) + mask_rows(bias_ref[0, ng, :, 0:PAGE_SIZE])
        _sample_softmax_step(s, vn_ref[0].astype(BF16), m_ref, l_ref, acc_ref)
        o_ref[0] = acc_ref[...] / l_ref[...]


def _dsa_sample(page_table, q_rows, bias, k_new, v_new, k_pool, v_pool, *, layer, dec_seq):
    db, n_pages = page_table.shape
    ng = n_pages // DSA_PAGES
    ch = DSA_PAGES * PAGE_SIZE
    rows = dec_seq * DSA_HEADS
    per_b = lambda b, j, pt: (b, 0, 0)
    grid_spec = pltpu.PrefetchScalarGridSpec(
        num_scalar_prefetch=1,
        grid=(db, ng),
        in_specs=[
            pl.BlockSpec((1, rows, DSA_KV), per_b),
            pl.BlockSpec((1, ng + 1, SUBLANES, ch), lambda b, j, pt: (b, 0, 0, 0)),
            pl.BlockSpec((1, DSA_KV, PAGE_SIZE), per_b),
            pl.BlockSpec((1, DSA_KV, PAGE_SIZE), per_b),
        ] + [pl.BlockSpec(memory_space=pl.ANY)] * 2,
        out_specs=pl.BlockSpec((1, rows, DSA_KV), per_b),
        scratch_shapes=[pltpu.VMEM((rows, 1), F32), pltpu.VMEM((rows, 1), F32), pltpu.VMEM((rows, DSA_KV), F32)]
        + _page_ring_scratch((DSA_KV, DSA_KV), DSA_PAGES),
    )
    return pl.pallas_call(
        functools.partial(_dsa_sample_body, layer=layer, dec_seq=dec_seq),
        grid_spec=grid_spec,
        out_shape=jax.ShapeDtypeStruct((db, rows, DSA_KV), F32),
        compiler_params=pltpu.CompilerParams(dimension_semantics=("arbitrary", "arbitrary"), vmem_limit_bytes=VMEM_LIMIT),
        name="dsa_sample",
    )(page_table, q_rows, bias, k_new, v_new, k_pool, v_pool)


_FF_CHUNK = 1024


def _post_body(x_ref, of_ref, od_ref, g1_ref, wg_ref, wof_ref, wod_ref, wout_ref, g2_ref, wup_ref, wdn_ref, gf_ref, y_ref):
    x = x_ref[...]
    d = x.shape[1]
    n = _rmsnorm(x, g1_ref[...]).astype(BF16)
    gate_a = jax.nn.sigmoid(_dot(n, wg_ref[:, 0:d]))
    gate_b = jax.nn.sigmoid(_dot(n, wg_ref[:, d:2 * d]))
    mix = gate_a * _dot(of_ref[...], wof_ref[...]) + gate_b * _dot(od_ref[...], wod_ref[...])
    h = x + _dot(mix.astype(BF16), wout_ref[...])
    n2 = _rmsnorm(h, g2_ref[...]).astype(BF16)
    y = h
    for c in range(wup_ref.shape[1] // _FF_CHUNK):
        u = jnp.maximum(_dot(n2, wup_ref[:, c * _FF_CHUNK:(c + 1) * _FF_CHUNK]), 0.0)
        y = y + _dot((u * u).astype(BF16), wdn_ref[c * _FF_CHUNK:(c + 1) * _FF_CHUNK, :])
    y_ref[...] = _rmsnorm(y, gf_ref[...])


def _post(x2d, o_fox, o_dsa, g1, wg, wof, wod, wout, g2, wup, wdn, gf, *, tm):
    m, d = x2d.shape
    row = lambda i: (i, 0)
    const = lambda i: (0, 0)

    def whole(a):
        return pl.BlockSpec(a.shape, const, pipeline_mode=pl.Buffered(1))

    return pl.pallas_call(
        _post_body,
        grid=(m // tm,),
        in_specs=[pl.BlockSpec((tm, d), row), pl.BlockSpec((tm, FOX_Q), row), pl.BlockSpec((tm, DSA_Q), row),
                  whole(g1), whole(wg), whole(wof), whole(wod), whole(wout), whole(g2), whole(wup), whole(wdn), whole(gf)],
        out_specs=pl.BlockSpec((tm, d), row),
        out_shape=jax.ShapeDtypeStruct((m, d), F32),
        compiler_params=pltpu.CompilerParams(dimension_semantics=("arbitrary",), vmem_limit_bytes=VMEM_LIMIT),
        name="post",
    )(x2d, o_fox, o_dsa, g1, wg, wof, wod, wout, g2, wup, wdn, gf)


def _rope_tables(pos):
    r = HEAD_DIM // ROPE_FRACTION_DIV
    half = r // 2
    inv_freq = jnp.float32(ROPE_THETA) ** (-jnp.arange(half, dtype=F32) * (2.0 / r))
    ang = pos.astype(F32)[:, None] * inv_freq[None, :]
    cos, sin = jnp.cos(ang), jnp.sin(ang)
    p = pos.shape[0]
    one = jnp.ones((p, HEAD_DIM - r), F32)
    zero_r = jnp.zeros((p, HEAD_DIM - r), F32)
    zero_h = jnp.zeros((p, half), F32)
    c = jnp.concatenate([cos, cos, one], axis=1)
    sa = jnp.concatenate([-sin, zero_h, zero_r], axis=1)
    sb = jnp.concatenate([zero_h, sin, zero_r], axis=1)
    return jnp.stack([jnp.tile(t, (1, LANES // HEAD_DIM)) for t in (c, sa, sb)], axis=0)


def _pad_cols(w, width):
    return jnp.pad(w, ((0, 0), (0, width - w.shape[1])))


def _repack_w_in(w_in, d_model):
    sizes = (FOX_Q, FOX_KV, FOX_KV, FOX_HEADS, DSA_Q, DSA_KV, DSA_KV, IDX_Q, IDX_DIM, IDX_HEADS, d_model, d_model)
    cuts = [int(c) for c in np.cumsum(sizes)[:-1]]
    fq, fk, fv, ff, dq, dk, dv, iq, ik, iw, ga, gb = jnp.split(w_in, cuts, axis=-1)
    w1 = jnp.concatenate([fq, fk, fv, dq, dk, dv, iq, _pad_cols(ik, LANES), _pad_cols(ff, LANES), _pad_cols(iw, LANES)], axis=1)
    return w1.astype(BF16), jnp.concatenate([ga, gb], axis=1).astype(BF16)


def _new_page(x, db, dec_seq):
    xt = jnp.swapaxes(x.reshape(db, dec_seq, x.shape[1]), 1, 2)
    return jnp.pad(xt, ((0, 0), (0, 0), (0, PAGE_SIZE - dec_seq)))


def _head_rows(x, db, heads, kv_heads):
    g = heads // kv_heads
    xh = x.reshape(db, -1, kv_heads, g, 1, HEAD_DIM)
    eye = jnp.eye(kv_heads, dtype=x.dtype).reshape(1, 1, kv_heads, 1, kv_heads, 1)
    return (xh * eye).reshape(db, -1, kv_heads * HEAD_DIM)


def _own_features(o, heads, kv_heads):
    db = o.shape[0]
    g = heads // kv_heads
    t = o.shape[1] // heads
    oh = o.reshape(db, t, kv_heads, g, kv_heads, HEAD_DIM)
    idx = jnp.arange(kv_heads)
    return oh[:, :, idx, :, idx, :].transpose(1, 2, 0, 3, 4).reshape(db * t, heads * HEAD_DIM)


def _feature_major(pool):
    lead = pool.shape[:2]
    return jnp.moveaxis(pool.reshape(lead + (PAGE_SIZE, -1)), 2, 3)


def _token_major(xt, tail):
    b, _, t = xt.shape
    return jnp.moveaxis(xt, 1, 2).reshape((1, b, t) + tail)


def kernel(x_prompt, x_sample, cache_fox_k, cache_fox_v, cache_fox_logf, cache_dsa_k, cache_dsa_v, cache_idx_k,
           page_table, norm1_g, w_in, b_forget, w_o_fox, w_o_dsa, w_out, norm2_g, w_up, w_down, final_norm_g):
    batch, seq_len, d = x_prompt.shape
    db, dec_seq, _ = x_sample.shape
    depth = norm1_g.shape[0]
    n_pages = page_table.shape[1]
    past = n_pages * PAGE_SIZE
    assert dec_seq <= SUBLANES and dec_seq & (dec_seq - 1) == 0
    assert n_pages % FOX_PAGES == 0 and n_pages % DSA_PAGES == 0

    tm_p = min(TOKEN_TILE, seq_len)
    tm_s = db * dec_seq
    rope_p = _rope_tables(jnp.arange(seq_len))
    rope_s = _rope_tables(jnp.tile(past + jnp.arange(dec_seq), db))

    fox_k_pool, fox_v_pool, fox_lf_pool, dsa_k_pool, dsa_v_pool, idx_k_pool = (
        _feature_major(c) for c in (cache_fox_k, cache_fox_v, cache_fox_logf, cache_dsa_k, cache_dsa_v, cache_idx_k))

    xp = x_prompt.reshape(batch * seq_len, d)
    xs = x_sample.reshape(db * dec_seq, d)
    rows_p, rows_s = [], []
    for l in range(depth):
        w1, wg = _repack_w_in(w_in[l], d)
        bias = _pad_cols(b_forget[l][None, :], LANES)
        g1 = norm1_g[l][None, :]
        g2 = norm2_g[l][None, :]
        gf = final_norm_g[None, :] if l == depth - 1 else jnp.ones((1, d), F32)
        post_w = (g1, wg, w_o_fox[l].astype(BF16), w_o_dsa[l].astype(BF16), w_out[l].astype(BF16), g2,
                  w_up[l].astype(BF16), w_down[l].astype(BF16), gf)

        (fq, dq, fk, fv, lf, dk, dv, ik, negc, fkt, fvd, dkt, dvd, iqt, iwt, ikb, fkn) = _proj(
            xp, g1, w1, bias, rope_p, prompt=True, seq_len=seq_len, tm=tm_p)
        o_fox = _fox_prompt(fq, fkt, fvd, negc, fkn, batch=batch, seq_len=seq_len, tq=min(FOX_Q_TILE, seq_len), tk=tm_p)
        o_dsa = _dsa_prompt(iqt, iwt, dq, ikb, dkt, dvd, batch=batch, seq_len=seq_len, tq=min(DSA_Q_TILE, seq_len), ch=tm_p)
        xp = _post(xp, o_fox, o_dsa, *post_w, tm=tm_p)
        rows_p.append((fk, fv, lf, dk, dv, ik))

        (fq, dq, fk, fv, lf, dk, dv, ik, iq, iw) = _proj(xs, g1, w1, bias, rope_s, prompt=False, seq_len=tm_s, tm=tm_s)
        o_fox = _fox_sample(
            page_table, _head_rows(fq, db, FOX_HEADS, FOX_KV_HEADS), _new_page(fk, db, dec_seq), _new_page(fv, db, dec_seq),
            _new_page(lf, db, dec_seq), fox_k_pool, fox_v_pool, fox_lf_pool, layer=l, dec_seq=dec_seq)
        ksel = min(TOPK_MAX, (past + dec_seq) // 4)
        sel_bias = _idx_sample(
            page_table, iq.reshape(db, dec_seq * IDX_HEADS, IDX_DIM), iw.reshape(db, dec_seq * IDX_HEADS, 1),
            _new_page(ik, db, dec_seq), idx_k_pool, layer=l, dec_seq=dec_seq, ksel=ksel)
        o_dsa = _dsa_sample(
            page_table, _head_rows(dq, db, DSA_HEADS, DSA_KV_HEADS), sel_bias, _new_page(dk, db, dec_seq),
            _new_page(dv, db, dec_seq), dsa_k_pool, dsa_v_pool, layer=l, dec_seq=dec_seq)
        xs = _post(xs, _own_features(o_fox, FOX_HEADS, FOX_KV_HEADS).astype(BF16),
                   _own_features(o_dsa, DSA_HEADS, DSA_KV_HEADS).astype(BF16), *post_w, tm=tm_s)
        rows_s.append((fk, fv, lf, dk, dv, ik))

    tails = ((FOX_KV_HEADS, HEAD_DIM), (FOX_KV_HEADS, HEAD_DIM), (FOX_HEADS,), (DSA_KV_HEADS, HEAD_DIM),
             (DSA_KV_HEADS, HEAD_DIM), (IDX_DIM,))
    dtypes = tuple(c.dtype for c in (cache_fox_k, cache_fox_v, cache_fox_logf, cache_dsa_k, cache_dsa_v, cache_idx_k))
    new_p = tuple(jnp.concatenate([_token_major(r[k], tails[k]) for r in rows_p], axis=0).astype(dtypes[k])
                  for k in range(len(tails)))
    new_s = tuple(jnp.stack([r[k].reshape((db, dec_seq) + tails[k]) for r in rows_s], axis=0).astype(dtypes[k])
                  for k in range(len(tails)))
    return (xp.reshape(batch, seq_len, d), xs.reshape(db, dec_seq, d)) + new_p + new_s
```

```python
import functools

import jax
import jax.numpy as jnp
import numpy as np
from jax import lax
from jax.experimental import pallas as pl
from jax.experimental.pallas import tpu as pltpu

HEAD_DIM = 64
FOX_HEADS = 8
FOX_KV_HEADS = 4
DSA_HEADS = 8
DSA_KV_HEADS = 2
IDX_HEADS = 8
IDX_DIM = 64
TOPK_MAX = 256
ROPE_THETA = 500000.0
ROPE_FRACTION_DIV = 4
NORM_EPS = 1e-6
PAGE_SIZE = 128

FOX_Q = FOX_HEADS * HEAD_DIM
FOX_KV = FOX_KV_HEADS * HEAD_DIM
DSA_Q = DSA_HEADS * HEAD_DIM
DSA_KV = DSA_KV_HEADS * HEAD_DIM
IDX_Q = IDX_HEADS * IDX_DIM
FOX_GROUPS = FOX_HEADS // FOX_KV_HEADS
DSA_GROUPS = DSA_HEADS // DSA_KV_HEADS

LANES = 128
SUBLANES = 8
NEG = -(2.0 ** 100)
QK_SCALE = HEAD_DIM ** -0.5
VMEM_LIMIT = 56 * 1024 * 1024

TOKEN_TILE = 512
FOX_Q_TILE = 512
DSA_Q_TILE = 128
FOX_PAGES = 32
DSA_PAGES = 32
BISECT_PASSES = 16
SKIP_MARGIN = 110.0
NORM_SLACK = 1.05

F32 = jnp.float32
BF16 = jnp.bfloat16

_C_FOX = 0
_C_DSA = _C_FOX + FOX_Q + 2 * FOX_KV
_C_IQ = _C_DSA + DSA_Q + 2 * DSA_KV
_C_IK = _C_IQ + IDX_Q
_C_FF = _C_IK + LANES
_C_IW = _C_FF + LANES
_C_END = _C_IW + LANES


def _dot(a, b):
    return jnp.dot(a, b, preferred_element_type=F32)


def _dot_nt(a, b):
    return lax.dot_general(a, b, (((1,), (1,)), ((), ())), preferred_element_type=F32)


def _split3(x):
    hi = x.astype(BF16)
    r1 = x - hi.astype(F32)
    mid = r1.astype(BF16)
    lo = (r1 - mid.astype(F32)).astype(BF16)
    return hi, mid, lo


def _lane_iota(shape):
    return lax.broadcasted_iota(jnp.int32, shape, len(shape) - 1)


def _row_iota(shape):
    return lax.broadcasted_iota(jnp.int32, shape, len(shape) - 2)


def _rope(x, cos, sa, sb):
    w = x.shape[1]
    reps = w // LANES
    if reps > 1:
        cos = jnp.concatenate([cos] * reps, axis=1)
        sa = jnp.concatenate([sa] * reps, axis=1)
        sb = jnp.concatenate([sb] * reps, axis=1)
    up = pltpu.roll(x, w - 8, 1)
    dn = pltpu.roll(x, 8, 1)
    return x * cos + up * sa + dn * sb


def _rmsnorm(x, g):
    ms = jnp.mean(x * x, axis=-1, keepdims=True)
    return x * lax.rsqrt(ms + NORM_EPS) * g


def _proj_body(x_ref, g_ref, w_ref, b_ref, rope_ref, *refs, prompt, blocks_per_seq):
    fq_ref, dq_ref = refs[:2]
    tm = x_ref.shape[0]
    n = _rmsnorm(x_ref[...], g_ref[...]).astype(BF16)
    cos, sa, sb = rope_ref[0], rope_ref[1], rope_ref[2]

    def mm(lo, hi):
        return _dot(n, w_ref[:, lo:hi])

    fq = mm(_C_FOX, _C_FOX + FOX_Q)
    fq_ref[...] = (fq * QK_SCALE).astype(BF16)
    fk = mm(_C_FOX + FOX_Q, _C_FOX + FOX_Q + FOX_KV)
    fv = mm(_C_FOX + FOX_Q + FOX_KV, _C_DSA)
    dq = _rope(mm(_C_DSA, _C_DSA + DSA_Q), cos, sa, sb)
    dq_ref[...] = (dq * QK_SCALE).astype(BF16)
    dk = _rope(mm(_C_DSA + DSA_Q, _C_DSA + DSA_Q + DSA_KV), cos, sa, sb)
    dv = mm(_C_DSA + DSA_Q + DSA_KV, _C_IQ)
    iq = _rope(mm(_C_IQ, _C_IK), cos, sa, sb) * (IDX_DIM ** -0.5)
    ikg = _rope(mm(_C_IK, _C_FF), cos, sa, sb)
    ff = mm(_C_FF, _C_IW) + b_ref[...]
    lf = jnp.minimum(ff, 0.0) - jnp.log(1.0 + jnp.exp(-jnp.abs(ff)))
    iw = mm(_C_IW, _C_END) * (IDX_HEADS ** -0.5)

    if not prompt:
        fk_ref, fv_ref, lf_ref, dk_ref, dv_ref, ik_ref, iq_ref, iw_ref = refs[2:]
        fk_ref[...] = fk
        fv_ref[...] = fv
        lf_ref[...] = lf[:, :FOX_HEADS]
        dk_ref[...] = dk
        dv_ref[...] = dv
        ik_ref[...] = ikg[:, :IDX_DIM]
        iq_ref[...] = iq.astype(BF16)
        iw_ref[...] = iw[:, :IDX_HEADS]
        return
    (fkx_ref, fvx_ref, lfx_ref, dkx_ref, dvx_ref, ikx_ref,
     negc_ref, fkt_ref, fvd_ref, dkt_ref, dvd_ref, iqt_ref, iwt_ref, ikb_ref, fkn_ref, carry_ref) = refs[2:]
    low = _lane_iota((tm, LANES)) < HEAD_DIM

    fk_t, dk_t, ik_t = fk.T, dk.T, ikg.T
    fkx_ref[0] = fk_t
    fvx_ref[0] = fv.T
    lfx_ref[0] = lf.T[:FOX_HEADS, :]
    dkx_ref[0] = dk_t
    dvx_ref[0] = dv.T
    ikx_ref[0] = ik_t[:IDX_DIM, :]

    @pl.when(pl.program_id(0) % blocks_per_seq == 0)
    def _():
        carry_ref[...] = jnp.zeros_like(carry_ref)

    tri = (_row_iota((tm, tm)) >= _lane_iota((tm, tm))).astype(BF16)
    hi, mid, lo = _split3(lf)
    csum = _dot(tri, hi) + _dot(tri, mid) + _dot(tri, lo) + carry_ref[0:1, :]
    carry_ref[0:1, :] = csum[tm - 1:tm, :]
    negc_ref[0] = -(csum.T[:FOX_HEADS, :])

    grp = ((_row_iota((FOX_KV, LANES)) >> (HEAD_DIM.bit_length() - 1))
           == (_lane_iota((FOX_KV, LANES)) >> (FOX_GROUPS.bit_length() - 1)))
    fkn_ref[0] = jnp.max(_dot((fk * fk).astype(BF16), grp.astype(BF16)), axis=0, keepdims=True)

    ones_lane = jnp.where(_lane_iota((tm, LANES)) == HEAD_DIM, 1.0, 0.0)

    def value_tiles(pair):
        swap = pltpu.roll(pair, HEAD_DIM, 1)
        return jnp.where(low, pair, ones_lane).astype(BF16), jnp.where(low, swap, ones_lane).astype(BF16)

    fkt = fk_t.astype(BF16)
    for h in range(FOX_KV_HEADS):
        kt = fkt[h * HEAD_DIM:(h + 1) * HEAD_DIM, :]
        fkt_ref[h, 0] = jnp.concatenate([kt, kt], axis=0)
    for p in range(FOX_KV_HEADS // 2):
        fvd_ref[2 * p], fvd_ref[2 * p + 1] = value_tiles(fv[:, p * LANES:(p + 1) * LANES])

    dkt = dk_t.astype(BF16)
    dv_tiles = value_tiles(dv)
    for h in range(DSA_KV_HEADS):
        kt = dkt[h * HEAD_DIM:(h + 1) * HEAD_DIM, :]
        dkt_ref[h, 0] = jnp.concatenate([kt, kt, kt, kt], axis=0)
        dvd_ref[h] = dv_tiles[h]

    iqt_ref[...] = iq.T.astype(BF16)
    iwt_ref[...] = iw.T[:IDX_HEADS, :]
    ikb_ref[0] = ikg[:, :IDX_DIM].astype(BF16)


def _proj(x2d, g1, w1, bias, rope_tab, *, prompt, seq_len, tm):
    m, d = x2d.shape
    nblk = m // tm
    npos = rope_tab.shape[1] // tm
    row = lambda i: (i, 0)
    col = lambda i: (0, i)
    outs = [
        (jax.ShapeDtypeStruct((m, FOX_Q), BF16), pl.BlockSpec((tm, FOX_Q), row)),
        (jax.ShapeDtypeStruct((m, DSA_Q), BF16), pl.BlockSpec((tm, DSA_Q), row)),
    ]
    cache_widths = (FOX_KV, FOX_KV, FOX_HEADS, DSA_KV, DSA_KV, IDX_DIM)
    scratch = []
    if prompt:
        bps = seq_len // tm
        feat_major = lambda i: (i // bps, 0, i % bps)
        outs += [(jax.ShapeDtypeStruct((m // seq_len, w, seq_len), F32), pl.BlockSpec((1, w, tm), feat_major))
                 for w in cache_widths]
        outs += [
            (jax.ShapeDtypeStruct((nblk, FOX_HEADS, tm), F32), pl.BlockSpec((1, FOX_HEADS, tm), lambda i: (i, 0, 0))),
            (jax.ShapeDtypeStruct((FOX_KV_HEADS, nblk, 2 * HEAD_DIM, tm), BF16),
             pl.BlockSpec((FOX_KV_HEADS, 1, 2 * HEAD_DIM, tm), lambda i: (0, i, 0, 0))),
            (jax.ShapeDtypeStruct((FOX_KV_HEADS, m, 2 * HEAD_DIM), BF16),
             pl.BlockSpec((FOX_KV_HEADS, tm, 2 * HEAD_DIM), lambda i: (0, i, 0))),
            (jax.ShapeDtypeStruct((DSA_KV_HEADS, nblk, 4 * HEAD_DIM, tm), BF16),
             pl.BlockSpec((DSA_KV_HEADS, 1, 4 * HEAD_DIM, tm), lambda i: (0, i, 0, 0))),
            (jax.ShapeDtypeStruct((DSA_KV_HEADS, m, 2 * HEAD_DIM), BF16),
             pl.BlockSpec((DSA_KV_HEADS, tm, 2 * HEAD_DIM), lambda i: (0, i, 0))),
            (jax.ShapeDtypeStruct((IDX_Q, m), BF16), pl.BlockSpec((IDX_Q, tm), col)),
            (jax.ShapeDtypeStruct((IDX_HEADS, m), F32), pl.BlockSpec((IDX_HEADS, tm), col)),
            (jax.ShapeDtypeStruct((nblk, tm, IDX_DIM), BF16), pl.BlockSpec((1, tm, IDX_DIM), lambda i: (i, 0, 0))),
            (jax.ShapeDtypeStruct((nblk, 1, LANES), F32), pl.BlockSpec((1, 1, LANES), lambda i: (i, 0, 0))),
        ]
        scratch = [pltpu.VMEM((SUBLANES, LANES), F32)]
    else:
        outs += [(jax.ShapeDtypeStruct((m, w), F32), pl.BlockSpec((tm, w), row)) for w in cache_widths]
        outs += [
            (jax.ShapeDtypeStruct((m, IDX_Q), BF16), pl.BlockSpec((tm, IDX_Q), row)),
            (jax.ShapeDtypeStruct((m, IDX_HEADS), F32), pl.BlockSpec((tm, IDX_HEADS), row)),
        ]
    const = lambda i: (0, 0)
    return pl.pallas_call(
        functools.partial(_proj_body, prompt=prompt, blocks_per_seq=seq_len // tm),
        grid=(nblk,),
        in_specs=[
            pl.BlockSpec((tm, d), row),
            pl.BlockSpec((1, d), const),
            pl.BlockSpec((d, _C_END), const, pipeline_mode=pl.Buffered(1)),
            pl.BlockSpec((1, LANES), const),
            pl.BlockSpec((3, tm, LANES), lambda i: (0, i % npos, 0)),
        ],
        out_specs=[o[1] for o in outs],
        out_shape=[o[0] for o in outs],
        scratch_shapes=scratch,
        compiler_params=pltpu.CompilerParams(dimension_semantics=("arbitrary",), vmem_limit_bytes=VMEM_LIMIT),
        name="proj_prompt" if prompt else "proj_sample",
    )(x2d, g1, w1, bias, rope_tab)


def _softmax_step(s, m, acc, v):
    m_new = jnp.maximum(m, jnp.max(s, axis=-1, keepdims=True))
    acc = jnp.exp(m - m_new) * acc + _dot(jnp.exp(s - m_new).astype(BF16), v)
    return m_new, acc


def _softmax_init(rows):
    return jnp.full((rows, 1), NEG, F32), jnp.zeros((rows, LANES), F32)


def _head_outputs(acc, tq, groups):
    low = _lane_iota((tq, LANES)) < HEAD_DIM
    pairs = []
    for g in range(0, groups, 2):
        even, odd = (acc[k * tq:(k + 1) * tq] for k in (g, g + 1))
        even = even / even[:, HEAD_DIM:HEAD_DIM + 1]
        odd = odd / odd[:, HEAD_DIM:HEAD_DIM + 1]
        pairs.append(jnp.where(low, even, pltpu.roll(odd, HEAD_DIM, 1)))
    return pairs[0] if len(pairs) == 1 else jnp.concatenate(pairs, axis=1)


def _stack_heads(q, tq, groups):
    lane = _lane_iota((tq, groups * HEAD_DIM))
    zero = jnp.zeros_like(q)
    return jnp.concatenate(
        [jnp.where((lane >= g * HEAD_DIM) & (lane < (g + 1) * HEAD_DIM), q, zero) for g in range(groups)], axis=0)


def _fox_skip_count(q_ref, kn_ref, first_ref, last_ref, nfull):
    qf = q_ref[...].astype(F32)
    head_of = ((_row_iota((FOX_Q, LANES)) >> (HEAD_DIM.bit_length() - 1)) == _lane_iota((FOX_Q, LANES))).astype(BF16)
    qn2 = jnp.max(_dot((qf * qf).astype(BF16), head_of), axis=0, keepdims=True)
    kn2 = jnp.max(kn_ref[:, 0, :], axis=0, keepdims=True)
    bound = NORM_SLACK * jnp.sqrt(qn2 * kn2)
    thr = jnp.where(_lane_iota((1, LANES)) < FOX_HEADS, 2.0 * bound + SKIP_MARGIN, -1.0)
    gap = first_ref[0] - last_ref[:, 0, :]
    skip = jnp.min(jnp.where(gap > thr, 1.0, 0.0), axis=1, keepdims=True)
    blk = _row_iota(skip.shape)
    return jnp.sum(jnp.where(blk < nfull, skip, 0.0)).astype(jnp.int32)


def _fox_prompt_body(q_ref, kt_ref, v_ref, negc_ref, kn_ref, first_ref, last_ref, o_ref, *, tq, tk):
    i = pl.program_id(1)
    nfull = (i * tq) // tk
    nskip = _fox_skip_count(q_ref, kn_ref, first_ref, last_ref, nfull)
    g = FOX_GROUPS
    w = g * HEAD_DIM
    qpos = i * tq + (_row_iota((g * tq, tk)) & (tq - 1))
    kloc = _lane_iota((g * tq, tk))
    qs = [_stack_heads(q_ref[:, h * w:(h + 1) * w], tq, g) for h in range(FOX_KV_HEADS)]

    def step(j, carry, masked):
        nb = negc_ref[j]
        off = pl.multiple_of(j * tk, tk)
        out = []
        for h in range(FOX_KV_HEADS):
            bias = jnp.concatenate(
                [jnp.broadcast_to(nb[h * g + k:h * g + k + 1, :], (tq, tk)) for k in range(g)], axis=0)
            s = _dot(qs[h], kt_ref[h, j]) + bias
            if masked:
                s = jnp.where(j * tk + kloc <= qpos, s, NEG)
            out.append(_softmax_step(s, *carry[h], v_ref[h, pl.ds(off, tk), :]))
        return tuple(out)

    init = tuple(_softmax_init(g * tq) for _ in range(FOX_KV_HEADS))
    carry = lax.fori_loop(nskip, nfull, lambda j, c: step(j, c, False), init)
    carry = step(nfull, carry, True)
    for h in range(FOX_KV_HEADS):
        o_ref[:, h * w:(h + 1) * w] = _head_outputs(carry[h][1], tq, g).astype(BF16)


def _fox_prompt(fq, fkt, fvd, negc, fkn, *, batch, seq_len, tq, tk):
    m = fq.shape[0]
    nq = seq_len // tq
    nkb = seq_len // tk
    lane_pad = lambda a: jnp.pad(a, ((0, 0), (0, LANES - FOX_HEADS)))[:, None, :]
    a_last = lane_pad(negc[:, :, tk - 1])
    a_first = lane_pad(jnp.swapaxes(negc[:, :, ::tq], 1, 2).reshape(batch * nq, FOX_HEADS))
    return pl.pallas_call(
        functools.partial(_fox_prompt_body, tq=tq, tk=tk),
        grid=(batch, nq),
        in_specs=[
            pl.BlockSpec((tq, FOX_Q), lambda b, i: (b * nq + i, 0)),
            pl.BlockSpec((FOX_KV_HEADS, nkb, 2 * HEAD_DIM, tk), lambda b, i: (0, b, 0, 0), pipeline_mode=pl.Buffered(1)),
            pl.BlockSpec((FOX_KV_HEADS, seq_len, 2 * HEAD_DIM), lambda b, i: (0, b, 0), pipeline_mode=pl.Buffered(1)),
            pl.BlockSpec((nkb, FOX_HEADS, tk), lambda b, i: (b, 0, 0), pipeline_mode=pl.Buffered(1)),
            pl.BlockSpec((nkb, 1, LANES), lambda b, i: (b, 0, 0)),
            pl.BlockSpec((1, 1, LANES), lambda b, i: (b * nq + i, 0, 0)),
            pl.BlockSpec((nkb, 1, LANES), lambda b, i: (b, 0, 0)),
        ],
        out_specs=pl.BlockSpec((tq, FOX_Q), lambda b, i: (b * nq + i, 0)),
        out_shape=jax.ShapeDtypeStruct((m, FOX_Q), BF16),
        compiler_params=pltpu.CompilerParams(dimension_semantics=("arbitrary", "arbitrary"), vmem_limit_bytes=VMEM_LIMIT),
        name="fox_prompt",
    )(fq, fkt, fvd, negc, fkn, a_first, a_last)


def _topk_threshold(reduce_fn, idx_of, nvalid, ksel, n_idx):
    kf = jnp.float32(ksel)
    inf = jnp.float32(jnp.inf)
    one, zero = jnp.float32(1.0), jnp.float32(0.0)
    shape = nvalid.shape

    def count_ge(t):
        return reduce_fn(lambda c, blk: jnp.where(blk >= t, one, zero), "sum")

    row_max = reduce_fn(lambda c, blk: blk, "max")
    row_min = reduce_fn(lambda c, blk: jnp.where(blk > -inf, blk, inf), "min")
    all_sel = jnp.where(nvalid <= kf, one, zero)

    def bisect(_, st):
        lo, clo, hi, chi, done = st
        t = jnp.where(hi == inf, row_max, 0.5 * (lo + hi))
        cnt = count_ge(t)
        upd = (done < 0.5) & (t > lo) & (t < hi)
        ge = cnt >= kf
        lo = jnp.where(upd & ge, t, lo)
        clo = jnp.where(upd & ge, cnt, clo)
        hi = jnp.where(upd & (cnt < kf), t, hi)
        chi = jnp.where(upd & (cnt < kf), cnt, chi)
        return lo, clo, hi, chi, jnp.where(clo == kf, one, done)

    st = (row_min, nvalid, jnp.full(shape, inf, F32), jnp.zeros(shape, F32), all_sel)
    st = lax.fori_loop(0, BISECT_PASSES, bisect, st)

    def snap(st):
        lo, clo, hi, chi, done = st
        v = reduce_fn(lambda c, blk: jnp.where(blk < hi, blk, -inf), "max")
        cv = count_ge(v)
        fin = (done < 0.5) & (cv >= kf)
        go = (done < 0.5) & (cv < kf)
        lo = jnp.where(fin, v, lo)
        clo = jnp.where(fin, cv, clo)
        hi = jnp.where(go, v, hi)
        chi = jnp.where(go, cv, chi)
        return lo, clo, hi, chi, jnp.where(fin, one, done)

    lo, clo, hi, chi, done = lax.while_loop(lambda st: jnp.min(st[4]) < 0.5, snap, st)

    thr = jnp.where(all_sel > 0.5, -inf, lo)
    need_cut = jnp.where((all_sel < 0.5) & (clo > kf), one, zero)
    need = kf - chi
    big = jnp.float32(n_idx)

    def cut_search():
        def body(_, st):
            jlo, jhi = st
            mid = jnp.floor(0.5 * (jlo + jhi))
            cnt = reduce_fn(lambda c, blk: jnp.where((blk == thr) & (idx_of(c) <= mid), one, zero), "sum")
            ok = cnt >= need
            return jnp.where(ok, jlo, mid), jnp.where(ok, mid, jhi)

        npass = int(np.ceil(np.log2(n_idx))) + 1
        _, jhi = lax.fori_loop(0, npass, body, (jnp.full(shape, -1.0, F32), jnp.full(shape, n_idx - 1.0, F32)))
        return jhi

    jcut = lax.cond(jnp.max(need_cut) > 0.5, cut_search, lambda: jnp.full(shape, -1.0, F32))
    jcut = jnp.where(need_cut > 0.5, jcut, jnp.where(all_sel > 0.5, -1.0, big))
    return thr, jcut


_REDUCE_OPS = {"sum": (jnp.add, 0.0, jnp.sum), "max": (jnp.maximum, -np.inf, jnp.max), "min": (jnp.minimum, np.inf, jnp.min)}


def _tree(parts, op):
    while len(parts) > 1:
        parts = [op(parts[k], parts[k + 1]) if k + 1 < len(parts) else parts[k] for k in range(0, len(parts), 2)]
    return parts[0]


def _row_major_reducer(s_ref, nch):
    _, rows, ch = s_ref.shape

    def reduce_fn(elem_fn, kind):
        op, init, lane_reduce = _REDUCE_OPS[kind]

        def body(c, acc):
            x = elem_fn(c, s_ref[c])
            return op(acc, _tree([x[:, k * LANES:(k + 1) * LANES] for k in range(ch // LANES)], op))

        acc = jnp.full((rows, LANES), init, F32)
        if isinstance(nch, int):
            for c in range(nch):
                acc = body(c, acc)
        else:
            acc = lax.fori_loop(0, nch, body, acc)
        return lane_reduce(acc, axis=1, keepdims=True)

    return reduce_fn


def _lane_major_reducer(s_ref, nch):
    _, ch, _ = s_ref.shape

    def reduce_fn(elem_fn, kind):
        op, init, sub_reduce = _REDUCE_OPS[kind]

        def body(c, acc):
            x = elem_fn(c, s_ref[c])
            return op(acc, _tree([x[k * SUBLANES:(k + 1) * SUBLANES, :] for k in range(ch // SUBLANES)], op))

        acc = lax.fori_loop(0, nch, body, jnp.full((SUBLANES, LANES), init, F32))
        return sub_reduce(acc, axis=0, keepdims=True)

    return reduce_fn


def _dsa_prompt_body(iqt_ref, iwt_ref, dq_ref, ikb_ref, dkt_ref, dv_ref, o_ref, s_ref, bias_ref, *, tq, ch, ksel):
    i = pl.program_id(1)
    nch = (i * tq + tq - 1) // ch + 1
    g = DSA_GROUPS
    w = g * HEAD_DIM

    iqt = iqt_ref[...]
    rhs = [jnp.concatenate([iqt[(2 * p) * IDX_DIM:(2 * p + 1) * IDX_DIM, :], iqt[(2 * p + 1) * IDX_DIM:(2 * p + 2) * IDX_DIM, :]],
                           axis=1) for p in range(IDX_HEADS // 2)]
    iwt = iwt_ref[...]
    qpos = i * tq + _lane_iota((ch, tq))
    kloc = _row_iota((ch, tq))

    def score_chunk(c, _):
        keys = ikb_ref[c]
        acc = None
        for p in range(IDX_HEADS // 2):
            d = jnp.maximum(_dot(keys, rhs[p]), 0.0)
            term = d[:, :tq] * iwt[2 * p:2 * p + 1, :] + d[:, tq:] * iwt[2 * p + 1:2 * p + 2, :]
            acc = term if acc is None else acc + term
        s_ref[c] = jnp.where(c * ch + kloc <= qpos, acc, -jnp.inf)
        return 0

    lax.fori_loop(0, nch, score_chunk, 0)

    nvalid = (i * tq + _lane_iota((1, tq)) + 1).astype(F32)
    idx_of = lambda c: (c * ch + kloc).astype(F32)
    thr, jcut = _topk_threshold(_lane_major_reducer(s_ref, nch), idx_of, nvalid, ksel, s_ref.shape[0] * ch)

    def write_bias(c, _):
        blk = s_ref[c]
        sel = (blk > thr) | ((blk == thr) & (idx_of(c) <= jcut))
        bias_ref[c] = jnp.where(sel, 0.0, NEG).T
        return 0

    lax.fori_loop(0, nch, write_bias, 0)

    qs = [_stack_heads(dq_ref[:, h * w:(h + 1) * w], tq, g) for h in range(DSA_KV_HEADS)]

    def step(c, carry):
        b = bias_ref[c]
        bias = jnp.concatenate([b] * g, axis=0)
        off = pl.multiple_of(c * ch, ch)
        return tuple(_softmax_step(_dot(qs[h], dkt_ref[h, c]) + bias, *carry[h], dv_ref[h, pl.ds(off, ch), :])
                     for h in range(DSA_KV_HEADS))

    carry = lax.fori_loop(0, nch, step, tuple(_softmax_init(g * tq) for _ in range(DSA_KV_HEADS)))
    for h in range(DSA_KV_HEADS):
        o_ref[:, h * w:(h + 1) * w] = _head_outputs(carry[h][1], tq, g).astype(BF16)


def _dsa_prompt(iqt, iwt, dq, ikb, dkt, dvd, *, batch, seq_len, tq, ch):
    m = dq.shape[0]
    nq = seq_len // tq
    nkb = seq_len // ch
    ksel = min(TOPK_MAX, seq_len // 4)
    row = lambda b, i: (b * nq + i, 0)
    col = lambda b, i: (0, b * nq + i)
    return pl.pallas_call(
        functools.partial(_dsa_prompt_body, tq=tq, ch=ch, ksel=ksel),
        grid=(batch, nq),
        in_specs=[
            pl.BlockSpec((IDX_Q, tq), col),
            pl.BlockSpec((IDX_HEADS, tq), col),
            pl.BlockSpec((tq, DSA_Q), row),
            pl.BlockSpec((nkb, ch, IDX_DIM), lambda b, i: (b, 0, 0), pipeline_mode=pl.Buffered(1)),
            pl.BlockSpec((DSA_KV_HEADS, nkb, 4 * HEAD_DIM, ch), lambda b, i: (0, b, 0, 0), pipeline_mode=pl.Buffered(1)),
            pl.BlockSpec((DSA_KV_HEADS, seq_len, 2 * HEAD_DIM), lambda b, i: (0, b, 0), pipeline_mode=pl.Buffered(1)),
        ],
        out_specs=pl.BlockSpec((tq, DSA_Q), row),
        out_shape=jax.ShapeDtypeStruct((m, DSA_Q), BF16),
        scratch_shapes=[pltpu.VMEM((nkb, ch, tq), F32), pltpu.VMEM((nkb, tq, ch), F32)],
        compiler_params=pltpu.CompilerParams(dimension_semantics=("arbitrary", "arbitrary"), vmem_limit_bytes=VMEM_LIMIT),
        name="dsa_prompt",
    )(iqt, iwt, dq, ikb, dkt, dvd)


def _page_ring(pt_ref, pools, bufs, sem, layer, pages):
    ng = pl.num_programs(1)
    step = pl.program_id(0) * ng + pl.program_id(1)
    last = pl.num_programs(0) * ng - 1

    def copies(s):
        b, j, slot = lax.div(s, ng), lax.rem(s, ng), lax.rem(s, 2)
        return [pltpu.make_async_copy(pool.at[layer, pt_ref[b, j * pages + k]], buf.at[slot, k], sem.at[slot, a])
                for a, (pool, buf) in enumerate(zip(pools, bufs)) for k in range(pages)]

    @pl.when(step == 0)
    def _():
        for c in copies(step):
            c.start()

    @pl.when(step < last)
    def _():
        for c in copies(step + 1):
            c.start()

    for c in copies(step):
        c.wait()
    return lax.rem(step, 2)


def _page_ring_scratch(features, pages):
    return [pltpu.VMEM((2, pages, f, PAGE_SIZE), F32) for f in features] + [pltpu.SemaphoreType.DMA((2, len(features)))]


def _page_tiles(buf, slot):
    return jnp.concatenate([buf[slot, k] for k in range(buf.shape[1])], axis=1).astype(BF16)


def _sample_softmax_step(s, vt, m_ref, l_ref, acc_ref):
    m = m_ref[...]
    m_new = jnp.maximum(m, jnp.max(s, axis=-1, keepdims=True))
    alpha = jnp.exp(m - m_new)
    p = jnp.exp(s - m_new)
    m_ref[...] = m_new
    l_ref[...] = alpha * l_ref[...] + jnp.sum(p, axis=-1, keepdims=True)
    acc_ref[...] = alpha * acc_ref[...] + _dot_nt(p.astype(BF16), vt)


def _fox_sample_body(pt_ref, q_ref, kn_ref, vn_ref, lfn_ref, k_hbm, v_hbm, lf_hbm, o_ref, m_ref, l_ref, acc_ref, pre_ref,
                     kbuf, vbuf, lbuf, sem, *, layer, dec_seq):
    np_ = FOX_PAGES
    j = pl.program_id(1)
    rows = dec_seq * FOX_HEADS
    slot = _page_ring(pt_ref, (k_hbm, v_hbm, lf_hbm), (kbuf, vbuf, lbuf), sem, layer, np_)

    @pl.when(j == 0)
    def _():
        m_ref[...] = jnp.full_like(m_ref, NEG)
        l_ref[...] = jnp.zeros_like(l_ref)
        acc_ref[...] = jnp.zeros_like(acc_ref)
        pre_ref[...] = jnp.zeros_like(pre_ref)

    triu = (_row_iota((PAGE_SIZE, PAGE_SIZE)) <= _lane_iota((PAGE_SIZE, PAGE_SIZE))).astype(BF16)

    def cum_keys(lf):
        r = lf.shape[0]
        parts = _dot(jnp.concatenate(_split3(lf), axis=0), triu)
        return parts[0:r] + parts[r:2 * r] + parts[2 * r:3 * r]

    q = q_ref[0]
    nrow = np_ * FOX_HEADS
    within = cum_keys(jnp.concatenate([lbuf[slot, k] for k in range(np_)], axis=0))
    r, c = _row_iota((nrow, nrow)), _lane_iota((nrow, nrow))
    earlier = (((r ^ c) & (FOX_HEADS - 1)) == 0) & (c < r)
    totals = jnp.broadcast_to(within[:, PAGE_SIZE - 1:PAGE_SIZE], (nrow, PAGE_SIZE))
    carried = _dot(earlier.astype(BF16), jnp.concatenate(_split3(totals), axis=1))
    carried = carried[:, 0:PAGE_SIZE] + carried[:, PAGE_SIZE:2 * PAGE_SIZE] + carried[:, 2 * PAGE_SIZE:3 * PAGE_SIZE]
    cum = within + carried + jnp.concatenate([pre_ref[...]] * np_, axis=0)
    pre_ref[...] = jnp.broadcast_to(cum[nrow - FOX_HEADS:nrow, PAGE_SIZE - 1:PAGE_SIZE], pre_ref.shape)
    c_all = jnp.concatenate([cum[k * FOX_HEADS:(k + 1) * FOX_HEADS, :] for k in range(np_)], axis=1)
    s = _dot(q, _page_tiles(kbuf, slot)) - jnp.concatenate([c_all] * dec_seq, axis=0)
    _sample_softmax_step(s, _page_tiles(vbuf, slot), m_ref, l_ref, acc_ref)

    @pl.when(j == pl.num_programs(1) - 1)
    def _():
        c = cum_keys(lfn_ref[0]) + pre_ref[...]
        s = _dot(q, kn_ref[0].astype(BF16)) - jnp.concatenate([c] * dec_seq, axis=0)
        tok = _row_iota((rows, PAGE_SIZE)) >> (FOX_HEADS.bit_length() - 1)
        s = jnp.where(_lane_iota((rows, PAGE_SIZE)) <= tok, s, NEG)
        _sample_softmax_step(s, vn_ref[0].astype(BF16), m_ref, l_ref, acc_ref)
        o_ref[0] = acc_ref[...] / l_ref[...]


def _fox_sample(page_table, q_rows, k_new, v_new, lf_new, k_pool, v_pool, lf_pool, *, layer, dec_seq):
    db, n_pages = page_table.shape
    ng = n_pages // FOX_PAGES
    rows = dec_seq * FOX_HEADS
    per_b = lambda b, j, pt: (b, 0, 0)
    grid_spec = pltpu.PrefetchScalarGridSpec(
        num_scalar_prefetch=1,
        grid=(db, ng),
        in_specs=[
            pl.BlockSpec((1, rows, FOX_KV), per_b),
            pl.BlockSpec((1, FOX_KV, PAGE_SIZE), per_b),
            pl.BlockSpec((1, FOX_KV, PAGE_SIZE), per_b),
            pl.BlockSpec((1, FOX_HEADS, PAGE_SIZE), per_b),
        ] + [pl.BlockSpec(memory_space=pl.ANY)] * 3,
        out_specs=pl.BlockSpec((1, rows, FOX_KV), per_b),
        scratch_shapes=[pltpu.VMEM((rows, 1), F32), pltpu.VMEM((rows, 1), F32), pltpu.VMEM((rows, FOX_KV), F32),
                        pltpu.VMEM((FOX_HEADS, PAGE_SIZE), F32)]
        + _page_ring_scratch((FOX_KV, FOX_KV, FOX_HEADS), FOX_PAGES),
    )
    return pl.pallas_call(
        functools.partial(_fox_sample_body, layer=layer, dec_seq=dec_seq),
        grid_spec=grid_spec,
        out_shape=jax.ShapeDtypeStruct((db, rows, FOX_KV), F32),
        compiler_params=pltpu.CompilerParams(dimension_semantics=("arbitrary", "arbitrary"), vmem_limit_bytes=VMEM_LIMIT),
        name="fox_sample",
    )(page_table, q_rows, k_new, v_new, lf_new, k_pool, v_pool, lf_pool)


def _idx_sample_body(pt_ref, q_ref, w_ref, kn_ref, k_hbm, bias_ref, s_ref, kbuf, sem, *, layer, dec_seq, past, ksel):
    np_ = DSA_PAGES
    j = pl.program_id(1)
    slot = _page_ring(pt_ref, (k_hbm,), (kbuf,), sem, layer, np_)
    ng = pl.num_programs(1)
    ch = np_ * PAGE_SIZE
    q = q_ref[0]
    w = w_ref[0]

    def scores(keys_t):
        d = jnp.maximum(_dot(q, keys_t), 0.0) * w
        shape = (SUBLANES, d.shape[1])
        tok = _row_iota(shape)
        out = jnp.full(shape, -jnp.inf, F32)
        for t in range(dec_seq):
            per_tok = jnp.sum(d[t * IDX_HEADS:(t + 1) * IDX_HEADS], axis=0, keepdims=True)
            out = jnp.where(tok == t, jnp.broadcast_to(per_tok, shape), out)
        return out

    s_ref[j] = scores(_page_tiles(kbuf, slot))

    @pl.when(j == ng - 1)
    def _():
        sn = scores(kn_ref[0].astype(BF16))
        sn = jnp.where(_lane_iota((SUBLANES, PAGE_SIZE)) <= _row_iota((SUBLANES, PAGE_SIZE)), sn, -jnp.inf)
        s_ref[ng] = jnp.concatenate([sn, jnp.full((SUBLANES, ch - PAGE_SIZE), -jnp.inf, F32)], axis=1)
        tok = _row_iota((SUBLANES, 1))
        nvalid = jnp.where(tok < dec_seq, (past + 1 + tok).astype(F32), 0.0)
        idx_of = lambda c: (c * ch + _lane_iota((SUBLANES, ch))).astype(F32)
        thr, jcut = _topk_threshold(_row_major_reducer(s_ref, ng + 1), idx_of, nvalid, ksel, s_ref.shape[0] * ch)
        for c in range(s_ref.shape[0]):
            blk = s_ref[c]
            sel = (blk > thr) | ((blk == thr) & (idx_of(c) <= jcut))
            bias_ref[0, c] = jnp.where(sel, 0.0, NEG)


def _idx_sample(page_table, iq_rows, iw_rows, ik_new, ik_pool, *, layer, dec_seq, ksel):
    db, n_pages = page_table.shape
    ng = n_pages // DSA_PAGES
    ch = DSA_PAGES * PAGE_SIZE
    rows = dec_seq * IDX_HEADS
    per_b = lambda b, j, pt: (b, 0, 0)
    grid_spec = pltpu.PrefetchScalarGridSpec(
        num_scalar_prefetch=1,
        grid=(db, ng),
        in_specs=[
            pl.BlockSpec((1, rows, IDX_DIM), per_b),
            pl.BlockSpec((1, rows, 1), per_b),
            pl.BlockSpec((1, IDX_DIM, PAGE_SIZE), per_b),
            pl.BlockSpec(memory_space=pl.ANY),
        ],
        out_specs=pl.BlockSpec((1, ng + 1, SUBLANES, ch), lambda b, j, pt: (b, 0, 0, 0)),
        scratch_shapes=[pltpu.VMEM((ng + 1, SUBLANES, ch), F32)] + _page_ring_scratch((IDX_DIM,), DSA_PAGES),
    )
    return pl.pallas_call(
        functools.partial(_idx_sample_body, layer=layer, dec_seq=dec_seq, past=n_pages * PAGE_SIZE, ksel=ksel),
        grid_spec=grid_spec,
        out_shape=jax.ShapeDtypeStruct((db, ng + 1, SUBLANES, ch), F32),
        compiler_params=pltpu.CompilerParams(dimension_semantics=("arbitrary", "arbitrary"), vmem_limit_bytes=VMEM_LIMIT),
        name="idx_sample",
    )(page_table, iq_rows, iw_rows, ik_new, ik_pool)


def _dsa_sample_body(pt_ref, q_ref, bias_ref, kn_ref, vn_ref, k_hbm, v_hbm, o_ref, m_ref, l_ref, acc_ref, kbuf, vbuf, sem,
                     *, layer, dec_seq):
    np_ = DSA_PAGES
    j = pl.program_id(1)
    ng = pl.num_programs(1)
    slot = _page_ring(pt_ref, (k_hbm, v_hbm), (kbuf, vbuf), sem, layer, np_)
    rows = dec_seq * DSA_HEADS

    @pl.when(j == 0)
    def _():
        m_ref[...] = jnp.full_like(m_ref, NEG)
        l_ref[...] = jnp.zeros_like(l_ref)
        acc_ref[...] = jnp.zeros_like(acc_ref)

    def mask_rows(b):
        shape = (rows, b.shape[1])
        tok = _row_iota(shape) >> (DSA_HEADS.bit_length() - 1)
        out = jnp.broadcast_to(b[0:1, :], shape)
        for t in range(1, dec_seq):
            out = jnp.where(tok == t, jnp.broadcast_to(b[t:t + 1, :], shape), out)
        return out

    q = q_ref[0]
    s = _dot(q, _page_tiles(kbuf, slot)) + mask_rows(bias_ref[0, j])
    _sample_softmax_step(s, _page_tiles(vbuf, slot), m_ref, l_ref, acc_ref)

    @pl.when(j == ng - 1)
    def _():
        s = _dot(q, kn_ref[0].astype(BF16)) + mask_rows(bias_ref[0, ng, :, 0:PAGE_SIZE])
        _sample_softmax_step(s, vn_ref[0].astype(BF16), m_ref, l_ref, acc_ref)
        o_ref[0] = acc_ref[...] / l_ref[...]


def _dsa_sample(page_table, q_rows, bias, k_new, v_new, k_pool, v_pool, *, layer, dec_seq):
    db, n_pages = page_table.shape
    ng = n_pages // DSA_PAGES
    ch = DSA_PAGES * PAGE_SIZE
    rows = dec_seq * DSA_HEADS
    per_b = lambda b, j, pt: (b, 0, 0)
    grid_spec = pltpu.PrefetchScalarGridSpec(
        num_scalar_prefetch=1,
        grid=(db, ng),
        in_specs=[
            pl.BlockSpec((1, rows, DSA_KV), per_b),
            pl.BlockSpec((1, ng + 1, SUBLANES, ch), lambda b, j, pt: (b, 0, 0, 0)),
            pl.BlockSpec((1, DSA_KV, PAGE_SIZE), per_b),
            pl.BlockSpec((1, DSA_KV, PAGE_SIZE), per_b),
        ] + [pl.BlockSpec(memory_space=pl.ANY)] * 2,
        out_specs=pl.BlockSpec((1, rows, DSA_KV), per_b),
        scratch_shapes=[pltpu.VMEM((rows, 1), F32), pltpu.VMEM((rows, 1), F32), pltpu.VMEM((rows, DSA_KV), F32)]
        + _page_ring_scratch((DSA_KV, DSA_KV), DSA_PAGES),
    )
    return pl.pallas_call(
        functools.partial(_dsa_sample_body, layer=layer, dec_seq=dec_seq),
        grid_spec=grid_spec,
        out_shape=jax.ShapeDtypeStruct((db, rows, DSA_KV), F32),
        compiler_params=pltpu.CompilerParams(dimension_semantics=("arbitrary", "arbitrary"), vmem_limit_bytes=VMEM_LIMIT),
        name="dsa_sample",
    )(page_table, q_rows, bias, k_new, v_new, k_pool, v_pool)


_FF_CHUNK = 1024


def _post_body(x_ref, of_ref, od_ref, g1_ref, wg_ref, wof_ref, wod_ref, wout_ref, g2_ref, wup_ref, wdn_ref, gf_ref, y_ref):
    x = x_ref[...]
    d = x.shape[1]
    n = _rmsnorm(x, g1_ref[...]).astype(BF16)
    gate_a = jax.nn.sigmoid(_dot(n, wg_ref[:, 0:d]))
    gate_b = jax.nn.sigmoid(_dot(n, wg_ref[:, d:2 * d]))
    mix = gate_a * _dot(of_ref[...], wof_ref[...]) + gate_b * _dot(od_ref[...], wod_ref[...])
    h = x + _dot(mix.astype(BF16), wout_ref[...])
    n2 = _rmsnorm(h, g2_ref[...]).astype(BF16)
    y = h
    for c in range(wup_ref.shape[1] // _FF_CHUNK):
        u = jnp.maximum(_dot(n2, wup_ref[:, c * _FF_CHUNK:(c + 1) * _FF_CHUNK]), 0.0)
        y = y + _dot((u * u).astype(BF16), wdn_ref[c * _FF_CHUNK:(c + 1) * _FF_CHUNK, :])
    y_ref[...] = _rmsnorm(y, gf_ref[...])


def _post(x2d, o_fox, o_dsa, g1, wg, wof, wod, wout, g2, wup, wdn, gf, *, tm):
    m, d = x2d.shape
    row = lambda i: (i, 0)
    const = lambda i: (0, 0)

    def whole(a):
        return pl.BlockSpec(a.shape, const, pipeline_mode=pl.Buffered(1))

    return pl.pallas_call(
        _post_body,
        grid=(m // tm,),
        in_specs=[pl.BlockSpec((tm, d), row), pl.BlockSpec((tm, FOX_Q), row), pl.BlockSpec((tm, DSA_Q), row),
                  whole(g1), whole(wg), whole(wof), whole(wod), whole(wout), whole(g2), whole(wup), whole(wdn), whole(gf)],
        out_specs=pl.BlockSpec((tm, d), row),
        out_shape=jax.ShapeDtypeStruct((m, d), F32),
        compiler_params=pltpu.CompilerParams(dimension_semantics=("arbitrary",), vmem_limit_bytes=VMEM_LIMIT),
        name="post",
    )(x2d, o_fox, o_dsa, g1, wg, wof, wod, wout, g2, wup, wdn, gf)


def _rope_tables(pos):
    r = HEAD_DIM // ROPE_FRACTION_DIV
    half = r // 2
    inv_freq = jnp.float32(ROPE_THETA) ** (-jnp.arange(half, dtype=F32) * (2.0 / r))
    ang = pos.astype(F32)[:, None] * inv_freq[None, :]
    cos, sin = jnp.cos(ang), jnp.sin(ang)
    p = pos.shape[0]
    one = jnp.ones((p, HEAD_DIM - r), F32)
    zero_r = jnp.zeros((p, HEAD_DIM - r), F32)
    zero_h = jnp.zeros((p, half), F32)
    c = jnp.concatenate([cos, cos, one], axis=1)
    sa = jnp.concatenate([-sin, zero_h, zero_r], axis=1)
    sb = jnp.concatenate([zero_h, sin, zero_r], axis=1)
    return jnp.stack([jnp.tile(t, (1, LANES // HEAD_DIM)) for t in (c, sa, sb)], axis=0)


def _pad_cols(w, width):
    return jnp.pad(w, ((0, 0), (0, width - w.shape[1])))


def _repack_w_in(w_in, d_model):
    sizes = (FOX_Q, FOX_KV, FOX_KV, FOX_HEADS, DSA_Q, DSA_KV, DSA_KV, IDX_Q, IDX_DIM, IDX_HEADS, d_model, d_model)
    cuts = [int(c) for c in np.cumsum(sizes)[:-1]]
    fq, fk, fv, ff, dq, dk, dv, iq, ik, iw, ga, gb = jnp.split(w_in, cuts, axis=-1)
    w1 = jnp.concatenate([fq, fk, fv, dq, dk, dv, iq, _pad_cols(ik, LANES), _pad_cols(ff, LANES), _pad_cols(iw, LANES)], axis=1)
    return w1.astype(BF16), jnp.concatenate([ga, gb], axis=1).astype(BF16)


def _new_page(x, db, dec_seq):
    xt = jnp.swapaxes(x.reshape(db, dec_seq, x.shape[1]), 1, 2)
    return jnp.pad(xt, ((0, 0), (0, 0), (0, PAGE_SIZE - dec_seq)))


def _head_rows(x, db, heads, kv_heads):
    g = heads // kv_heads
    xh = x.reshape(db, -1, kv_heads, g, 1, HEAD_DIM)
    eye = jnp.eye(kv_heads, dtype=x.dtype).reshape(1, 1, kv_heads, 1, kv_heads, 1)
    return (xh * eye).reshape(db, -1, kv_heads * HEAD_DIM)


def _own_features(o, heads, kv_heads):
    db = o.shape[0]
    g = heads // kv_heads
    t = o.shape[1] // heads
    oh = o.reshape(db, t, kv_heads, g, kv_heads, HEAD_DIM)
    idx = jnp.arange(kv_heads)
    return oh[:, :, idx, :, idx, :].transpose(1, 2, 0, 3, 4).reshape(db * t, heads * HEAD_DIM)


def _feature_major(pool):
    lead = pool.shape[:2]
    return jnp.moveaxis(pool.reshape(lead + (PAGE_SIZE, -1)), 2, 3)


def _token_major(xt, tail):
    b, _, t = xt.shape
    return jnp.moveaxis(xt, 1, 2).reshape((1, b, t) + tail)


def kernel(x_prompt, x_sample, cache_fox_k, cache_fox_v, cache_fox_logf, cache_dsa_k, cache_dsa_v, cache_idx_k,
           page_table, norm1_g, w_in, b_forget, w_o_fox, w_o_dsa, w_out, norm2_g, w_up, w_down, final_norm_g):
    batch, seq_len, d = x_prompt.shape
    db, dec_seq, _ = x_sample.shape
    depth = norm1_g.shape[0]
    n_pages = page_table.shape[1]
    past = n_pages * PAGE_SIZE
    assert dec_seq <= SUBLANES and dec_seq & (dec_seq - 1) == 0
    assert n_pages % FOX_PAGES == 0 and n_pages % DSA_PAGES == 0

    tm_p = min(TOKEN_TILE, seq_len)
    tm_s = db * dec_seq
    rope_p = _rope_tables(jnp.arange(seq_len))
    rope_s = _rope_tables(jnp.tile(past + jnp.arange(dec_seq), db))

    fox_k_pool, fox_v_pool, fox_lf_pool, dsa_k_pool, dsa_v_pool, idx_k_pool = (
        _feature_major(c) for c in (cache_fox_k, cache_fox_v, cache_fox_logf, cache_dsa_k, cache_dsa_v, cache_idx_k))

    xp = x_prompt.reshape(batch * seq_len, d)
    xs = x_sample.reshape(db * dec_seq, d)
    rows_p, rows_s = [], []
    for l in range(depth):
        w1, wg = _repack_w_in(w_in[l], d)
        bias = _pad_cols(b_forget[l][None, :], LANES)
        g1 = norm1_g[l][None, :]
        g2 = norm2_g[l][None, :]
        gf = final_norm_g[None, :] if l == depth - 1 else jnp.ones((1, d), F32)
        post_w = (g1, wg, w_o_fox[l].astype(BF16), w_o_dsa[l].astype(BF16), w_out[l].astype(BF16), g2,
                  w_up[l].astype(BF16), w_down[l].astype(BF16), gf)

        (fq, dq, fk, fv, lf, dk, dv, ik, negc, fkt, fvd, dkt, dvd, iqt, iwt, ikb, fkn) = _proj(
            xp, g1, w1, bias, rope_p, prompt=True, seq_len=seq_len, tm=tm_p)
        o_fox = _fox_prompt(fq, fkt, fvd, negc, fkn, batch=batch, seq_len=seq_len, tq=min(FOX_Q_TILE, seq_len), tk=tm_p)
        o_dsa = _dsa_prompt(iqt, iwt, dq, ikb, dkt, dvd, batch=batch, seq_len=seq_len, tq=min(DSA_Q_TILE, seq_len), ch=tm_p)
        xp = _post(xp, o_fox, o_dsa, *post_w, tm=tm_p)
        rows_p.append((fk, fv, lf, dk, dv, ik))

        (fq, dq, fk, fv, lf, dk, dv, ik, iq, iw) = _proj(xs, g1, w1, bias, rope_s, prompt=False, seq_len=tm_s, tm=tm_s)
        o_fox = _fox_sample(
            page_table, _head_rows(fq, db, FOX_HEADS, FOX_KV_HEADS), _new_page(fk, db, dec_seq), _new_page(fv, db, dec_seq),
            _new_page(lf, db, dec_seq), fox_k_pool, fox_v_pool, fox_lf_pool, layer=l, dec_seq=dec_seq)
        ksel = min(TOPK_MAX, (past + dec_seq) // 4)
        sel_bias = _idx_sample(
            page_table, iq.reshape(db, dec_seq * IDX_HEADS, IDX_DIM), iw.reshape(db, dec_seq * IDX_HEADS, 1),
            _new_page(ik, db, dec_seq), idx_k_pool, layer=l, dec_seq=dec_seq, ksel=ksel)
        o_dsa = _dsa_sample(
            page_table, _head_rows(dq, db, DSA_HEADS, DSA_KV_HEADS), sel_bias, _new_page(dk, db, dec_seq),
            _new_page(dv, db, dec_seq), dsa_k_pool, dsa_v_pool, layer=l, dec_seq=dec_seq)
        xs = _post(xs, _own_features(o_fox, FOX_HEADS, FOX_KV_HEADS).astype(BF16),
                   _own_features(o_dsa, DSA_HEADS, DSA_KV_HEADS).astype(BF16), *post_w, tm=tm_s)
        rows_s.append((fk, fv, lf, dk, dv, ik))

    tails = ((FOX_KV_HEADS, HEAD_DIM), (FOX_KV_HEADS, HEAD_DIM), (FOX_HEADS,), (DSA_KV_HEADS, HEAD_DIM),
             (DSA_KV_HEADS, HEAD_DIM), (IDX_DIM,))
    dtypes = tuple(c.dtype for c in (cache_fox_k, cache_fox_v, cache_fox_logf, cache_dsa_k, cache_dsa_v, cache_idx_k))
    new_p = tuple(jnp.concatenate([_token_major(r[k], tails[k]) for r in rows_p], axis=0).astype(dtypes[k])
                  for k in range(len(tails)))
    new_s = tuple(jnp.stack([r[k].reshape((db, dec_seq) + tails[k]) for r in rows_s], axis=0).astype(dtypes[k])
                  for k in range(len(tails)))
    return (xp.reshape(batch, seq_len, d), xs.reshape(db, dec_seq, d)) + new_p + new_s
```

```python
import functools

import jax
import jax.numpy as jnp
import numpy as np
from jax import lax
from jax.experimental import pallas as pl
from jax.experimental.pallas import tpu as pltpu

HEAD_DIM = 64
FOX_HEADS = 8
FOX_KV_HEADS = 4
DSA_HEADS = 8
DSA_KV_HEADS = 2
IDX_HEADS = 8
IDX_DIM = 64
TOPK_MAX = 256
ROPE_THETA = 500000.0
ROPE_FRACTION_DIV = 4
NORM_EPS = 1e-6
PAGE_SIZE = 128

FOX_Q = FOX_HEADS * HEAD_DIM
FOX_KV = FOX_KV_HEADS * HEAD_DIM
DSA_Q = DSA_HEADS * HEAD_DIM
DSA_KV = DSA_KV_HEADS * HEAD_DIM
IDX_Q = IDX_HEADS * IDX_DIM
FOX_GROUPS = FOX_HEADS // FOX_KV_HEADS
DSA_GROUPS = DSA_HEADS // DSA_KV_HEADS

LANES = 128
SUBLANES = 8
NEG = -(2.0 ** 100)
QK_SCALE = HEAD_DIM ** -0.5
VMEM_LIMIT = 56 * 1024 * 1024

TOKEN_TILE = 512
FOX_Q_TILE = 512
DSA_Q_TILE = 128
FOX_PAGES = 32
DSA_PAGES = 64
BISECT_PASSES = 16
SKIP_MARGIN = 110.0
NORM_SLACK = 1.05

F32 = jnp.float32
BF16 = jnp.bfloat16

_C_FOX = 0
_C_DSA = _C_FOX + FOX_Q + 2 * FOX_KV
_C_IQ = _C_DSA + DSA_Q + 2 * DSA_KV
_C_IK = _C_IQ + IDX_Q
_C_FF = _C_IK + LANES
_C_IW = _C_FF + LANES
_C_END = _C_IW + LANES


def _dot(a, b):
    return jnp.dot(a, b, preferred_element_type=F32)


def _dot_nt(a, b):
    return lax.dot_general(a, b, (((1,), (1,)), ((), ())), preferred_element_type=F32)


def _split3(x):
    hi = x.astype(BF16)
    r1 = x - hi.astype(F32)
    mid = r1.astype(BF16)
    lo = (r1 - mid.astype(F32)).astype(BF16)
    return hi, mid, lo


def _lane_iota(shape):
    return lax.broadcasted_iota(jnp.int32, shape, len(shape) - 1)


def _row_iota(shape):
    return lax.broadcasted_iota(jnp.int32, shape, len(shape) - 2)


def _rope(x, cos, sa, sb):
    w = x.shape[1]
    reps = w // LANES
    if reps > 1:
        cos = jnp.concatenate([cos] * reps, axis=1)
        sa = jnp.concatenate([sa] * reps, axis=1)
        sb = jnp.concatenate([sb] * reps, axis=1)
    up = pltpu.roll(x, w - 8, 1)
    dn = pltpu.roll(x, 8, 1)
    return x * cos + up * sa + dn * sb


def _rmsnorm(x, g):
    ms = jnp.mean(x * x, axis=-1, keepdims=True)
    return x * lax.rsqrt(ms + NORM_EPS) * g


def _proj_body(x_ref, g_ref, w_ref, b_ref, rope_ref, *refs, prompt, blocks_per_seq):
    fq_ref, dq_ref = refs[:2]
    tm = x_ref.shape[0]
    n = _rmsnorm(x_ref[...], g_ref[...]).astype(BF16)
    cos, sa, sb = rope_ref[0], rope_ref[1], rope_ref[2]

    def mm(lo, hi):
        return _dot(n, w_ref[:, lo:hi])

    fq = mm(_C_FOX, _C_FOX + FOX_Q)
    fq_ref[...] = (fq * QK_SCALE).astype(BF16)
    fk = mm(_C_FOX + FOX_Q, _C_FOX + FOX_Q + FOX_KV)
    fv = mm(_C_FOX + FOX_Q + FOX_KV, _C_DSA)
    dq = _rope(mm(_C_DSA, _C_DSA + DSA_Q), cos, sa, sb)
    dq_ref[...] = (dq * QK_SCALE).astype(BF16)
    dk = _rope(mm(_C_DSA + DSA_Q, _C_DSA + DSA_Q + DSA_KV), cos, sa, sb)
    dv = mm(_C_DSA + DSA_Q + DSA_KV, _C_IQ)
    iq = _rope(mm(_C_IQ, _C_IK), cos, sa, sb) * (IDX_DIM ** -0.5)
    ikg = _rope(mm(_C_IK, _C_FF), cos, sa, sb)
    ff = mm(_C_FF, _C_IW) + b_ref[...]
    lf = jnp.minimum(ff, 0.0) - jnp.log(1.0 + jnp.exp(-jnp.abs(ff)))
    iw = mm(_C_IW, _C_END) * (IDX_HEADS ** -0.5)

    if not prompt:
        fk_ref, fv_ref, lf_ref, dk_ref, dv_ref, ik_ref, iq_ref, iw_ref = refs[2:]
        fk_ref[...] = fk
        fv_ref[...] = fv
        lf_ref[...] = lf[:, :FOX_HEADS]
        dk_ref[...] = dk
        dv_ref[...] = dv
        ik_ref[...] = ikg[:, :IDX_DIM]
        iq_ref[...] = iq.astype(BF16)
        iw_ref[...] = iw[:, :IDX_HEADS]
        return
    (fkx_ref, fvx_ref, lfx_ref, dkx_ref, dvx_ref, ikx_ref,
     negc_ref, fkt_ref, fvd_ref, dkt_ref, dvd_ref, iqt_ref, iwt_ref, ikb_ref, fkn_ref, carry_ref) = refs[2:]
    low = _lane_iota((tm, LANES)) < HEAD_DIM

    fk_t, dk_t, ik_t = fk.T, dk.T, ikg.T
    fkx_ref[0] = fk_t
    fvx_ref[0] = fv.T
    lfx_ref[0] = lf.T[:FOX_HEADS, :]
    dkx_ref[0] = dk_t
    dvx_ref[0] = dv.T
    ikx_ref[0] = ik_t[:IDX_DIM, :]

    @pl.when(pl.program_id(0) % blocks_per_seq == 0)
    def _():
        carry_ref[...] = jnp.zeros_like(carry_ref)

    tri = (_row_iota((tm, tm)) >= _lane_iota((tm, tm))).astype(BF16)
    hi, mid, lo = _split3(lf)
    csum = _dot(tri, hi) + _dot(tri, mid) + _dot(tri, lo) + carry_ref[0:1, :]
    carry_ref[0:1, :] = csum[tm - 1:tm, :]
    negc_ref[0] = -(csum.T[:FOX_HEADS, :])

    grp = ((_row_iota((FOX_KV, LANES)) >> (HEAD_DIM.bit_length() - 1))
           == (_lane_iota((FOX_KV, LANES)) >> (FOX_GROUPS.bit_length() - 1)))
    fkn_ref[0] = jnp.max(_dot((fk * fk).astype(BF16), grp.astype(BF16)), axis=0, keepdims=True)

    ones_lane = jnp.where(_lane_iota((tm, LANES)) == HEAD_DIM, 1.0, 0.0)

    def value_tiles(pair):
        swap = pltpu.roll(pair, HEAD_DIM, 1)
        return jnp.where(low, pair, ones_lane).astype(BF16), jnp.where(low, swap, ones_lane).astype(BF16)

    fkt = fk_t.astype(BF16)
    for h in range(FOX_KV_HEADS):
        kt = fkt[h * HEAD_DIM:(h + 1) * HEAD_DIM, :]
        fkt_ref[h, 0] = jnp.concatenate([kt, kt], axis=0)
    for p in range(FOX_KV_HEADS // 2):
        fvd_ref[2 * p], fvd_ref[2 * p + 1] = value_tiles(fv[:, p * LANES:(p + 1) * LANES])

    dkt = dk_t.astype(BF16)
    dv_tiles = value_tiles(dv)
    for h in range(DSA_KV_HEADS):
        kt = dkt[h * HEAD_DIM:(h + 1) * HEAD_DIM, :]
        dkt_ref[h, 0] = jnp.concatenate([kt, kt, kt, kt], axis=0)
        dvd_ref[h] = dv_tiles[h]

    iqt_ref[...] = iq.T.astype(BF16)
    iwt_ref[...] = iw.T[:IDX_HEADS, :]
    ikb_ref[0] = ikg[:, :IDX_DIM].astype(BF16)


def _proj(x2d, g1, w1, bias, rope_tab, *, prompt, seq_len, tm):
    m, d = x2d.shape
    nblk = m // tm
    npos = rope_tab.shape[1] // tm
    row = lambda i: (i, 0)
    col = lambda i: (0, i)
    outs = [
        (jax.ShapeDtypeStruct((m, FOX_Q), BF16), pl.BlockSpec((tm, FOX_Q), row)),
        (jax.ShapeDtypeStruct((m, DSA_Q), BF16), pl.BlockSpec((tm, DSA_Q), row)),
    ]
    cache_widths = (FOX_KV, FOX_KV, FOX_HEADS, DSA_KV, DSA_KV, IDX_DIM)
    scratch = []
    if prompt:
        bps = seq_len // tm
        feat_major = lambda i: (i // bps, 0, i % bps)
        outs += [(jax.ShapeDtypeStruct((m // seq_len, w, seq_len), F32), pl.BlockSpec((1, w, tm), feat_major))
                 for w in cache_widths]
        outs += [
            (jax.ShapeDtypeStruct((nblk, FOX_HEADS, tm), F32), pl.BlockSpec((1, FOX_HEADS, tm), lambda i: (i, 0, 0))),
            (jax.ShapeDtypeStruct((FOX_KV_HEADS, nblk, 2 * HEAD_DIM, tm), BF16),
             pl.BlockSpec((FOX_KV_HEADS, 1, 2 * HEAD_DIM, tm), lambda i: (0, i, 0, 0))),
            (jax.ShapeDtypeStruct((FOX_KV_HEADS, m, 2 * HEAD_DIM), BF16),
             pl.BlockSpec((FOX_KV_HEADS, tm, 2 * HEAD_DIM), lambda i: (0, i, 0))),
            (jax.ShapeDtypeStruct((DSA_KV_HEADS, nblk, 4 * HEAD_DIM, tm), BF16),
             pl.BlockSpec((DSA_KV_HEADS, 1, 4 * HEAD_DIM, tm), lambda i: (0, i, 0, 0))),
            (jax.ShapeDtypeStruct((DSA_KV_HEADS, m, 2 * HEAD_DIM), BF16),
             pl.BlockSpec((DSA_KV_HEADS, tm, 2 * HEAD_DIM), lambda i: (0, i, 0))),
            (jax.ShapeDtypeStruct((IDX_Q, m), BF16), pl.BlockSpec((IDX_Q, tm), col)),
            (jax.ShapeDtypeStruct((IDX_HEADS, m), F32), pl.BlockSpec((IDX_HEADS, tm), col)),
            (jax.ShapeDtypeStruct((nblk, tm, IDX_DIM), BF16), pl.BlockSpec((1, tm, IDX_DIM), lambda i: (i, 0, 0))),
            (jax.ShapeDtypeStruct((nblk, 1, LANES), F32), pl.BlockSpec((1, 1, LANES), lambda i: (i, 0, 0))),
        ]
        scratch = [pltpu.VMEM((SUBLANES, LANES), F32)]
    else:
        outs += [(jax.ShapeDtypeStruct((m, w), F32), pl.BlockSpec((tm, w), row)) for w in cache_widths]
        outs += [
            (jax.ShapeDtypeStruct((m, IDX_Q), BF16), pl.BlockSpec((tm, IDX_Q), row)),
            (jax.ShapeDtypeStruct((m, IDX_HEADS), F32), pl.BlockSpec((tm, IDX_HEADS), row)),
        ]
    const = lambda i: (0, 0)
    return pl.pallas_call(
        functools.partial(_proj_body, prompt=prompt, blocks_per_seq=seq_len // tm),
        grid=(nblk,),
        in_specs=[
            pl.BlockSpec((tm, d), row),
            pl.BlockSpec((1, d), const),
            pl.BlockSpec((d, _C_END), const, pipeline_mode=pl.Buffered(1)),
            pl.BlockSpec((1, LANES), const),
            pl.BlockSpec((3, tm, LANES), lambda i: (0, i % npos, 0)),
        ],
        out_specs=[o[1] for o in outs],
        out_shape=[o[0] for o in outs],
        scratch_shapes=scratch,
        compiler_params=pltpu.CompilerParams(dimension_semantics=("arbitrary",), vmem_limit_bytes=VMEM_LIMIT),
        name="proj_prompt" if prompt else "proj_sample",
    )(x2d, g1, w1, bias, rope_tab)


def _softmax_step(s, m, acc, v):
    m_new = jnp.maximum(m, jnp.max(s, axis=-1, keepdims=True))
    acc = jnp.exp(m - m_new) * acc + _dot(jnp.exp(s - m_new).astype(BF16), v)
    return m_new, acc


def _softmax_init(rows):
    return jnp.full((rows, 1), NEG, F32), jnp.zeros((rows, LANES), F32)


def _head_outputs(acc, tq, groups):
    low = _lane_iota((tq, LANES)) < HEAD_DIM
    pairs = []
    for g in range(0, groups, 2):
        even, odd = (acc[k * tq:(k + 1) * tq] for k in (g, g + 1))
        even = even / even[:, HEAD_DIM:HEAD_DIM + 1]
        odd = odd / odd[:, HEAD_DIM:HEAD_DIM + 1]
        pairs.append(jnp.where(low, even, pltpu.roll(odd, HEAD_DIM, 1)))
    return pairs[0] if len(pairs) == 1 else jnp.concatenate(pairs, axis=1)


def _stack_heads(q, tq, groups):
    lane = _lane_iota((tq, groups * HEAD_DIM))
    zero = jnp.zeros_like(q)
    return jnp.concatenate(
        [jnp.where((lane >= g * HEAD_DIM) & (lane < (g + 1) * HEAD_DIM), q, zero) for g in range(groups)], axis=0)


def _fox_skip_count(q_ref, kn_ref, first_ref, last_ref, nfull):
    qf = q_ref[...].astype(F32)
    head_of = ((_row_iota((FOX_Q, LANES)) >> (HEAD_DIM.bit_length() - 1)) == _lane_iota((FOX_Q, LANES))).astype(BF16)
    qn2 = jnp.max(_dot((qf * qf).astype(BF16), head_of), axis=0, keepdims=True)
    kn2 = jnp.max(kn_ref[:, 0, :], axis=0, keepdims=True)
    bound = NORM_SLACK * jnp.sqrt(qn2 * kn2)
    thr = jnp.where(_lane_iota((1, LANES)) < FOX_HEADS, 2.0 * bound + SKIP_MARGIN, -1.0)
    gap = first_ref[0] - last_ref[:, 0, :]
    skip = jnp.min(jnp.where(gap > thr, 1.0, 0.0), axis=1, keepdims=True)
    blk = _row_iota(skip.shape)
    return jnp.sum(jnp.where(blk < nfull, skip, 0.0)).astype(jnp.int32)


def _fox_prompt_body(q_ref, kt_ref, v_ref, negc_ref, kn_ref, first_ref, last_ref, o_ref, *, tq, tk):
    i = pl.program_id(1)
    nfull = (i * tq) // tk
    nskip = _fox_skip_count(q_ref, kn_ref, first_ref, last_ref, nfull)
    g = FOX_GROUPS
    w = g * HEAD_DIM
    qpos = i * tq + (_row_iota((g * tq, tk)) & (tq - 1))
    kloc = _lane_iota((g * tq, tk))
    qs = [_stack_heads(q_ref[:, h * w:(h + 1) * w], tq, g) for h in range(FOX_KV_HEADS)]

    def step(j, carry, masked):
        nb = negc_ref[j]
        off = pl.multiple_of(j * tk, tk)
        out = []
        for h in range(FOX_KV_HEADS):
            bias = jnp.concatenate(
                [jnp.broadcast_to(nb[h * g + k:h * g + k + 1, :], (tq, tk)) for k in range(g)], axis=0)
            s = _dot(qs[h], kt_ref[h, j]) + bias
            if masked:
                s = jnp.where(j * tk + kloc <= qpos, s, NEG)
            out.append(_softmax_step(s, *carry[h], v_ref[h, pl.ds(off, tk), :]))
        return tuple(out)

    init = tuple(_softmax_init(g * tq) for _ in range(FOX_KV_HEADS))
    carry = lax.fori_loop(nskip, nfull, lambda j, c: step(j, c, False), init)
    carry = step(nfull, carry, True)
    for h in range(FOX_KV_HEADS):
        o_ref[:, h * w:(h + 1) * w] = _head_outputs(carry[h][1], tq, g).astype(BF16)


def _fox_prompt(fq, fkt, fvd, negc, fkn, *, batch, seq_len, tq, tk):
    m = fq.shape[0]
    nq = seq_len // tq
    nkb = seq_len // tk
    lane_pad = lambda a: jnp.pad(a, ((0, 0), (0, LANES - FOX_HEADS)))[:, None, :]
    a_last = lane_pad(negc[:, :, tk - 1])
    a_first = lane_pad(jnp.swapaxes(negc[:, :, ::tq], 1, 2).reshape(batch * nq, FOX_HEADS))
    return pl.pallas_call(
        functools.partial(_fox_prompt_body, tq=tq, tk=tk),
        grid=(batch, nq),
        in_specs=[
            pl.BlockSpec((tq, FOX_Q), lambda b, i: (b * nq + i, 0)),
            pl.BlockSpec((FOX_KV_HEADS, nkb, 2 * HEAD_DIM, tk), lambda b, i: (0, b, 0, 0), pipeline_mode=pl.Buffered(1)),
            pl.BlockSpec((FOX_KV_HEADS, seq_len, 2 * HEAD_DIM), lambda b, i: (0, b, 0), pipeline_mode=pl.Buffered(1)),
            pl.BlockSpec((nkb, FOX_HEADS, tk), lambda b, i: (b, 0, 0), pipeline_mode=pl.Buffered(1)),
            pl.BlockSpec((nkb, 1, LANES), lambda b, i: (b, 0, 0)),
            pl.BlockSpec((1, 1, LANES), lambda b, i: (b * nq + i, 0, 0)),
            pl.BlockSpec((nkb, 1, LANES), lambda b, i: (b, 0, 0)),
        ],
        out_specs=pl.BlockSpec((tq, FOX_Q), lambda b, i: (b * nq + i, 0)),
        out_shape=jax.ShapeDtypeStruct((m, FOX_Q), BF16),
        compiler_params=pltpu.CompilerParams(dimension_semantics=("arbitrary", "arbitrary"), vmem_limit_bytes=VMEM_LIMIT),
        name="fox_prompt",
    )(fq, fkt, fvd, negc, fkn, a_first, a_last)


def _topk_threshold(reduce_fn, idx_of, nvalid, ksel, n_idx):
    kf = jnp.float32(ksel)
    inf = jnp.float32(jnp.inf)
    one, zero = jnp.float32(1.0), jnp.float32(0.0)
    shape = nvalid.shape

    def count_ge(t):
        return reduce_fn(lambda c, blk: jnp.where(blk >= t, one, zero), "sum")

    row_max = reduce_fn(lambda c, blk: blk, "max")
    row_min = reduce_fn(lambda c, blk: jnp.where(blk > -inf, blk, inf), "min")
    all_sel = jnp.where(nvalid <= kf, one, zero)

    def bisect(_, st):
        lo, clo, hi, chi, done = st
        t = jnp.where(hi == inf, row_max, 0.5 * (lo + hi))
        cnt = count_ge(t)
        upd = (done < 0.5) & (t > lo) & (t < hi)
        ge = cnt >= kf
        lo = jnp.where(upd & ge, t, lo)
        clo = jnp.where(upd & ge, cnt, clo)
        hi = jnp.where(upd & (cnt < kf), t, hi)
        chi = jnp.where(upd & (cnt < kf), cnt, chi)
        return lo, clo, hi, chi, jnp.where(clo == kf, one, done)

    st = (row_min, nvalid, jnp.full(shape, inf, F32), jnp.zeros(shape, F32), all_sel)
    st = lax.fori_loop(0, BISECT_PASSES, bisect, st)

    def snap(st):
        lo, clo, hi, chi, done = st
        v = reduce_fn(lambda c, blk: jnp.where(blk < hi, blk, -inf), "max")
        cv = count_ge(v)
        fin = (done < 0.5) & (cv >= kf)
        go = (done < 0.5) & (cv < kf)
        lo = jnp.where(fin, v, lo)
        clo = jnp.where(fin, cv, clo)
        hi = jnp.where(go, v, hi)
        chi = jnp.where(go, cv, chi)
        return lo, clo, hi, chi, jnp.where(fin, one, done)

    lo, clo, hi, chi, done = lax.while_loop(lambda st: jnp.min(st[4]) < 0.5, snap, st)

    thr = jnp.where(all_sel > 0.5, -inf, lo)
    need_cut = jnp.where((all_sel < 0.5) & (clo > kf), one, zero)
    need = kf - chi
    big = jnp.float32(n_idx)

    def cut_search():
        def body(_, st):
            jlo, jhi = st
            mid = jnp.floor(0.5 * (jlo + jhi))
            cnt = reduce_fn(lambda c, blk: jnp.where((blk == thr) & (idx_of(c) <= mid), one, zero), "sum")
            ok = cnt >= need
            return jnp.where(ok, jlo, mid), jnp.where(ok, mid, jhi)

        npass = int(np.ceil(np.log2(n_idx))) + 1
        _, jhi = lax.fori_loop(0, npass, body, (jnp.full(shape, -1.0, F32), jnp.full(shape, n_idx - 1.0, F32)))
        return jhi

    jcut = lax.cond(jnp.max(need_cut) > 0.5, cut_search, lambda: jnp.full(shape, -1.0, F32))
    jcut = jnp.where(need_cut > 0.5, jcut, jnp.where(all_sel > 0.5, -1.0, big))
    return thr, jcut


_REDUCE_OPS = {"sum": (jnp.add, 0.0, jnp.sum), "max": (jnp.maximum, -np.inf, jnp.max), "min": (jnp.minimum, np.inf, jnp.min)}


def _tree(parts, op):
    while len(parts) > 1:
        parts = [op(parts[k], parts[k + 1]) if k + 1 < len(parts) else parts[k] for k in range(0, len(parts), 2)]
    return parts[0]


def _row_major_reducer(s_ref, nch):
    _, rows, ch = s_ref.shape

    def reduce_fn(elem_fn, kind):
        op, init, lane_reduce = _REDUCE_OPS[kind]

        def body(c, acc):
            x = elem_fn(c, s_ref[c])
            return op(acc, _tree([x[:, k * LANES:(k + 1) * LANES] for k in range(ch // LANES)], op))

        acc = jnp.full((rows, LANES), init, F32)
        if isinstance(nch, int):
            for c in range(nch):
                acc = body(c, acc)
        else:
            acc = lax.fori_loop(0, nch, body, acc)
        return lane_reduce(acc, axis=1, keepdims=True)

    return reduce_fn


def _lane_major_reducer(s_ref, nch):
    _, ch, _ = s_ref.shape

    def reduce_fn(elem_fn, kind):
        op, init, sub_reduce = _REDUCE_OPS[kind]

        def body(c, acc):
            x = elem_fn(c, s_ref[c])
            return op(acc, _tree([x[k * SUBLANES:(k + 1) * SUBLANES, :] for k in range(ch // SUBLANES)], op))

        acc = lax.fori_loop(0, nch, body, jnp.full((SUBLANES, LANES), init, F32))
        return sub_reduce(acc, axis=0, keepdims=True)

    return reduce_fn


def _dsa_prompt_body(iqt_ref, iwt_ref, dq_ref, ikb_ref, dkt_ref, dv_ref, o_ref, s_ref, bias_ref, *, tq, ch, ksel):
    i = pl.program_id(1)
    nch = (i * tq + tq - 1) // ch + 1
    g = DSA_GROUPS
    w = g * HEAD_DIM

    iqt = iqt_ref[...]
    rhs = [jnp.concatenate([iqt[(2 * p) * IDX_DIM:(2 * p + 1) * IDX_DIM, :], iqt[(2 * p + 1) * IDX_DIM:(2 * p + 2) * IDX_DIM, :]],
                           axis=1) for p in range(IDX_HEADS // 2)]
    iwt = iwt_ref[...]
    qpos = i * tq + _lane_iota((ch, tq))
    kloc = _row_iota((ch, tq))

    def score_chunk(c, _):
        keys = ikb_ref[c]
        acc = None
        for p in range(IDX_HEADS // 2):
            d = jnp.maximum(_dot(keys, rhs[p]), 0.0)
            term = d[:, :tq] * iwt[2 * p:2 * p + 1, :] + d[:, tq:] * iwt[2 * p + 1:2 * p + 2, :]
            acc = term if acc is None else acc + term
        s_ref[c] = jnp.where(c * ch + kloc <= qpos, acc, -jnp.inf)
        return 0

    lax.fori_loop(0, nch, score_chunk, 0)

    nvalid = (i * tq + _lane_iota((1, tq)) + 1).astype(F32)
    idx_of = lambda c: (c * ch + kloc).astype(F32)
    thr, jcut = _topk_threshold(_lane_major_reducer(s_ref, nch), idx_of, nvalid, ksel, s_ref.shape[0] * ch)

    def write_bias(c, _):
        blk = s_ref[c]
        sel = (blk > thr) | ((blk == thr) & (idx_of(c) <= jcut))
        bias_ref[c] = jnp.where(sel, 0.0, NEG).T
        return 0

    lax.fori_loop(0, nch, write_bias, 0)

    qs = [_stack_heads(dq_ref[:, h * w:(h + 1) * w], tq, g) for h in range(DSA_KV_HEADS)]

    def step(c, carry):
        b = bias_ref[c]
        bias = jnp.concatenate([b] * g, axis=0)
        off = pl.multiple_of(c * ch, ch)
        return tuple(_softmax_step(_dot(qs[h], dkt_ref[h, c]) + bias, *carry[h], dv_ref[h, pl.ds(off, ch), :])
                     for h in range(DSA_KV_HEADS))

    carry = lax.fori_loop(0, nch, step, tuple(_softmax_init(g * tq) for _ in range(DSA_KV_HEADS)))
    for h in range(DSA_KV_HEADS):
        o_ref[:, h * w:(h + 1) * w] = _head_outputs(carry[h][1], tq, g).astype(BF16)


def _dsa_prompt(iqt, iwt, dq, ikb, dkt, dvd, *, batch, seq_len, tq, ch):
    m = dq.shape[0]
    nq = seq_len // tq
    nkb = seq_len // ch
    ksel = min(TOPK_MAX, seq_len // 4)
    row = lambda b, i: (b * nq + i, 0)
    col = lambda b, i: (0, b * nq + i)
    return pl.pallas_call(
        functools.partial(_dsa_prompt_body, tq=tq, ch=ch, ksel=ksel),
        grid=(batch, nq),
        in_specs=[
            pl.BlockSpec((IDX_Q, tq), col),
            pl.BlockSpec((IDX_HEADS, tq), col),
            pl.BlockSpec((tq, DSA_Q), row),
            pl.BlockSpec((nkb, ch, IDX_DIM), lambda b, i: (b, 0, 0), pipeline_mode=pl.Buffered(1)),
            pl.BlockSpec((DSA_KV_HEADS, nkb, 4 * HEAD_DIM, ch), lambda b, i: (0, b, 0, 0), pipeline_mode=pl.Buffered(1)),
            pl.BlockSpec((DSA_KV_HEADS, seq_len, 2 * HEAD_DIM), lambda b, i: (0, b, 0), pipeline_mode=pl.Buffered(1)),
        ],
        out_specs=pl.BlockSpec((tq, DSA_Q), row),
        out_shape=jax.ShapeDtypeStruct((m, DSA_Q), BF16),
        scratch_shapes=[pltpu.VMEM((nkb, ch, tq), F32), pltpu.VMEM((nkb, tq, ch), F32)],
        compiler_params=pltpu.CompilerParams(dimension_semantics=("arbitrary", "arbitrary"), vmem_limit_bytes=VMEM_LIMIT),
        name="dsa_prompt",
    )(iqt, iwt, dq, ikb, dkt, dvd)


def _page_ring(pt_ref, pools, bufs, sem, layer, pages):
    ng = pl.num_programs(1)
    step = pl.program_id(0) * ng + pl.program_id(1)
    last = pl.num_programs(0) * ng - 1

    def copies(s):
        b, j, slot = lax.div(s, ng), lax.rem(s, ng), lax.rem(s, 2)
        return [pltpu.make_async_copy(pool.at[layer, pt_ref[b, j * pages + k]], buf.at[slot, k], sem.at[slot, a])
                for a, (pool, buf) in enumerate(zip(pools, bufs)) for k in range(pages)]

    @pl.when(step == 0)
    def _():
        for c in copies(step):
            c.start()

    @pl.when(step < last)
    def _():
        for c in copies(step + 1):
            c.start()

    for c in copies(step):
        c.wait()
    return lax.rem(step, 2)


def _page_ring_scratch(features, pages):
    return [pltpu.VMEM((2, pages, f, PAGE_SIZE), F32) for f in features] + [pltpu.SemaphoreType.DMA((2, len(features)))]


def _page_tiles(buf, slot):
    return jnp.concatenate([buf[slot, k] for k in range(buf.shape[1])], axis=1).astype(BF16)


def _sample_softmax_step(s, vt, m_ref, l_ref, acc_ref):
    m = m_ref[...]
    m_new = jnp.maximum(m, jnp.max(s, axis=-1, keepdims=True))
    alpha = jnp.exp(m - m_new)
    p = jnp.exp(s - m_new)
    m_ref[...] = m_new
    l_ref[...] = alpha * l_ref[...] + jnp.sum(p, axis=-1, keepdims=True)
    acc_ref[...] = alpha * acc_ref[...] + _dot_nt(p.astype(BF16), vt)


def _fox_sample_body(pt_ref, q_ref, kn_ref, vn_ref, lfn_ref, k_hbm, v_hbm, lf_hbm, o_ref, m_ref, l_ref, acc_ref, pre_ref,
                     kbuf, vbuf, lbuf, sem, *, layer, dec_seq):
    np_ = FOX_PAGES
    j = pl.program_id(1)
    rows = dec_seq * FOX_HEADS
    slot = _page_ring(pt_ref, (k_hbm, v_hbm, lf_hbm), (kbuf, vbuf, lbuf), sem, layer, np_)

    @pl.when(j == 0)
    def _():
        m_ref[...] = jnp.full_like(m_ref, NEG)
        l_ref[...] = jnp.zeros_like(l_ref)
        acc_ref[...] = jnp.zeros_like(acc_ref)
        pre_ref[...] = jnp.zeros_like(pre_ref)

    triu = (_row_iota((PAGE_SIZE, PAGE_SIZE)) <= _lane_iota((PAGE_SIZE, PAGE_SIZE))).astype(BF16)

    def cum_keys(lf):
        r = lf.shape[0]
        parts = _dot(jnp.concatenate(_split3(lf), axis=0), triu)
        return parts[0:r] + parts[r:2 * r] + parts[2 * r:3 * r]

    q = q_ref[0]
    nrow = np_ * FOX_HEADS
    within = cum_keys(jnp.concatenate([lbuf[slot, k] for k in range(np_)], axis=0))
    r, c = _row_iota((nrow, nrow)), _lane_iota((nrow, nrow))
    earlier = (((r ^ c) & (FOX_HEADS - 1)) == 0) & (c < r)
    totals = jnp.broadcast_to(within[:, PAGE_SIZE - 1:PAGE_SIZE], (nrow, PAGE_SIZE))
    carried = _dot(earlier.astype(BF16), jnp.concatenate(_split3(totals), axis=1))
    carried = carried[:, 0:PAGE_SIZE] + carried[:, PAGE_SIZE:2 * PAGE_SIZE] + carried[:, 2 * PAGE_SIZE:3 * PAGE_SIZE]
    cum = within + carried + jnp.concatenate([pre_ref[...]] * np_, axis=0)
    pre_ref[...] = jnp.broadcast_to(cum[nrow - FOX_HEADS:nrow, PAGE_SIZE - 1:PAGE_SIZE], pre_ref.shape)
    c_all = jnp.concatenate([cum[k * FOX_HEADS:(k + 1) * FOX_HEADS, :] for k in range(np_)], axis=1)
    s = _dot(q, _page_tiles(kbuf, slot)) - jnp.concatenate([c_all] * dec_seq, axis=0)
    _sample_softmax_step(s, _page_tiles(vbuf, slot), m_ref, l_ref, acc_ref)

    @pl.when(j == pl.num_programs(1) - 1)
    def _():
        c = cum_keys(lfn_ref[0]) + pre_ref[...]
        s = _dot(q, kn_ref[0].astype(BF16)) - jnp.concatenate([c] * dec_seq, axis=0)
        tok = _row_iota((rows, PAGE_SIZE)) >> (FOX_HEADS.bit_length() - 1)
        s = jnp.where(_lane_iota((rows, PAGE_SIZE)) <= tok, s, NEG)
        _sample_softmax_step(s, vn_ref[0].astype(BF16), m_ref, l_ref, acc_ref)
        o_ref[0] = acc_ref[...] / l_ref[...]


def _fox_sample(page_table, q_rows, k_new, v_new, lf_new, k_pool, v_pool, lf_pool, *, layer, dec_seq):
    db, n_pages = page_table.shape
    ng = n_pages // FOX_PAGES
    rows = dec_seq * FOX_HEADS
    per_b = lambda b, j, pt: (b, 0, 0)
    grid_spec = pltpu.PrefetchScalarGridSpec(
        num_scalar_prefetch=1,
        grid=(db, ng),
        in_specs=[
            pl.BlockSpec((1, rows, FOX_KV), per_b),
            pl.BlockSpec((1, FOX_KV, PAGE_SIZE), per_b),
            pl.BlockSpec((1, FOX_KV, PAGE_SIZE), per_b),
            pl.BlockSpec((1, FOX_HEADS, PAGE_SIZE), per_b),
        ] + [pl.BlockSpec(memory_space=pl.ANY)] * 3,
        out_specs=pl.BlockSpec((1, rows, FOX_KV), per_b),
        scratch_shapes=[pltpu.VMEM((rows, 1), F32), pltpu.VMEM((rows, 1), F32), pltpu.VMEM((rows, FOX_KV), F32),
                        pltpu.VMEM((FOX_HEADS, PAGE_SIZE), F32)]
        + _page_ring_scratch((FOX_KV, FOX_KV, FOX_HEADS), FOX_PAGES),
    )
    return pl.pallas_call(
        functools.partial(_fox_sample_body, layer=layer, dec_seq=dec_seq),
        grid_spec=grid_spec,
        out_shape=jax.ShapeDtypeStruct((db, rows, FOX_KV), F32),
        compiler_params=pltpu.CompilerParams(dimension_semantics=("arbitrary", "arbitrary"), vmem_limit_bytes=VMEM_LIMIT),
        name="fox_sample",
    )(page_table, q_rows, k_new, v_new, lf_new, k_pool, v_pool, lf_pool)


def _idx_sample_body(pt_ref, q_ref, w_ref, kn_ref, k_hbm, bias_ref, s_ref, kbuf, sem, *, layer, dec_seq, past, ksel):
    np_ = DSA_PAGES
    j = pl.program_id(1)
    slot = _page_ring(pt_ref, (k_hbm,), (kbuf,), sem, layer, np_)
    ng = pl.num_programs(1)
    ch = np_ * PAGE_SIZE
    q = q_ref[0]
    w = w_ref[0]

    def scores(keys_t):
        d = jnp.maximum(_dot(q, keys_t), 0.0) * w
        shape = (SUBLANES, d.shape[1])
        tok = _row_iota(shape)
        out = jnp.full(shape, -jnp.inf, F32)
        for t in range(dec_seq):
            per_tok = jnp.sum(d[t * IDX_HEADS:(t + 1) * IDX_HEADS], axis=0, keepdims=True)
            out = jnp.where(tok == t, jnp.broadcast_to(per_tok, shape), out)
        return out

    s_ref[j] = scores(_page_tiles(kbuf, slot))

    @pl.when(j == ng - 1)
    def _():
        sn = scores(kn_ref[0].astype(BF16))
        sn = jnp.where(_lane_iota((SUBLANES, PAGE_SIZE)) <= _row_iota((SUBLANES, PAGE_SIZE)), sn, -jnp.inf)
        s_ref[ng] = jnp.concatenate([sn, jnp.full((SUBLANES, ch - PAGE_SIZE), -jnp.inf, F32)], axis=1)
        tok = _row_iota((SUBLANES, 1))
        nvalid = jnp.where(tok < dec_seq, (past + 1 + tok).astype(F32), 0.0)
        idx_of = lambda c: (c * ch + _lane_iota((SUBLANES, ch))).astype(F32)
        thr, jcut = _topk_threshold(_row_major_reducer(s_ref, ng + 1), idx_of, nvalid, ksel, s_ref.shape[0] * ch)
        for c in range(s_ref.shape[0]):
            blk = s_ref[c]
            sel = (blk > thr) | ((blk == thr) & (idx_of(c) <= jcut))
            bias_ref[0, c] = jnp.where(sel, 0.0, NEG)


def _idx_sample(page_table, iq_rows, iw_rows, ik_new, ik_pool, *, layer, dec_seq, ksel):
    db, n_pages = page_table.shape
    ng = n_pages // DSA_PAGES
    ch = DSA_PAGES * PAGE_SIZE
    rows = dec_seq * IDX_HEADS
    per_b = lambda b, j, pt: (b, 0, 0)
    grid_spec = pltpu.PrefetchScalarGridSpec(
        num_scalar_prefetch=1,
        grid=(db, ng),
        in_specs=[
            pl.BlockSpec((1, rows, IDX_DIM), per_b),
            pl.BlockSpec((1, rows, 1), per_b),
            pl.BlockSpec((1, IDX_DIM, PAGE_SIZE), per_b),
            pl.BlockSpec(memory_space=pl.ANY),
        ],
        out_specs=pl.BlockSpec((1, ng + 1, SUBLANES, ch), lambda b, j, pt: (b, 0, 0, 0)),
        scratch_shapes=[pltpu.VMEM((ng + 1, SUBLANES, ch), F32)] + _page_ring_scratch((IDX_DIM,), DSA_PAGES),
    )
    return pl.pallas_call(
        functools.partial(_idx_sample_body, layer=layer, dec_seq=dec_seq, past=n_pages * PAGE_SIZE, ksel=ksel),
        grid_spec=grid_spec,
        out_shape=jax.ShapeDtypeStruct((db, ng + 1, SUBLANES, ch), F32),
        compiler_params=pltpu.CompilerParams(dimension_semantics=("arbitrary", "arbitrary"), vmem_limit_bytes=VMEM_LIMIT),
        name="idx_sample",
    )(page_table, iq_rows, iw_rows, ik_new, ik_pool)


def _dsa_sample_body(pt_ref, q_ref, bias_ref, kn_ref, vn_ref, k_hbm, v_hbm, o_ref, m_ref, l_ref, acc_ref, kbuf, vbuf, sem,
                     *, layer, dec_seq):
    np_ = DSA_PAGES
    j = pl.program_id(1)
    ng = pl.num_programs(1)
    slot = _page_ring(pt_ref, (k_hbm, v_hbm), (kbuf, vbuf), sem, layer, np_)
    rows = dec_seq * DSA_HEADS

    @pl.when(j == 0)
    def _():
        m_ref[...] = jnp.full_like(m_ref, NEG)
        l_ref[...] = jnp.zeros_like(l_ref)
        acc_ref[...] = jnp.zeros_like(acc_ref)

    def mask_rows(b):
        shape = (rows, b.shape[1])
        tok = _row_iota(shape) >> (DSA_HEADS.bit_length() - 1)
        out = jnp.broadcast_to(b[0:1, :], shape)
        for t in range(1, dec_seq):
            out = jnp.where(tok == t, jnp.broadcast_to(b[t:t + 1, :], shape), out)
        return out

    q = q_ref[0]
    s = _dot(q, _page_tiles(kbuf, slot)) + mask_rows(bias_ref[0, j])
    _sample_softmax_step(s, _page_tiles(vbuf, slot), m_ref, l_ref, acc_ref)

    @pl.when(j == ng - 1)
    def _():
        s = _dot(q, kn_ref[0].astype(BF16)) + mask_rows(bias_ref[0, ng, :, 0:PAGE_SIZE])
        _sample_softmax_step(s, vn_ref[0].astype(BF16), m_ref, l_ref, acc_ref)
        o_ref[0] = acc_ref[...] / l_ref[...]


def _dsa_sample(page_table, q_rows, bias, k_new, v_new, k_pool, v_pool, *, layer, dec_seq):
    db, n_pages = page_table.shape
    ng = n_pages // DSA_PAGES
    ch = DSA_PAGES * PAGE_SIZE
    rows = dec_seq * DSA_HEADS
    per_b = lambda b, j, pt: (b, 0, 0)
    grid_spec = pltpu.PrefetchScalarGridSpec(
        num_scalar_prefetch=1,
        grid=(db, ng),
        in_specs=[
            pl.BlockSpec((1, rows, DSA_KV), per_b),
            pl.BlockSpec((1, ng + 1, SUBLANES, ch), lambda b, j, pt: (b, 0, 0, 0)),
            pl.BlockSpec((1, DSA_KV, PAGE_SIZE), per_b),
            pl.BlockSpec((1, DSA_KV, PAGE_SIZE), per_b),
        ] + [pl.BlockSpec(memory_space=pl.ANY)] * 2,
        out_specs=pl.BlockSpec((1, rows, DSA_KV), per_b),
        scratch_shapes=[pltpu.VMEM((rows, 1), F32), pltpu.VMEM((rows, 1), F32), pltpu.VMEM((rows, DSA_KV), F32)]
        + _page_ring_scratch((DSA_KV, DSA_KV), DSA_PAGES),
    )
    return pl.pallas_call(
        functools.partial(_dsa_sample_body, layer=layer, dec_seq=dec_seq),
        grid_spec=grid_spec,
        out_shape=jax.ShapeDtypeStruct((db, rows, DSA_KV), F32),
        compiler_params=pltpu.CompilerParams(dimension_semantics=("arbitrary", "arbitrary"), vmem_limit_bytes=VMEM_LIMIT),
        name="dsa_sample",
    )(page_table, q_rows, bias, k_new, v_new, k_pool, v_pool)


_FF_CHUNK = 1024


def _post_body(x_ref, of_ref, od_ref, g1_ref, wg_ref, wof_ref, wod_ref, wout_ref, g2_ref, wup_ref, wdn_ref, gf_ref, y_ref):
    x = x_ref[...]
    d = x.shape[1]
    n = _rmsnorm(x, g1_ref[...]).astype(BF16)
    gate_a = jax.nn.sigmoid(_dot(n, wg_ref[:, 0:d]))
    gate_b = jax.nn.sigmoid(_dot(n, wg_ref[:, d:2 * d]))
    mix = gate_a * _dot(of_ref[...], wof_ref[...]) + gate_b * _dot(od_ref[...], wod_ref[...])
    h = x + _dot(mix.astype(BF16), wout_ref[...])
    n2 = _rmsnorm(h, g2_ref[...]).astype(BF16)
    y = h
    for c in range(wup_ref.shape[1] // _FF_CHUNK):
        u = jnp.maximum(_dot(n2, wup_ref[:, c * _FF_CHUNK:(c + 1) * _FF_CHUNK]), 0.0)
        y = y + _dot((u * u).astype(BF16), wdn_ref[c * _FF_CHUNK:(c + 1) * _FF_CHUNK, :])
    y_ref[...] = _rmsnorm(y, gf_ref[...])


def _post(x2d, o_fox, o_dsa, g1, wg, wof, wod, wout, g2, wup, wdn, gf, *, tm):
    m, d = x2d.shape
    row = lambda i: (i, 0)
    const = lambda i: (0, 0)

    def whole(a):
        return pl.BlockSpec(a.shape, const, pipeline_mode=pl.Buffered(1))

    return pl.pallas_call(
        _post_body,
        grid=(m // tm,),
        in_specs=[pl.BlockSpec((tm, d), row), pl.BlockSpec((tm, FOX_Q), row), pl.BlockSpec((tm, DSA_Q), row),
                  whole(g1), whole(wg), whole(wof), whole(wod), whole(wout), whole(g2), whole(wup), whole(wdn), whole(gf)],
        out_specs=pl.BlockSpec((tm, d), row),
        out_shape=jax.ShapeDtypeStruct((m, d), F32),
        compiler_params=pltpu.CompilerParams(dimension_semantics=("arbitrary",), vmem_limit_bytes=VMEM_LIMIT),
        name="post",
    )(x2d, o_fox, o_dsa, g1, wg, wof, wod, wout, g2, wup, wdn, gf)


def _rope_tables(pos):
    r = HEAD_DIM // ROPE_FRACTION_DIV
    half = r // 2
    inv_freq = jnp.float32(ROPE_THETA) ** (-jnp.arange(half, dtype=F32) * (2.0 / r))
    ang = pos.astype(F32)[:, None] * inv_freq[None, :]
    cos, sin = jnp.cos(ang), jnp.sin(ang)
    p = pos.shape[0]
    one = jnp.ones((p, HEAD_DIM - r), F32)
    zero_r = jnp.zeros((p, HEAD_DIM - r), F32)
    zero_h = jnp.zeros((p, half), F32)
    c = jnp.concatenate([cos, cos, one], axis=1)
    sa = jnp.concatenate([-sin, zero_h, zero_r], axis=1)
    sb = jnp.concatenate([zero_h, sin, zero_r], axis=1)
    return jnp.stack([jnp.tile(t, (1, LANES // HEAD_DIM)) for t in (c, sa, sb)], axis=0)


def _pad_cols(w, width):
    return jnp.pad(w, ((0, 0), (0, width - w.shape[1])))


def _repack_w_in(w_in, d_model):
    sizes = (FOX_Q, FOX_KV, FOX_KV, FOX_HEADS, DSA_Q, DSA_KV, DSA_KV, IDX_Q, IDX_DIM, IDX_HEADS, d_model, d_model)
    cuts = [int(c) for c in np.cumsum(sizes)[:-1]]
    fq, fk, fv, ff, dq, dk, dv, iq, ik, iw, ga, gb = jnp.split(w_in, cuts, axis=-1)
    w1 = jnp.concatenate([fq, fk, fv, dq, dk, dv, iq, _pad_cols(ik, LANES), _pad_cols(ff, LANES), _pad_cols(iw, LANES)], axis=1)
    return w1.astype(BF16), jnp.concatenate([ga, gb], axis=1).astype(BF16)


def _new_page(x, db, dec_seq):
    xt = jnp.swapaxes(x.reshape(db, dec_seq, x.shape[1]), 1, 2)
    return jnp.pad(xt, ((0, 0), (0, 0), (0, PAGE_SIZE - dec_seq)))


def _head_rows(x, db, heads, kv_heads):
    g = heads // kv_heads
    xh = x.reshape(db, -1, kv_heads, g, 1, HEAD_DIM)
    eye = jnp.eye(kv_heads, dtype=x.dtype).reshape(1, 1, kv_heads, 1, kv_heads, 1)
    return (xh * eye).reshape(db, -1, kv_heads * HEAD_DIM)


def _own_features(o, heads, kv_heads):
    db = o.shape[0]
    g = heads // kv_heads
    t = o.shape[1] // heads
    oh = o.reshape(db, t, kv_heads, g, kv_heads, HEAD_DIM)
    idx = jnp.arange(kv_heads)
    return oh[:, :, idx, :, idx, :].transpose(1, 2, 0, 3, 4).reshape(db * t, heads * HEAD_DIM)


def _feature_major(pool):
    lead = pool.shape[:2]
    return jnp.moveaxis(pool.reshape(lead + (PAGE_SIZE, -1)), 2, 3)


def _token_major(xt, tail):
    b, _, t = xt.shape
    return jnp.moveaxis(xt, 1, 2).reshape((1, b, t) + tail)


def kernel(x_prompt, x_sample, cache_fox_k, cache_fox_v, cache_fox_logf, cache_dsa_k, cache_dsa_v, cache_idx_k,
           page_table, norm1_g, w_in, b_forget, w_o_fox, w_o_dsa, w_out, norm2_g, w_up, w_down, final_norm_g):
    batch, seq_len, d = x_prompt.shape
    db, dec_seq, _ = x_sample.shape
    depth = norm1_g.shape[0]
    n_pages = page_table.shape[1]
    past = n_pages * PAGE_SIZE
    assert dec_seq <= SUBLANES and dec_seq & (dec_seq - 1) == 0
    assert n_pages % FOX_PAGES == 0 and n_pages % DSA_PAGES == 0

    tm_p = min(TOKEN_TILE, seq_len)
    tm_s = db * dec_seq
    rope_p = _rope_tables(jnp.arange(seq_len))
    rope_s = _rope_tables(jnp.tile(past + jnp.arange(dec_seq), db))

    fox_k_pool, fox_v_pool, fox_lf_pool, dsa_k_pool, dsa_v_pool, idx_k_pool = (
        _feature_major(c) for c in (cache_fox_k, cache_fox_v, cache_fox_logf, cache_dsa_k, cache_dsa_v, cache_idx_k))

    xp = x_prompt.reshape(batch * seq_len, d)
    xs = x_sample.reshape(db * dec_seq, d)
    rows_p, rows_s = [], []
    for l in range(depth):
        w1, wg = _repack_w_in(w_in[l], d)
        bias = _pad_cols(b_forget[l][None, :], LANES)
        g1 = norm1_g[l][None, :]
        g2 = norm2_g[l][None, :]
        gf = final_norm_g[None, :] if l == depth - 1 else jnp.ones((1, d), F32)
        post_w = (g1, wg, w_o_fox[l].astype(BF16), w_o_dsa[l].astype(BF16), w_out[l].astype(BF16), g2,
                  w_up[l].astype(BF16), w_down[l].astype(BF16), gf)

        (fq, dq, fk, fv, lf, dk, dv, ik, negc, fkt, fvd, dkt, dvd, iqt, iwt, ikb, fkn) = _proj(
            xp, g1, w1, bias, rope_p, prompt=True, seq_len=seq_len, tm=tm_p)
        o_fox = _fox_prompt(fq, fkt, fvd, negc, fkn, batch=batch, seq_len=seq_len, tq=min(FOX_Q_TILE, seq_len), tk=tm_p)
        o_dsa = _dsa_prompt(iqt, iwt, dq, ikb, dkt, dvd, batch=batch, seq_len=seq_len, tq=min(DSA_Q_TILE, seq_len), ch=tm_p)
        xp = _post(xp, o_fox, o_dsa, *post_w, tm=tm_p)
        rows_p.append((fk, fv, lf, dk, dv, ik))

        (fq, dq, fk, fv, lf, dk, dv, ik, iq, iw) = _proj(xs, g1, w1, bias, rope_s, prompt=False, seq_len=tm_s, tm=tm_s)
        o_fox = _fox_sample(
            page_table, _head_rows(fq, db, FOX_HEADS, FOX_KV_HEADS), _new_page(fk, db, dec_seq), _new_page(fv, db, dec_seq),
            _new_page(lf, db, dec_seq), fox_k_pool, fox_v_pool, fox_lf_pool, layer=l, dec_seq=dec_seq)
        ksel = min(TOPK_MAX, (past + dec_seq) // 4)
        sel_bias = _idx_sample(
            page_table, iq.reshape(db, dec_seq * IDX_HEADS, IDX_DIM), iw.reshape(db, dec_seq * IDX_HEADS, 1),
            _new_page(ik, db, dec_seq), idx_k_pool, layer=l, dec_seq=dec_seq, ksel=ksel)
        o_dsa = _dsa_sample(
            page_table, _head_rows(dq, db, DSA_HEADS, DSA_KV_HEADS), sel_bias, _new_page(dk, db, dec_seq),
            _new_page(dv, db, dec_seq), dsa_k_pool, dsa_v_pool, layer=l, dec_seq=dec_seq)
        xs = _post(xs, _own_features(o_fox, FOX_HEADS, FOX_KV_HEADS).astype(BF16),
                   _own_features(o_dsa, DSA_HEADS, DSA_KV_HEADS).astype(BF16), *post_w, tm=tm_s)
        rows_s.append((fk, fv, lf, dk, dv, ik))

    tails = ((FOX_KV_HEADS, HEAD_DIM), (FOX_KV_HEADS, HEAD_DIM), (FOX_HEADS,), (DSA_KV_HEADS, HEAD_DIM),
             (DSA_KV_HEADS, HEAD_DIM), (IDX_DIM,))
    dtypes = tuple(c.dtype for c in (cache_fox_k, cache_fox_v, cache_fox_logf, cache_dsa_k, cache_dsa_v, cache_idx_k))
    new_p = tuple(jnp.concatenate([_token_major(r[k], tails[k]) for r in rows_p], axis=0).astype(dtypes[k])
                  for k in range(len(tails)))
    new_s = tuple(jnp.stack([r[k].reshape((db, dec_seq) + tails[k]) for r in rows_s], axis=0).astype(dtypes[k])
                  for k in range(len(tails)))
    return (xp.reshape(batch, seq_len, d), xs.reshape(db, dec_seq, d)) + new_p + new_s
```
